```python
import math
import jax, jax.numpy as jnp
from jax import lax
import numpy as np

D_MODEL = 1024
BATCH = 8
SEQ = 4096
DEPTH = 2
DEC_BATCH = 16
DEC_SEQ = 16
PAST_LEN = 2048

CHUNK = 64
Q_BLOCK = 128
N_A = DEPTH // 2
N_B = DEPTH - N_A
M_HEADS = 4
M_DK = D_MODEL // (2 * M_HEADS)
M_DV = D_MODEL // M_HEADS
M_QK = M_HEADS * M_DK
M_V = M_HEADS * M_DV
M_PROJ = 2 * M_QK + M_V + D_MODEL + 2 * M_HEADS
DA_HEADS = 8
DA_DH = D_MODEL // (2 * DA_HEADS)
DA_Q = DA_HEADS * 2 * DA_DH
FFN_HIDDEN = -(-8 * D_MODEL // (3 * 256)) * 256
N_BUCKETS = 32
MAX_DISTANCE = 128
EPS = 1e-6

kernel_name = 'yoco_mlstm_diffattn_stream_step'


def _rmsnorm(x, g):
    x32 = x.astype(jnp.float32)
    y = x32 * lax.rsqrt(jnp.mean(x32 * x32, axis=-1, keepdims=True) + EPS) * g.astype(jnp.float32)
    return y.astype(x.dtype)


def _t5_bucket(rel):
    half = N_BUCKETS // 2
    max_exact = half // 2
    ret = jnp.where(rel > 0, half, 0)
    n = jnp.abs(rel)
    large = max_exact + (jnp.log(jnp.maximum(n, 1).astype(jnp.float32) / max_exact)
                         / math.log(MAX_DISTANCE / max_exact) * (half - max_exact)).astype(jnp.int32)
    large = jnp.minimum(large, half - 1)
    return ret + jnp.where(n < max_exact, n, large)


def _mlstm_chunk(carry, xs):
    c0, n0, m0 = carry
    q, k, v, ig, lf = xs
    L = q.shape[2]
    b = jnp.cumsum(lf, axis=-1)
    causal = jnp.tril(jnp.ones((L, L), dtype=bool))
    d = jnp.where(causal, b[..., :, None] - b[..., None, :] + ig[..., None, :], -jnp.inf)
    g = b + m0[..., None]
    m = jnp.maximum(g, jnp.max(d, axis=-1))
    w_inter = jnp.exp(g - m)
    s = jnp.exp(d - m[..., None]) * jnp.einsum('bhtk,bhsk->bhts', q, k)
    num = w_inter[..., None] * jnp.einsum('bhtk,bhvk->bhtv', q, c0) + jnp.einsum('bhts,bhsv->bhtv', s, v)
    den = w_inter * jnp.einsum('bhtk,bhk->bht', q, n0) + jnp.sum(s, axis=-1)
    h = num / jnp.maximum(jnp.abs(den), jnp.exp(-m))[..., None]
    m_last = m[..., -1]
    w_state = jnp.exp(g[..., -1] - m_last)
    w_rows = jnp.exp(b[..., -1:] - b + ig - m_last[..., None])
    c_new = w_state[..., None, None] * c0 + jnp.einsum('bhs,bhsv,bhsk->bhvk', w_rows, v, k)
    n_new = w_state[..., None] * n0 + jnp.einsum('bhs,bhsk->bhk', w_rows, k)
    return (c_new, n_new, m_last), h


def _mlstm_scan(q, k, v, ig, lf, carry):
    B, S = q.shape[0], q.shape[1]
    L = min(CHUNK, S)
    nc = S // L
    f32 = jnp.float32
    chunks = lambda a: a.astype(f32).reshape(B, nc, L, M_HEADS, a.shape[-1]).transpose(1, 0, 3, 2, 4)
    gchunks = lambda a: a.astype(f32).reshape(B, nc, L, M_HEADS).transpose(1, 0, 3, 2)
    carry = tuple(c.astype(f32) for c in carry)
    carry, h = lax.scan(_mlstm_chunk, carry, (chunks(q), chunks(k), chunks(v), gchunks(ig), gchunks(lf)))
    h = h.transpose(1, 0, 3, 2, 4).reshape(B, S, M_HEADS, M_DV)
    return h, carry


def _mlstm_mixer(xn, w_in, b_gate, g_head, w_out, carry):
    B, S, _ = xn.shape
    p = xn @ w_in
    q = p[..., :M_QK].reshape(B, S, M_HEADS, M_DK)
    k = p[..., M_QK:2 * M_QK].reshape(B, S, M_HEADS, M_DK) * (M_DK ** -0.5)
    v = p[..., 2 * M_QK:2 * M_QK + M_V].reshape(B, S, M_HEADS, M_DV)
    o = p[..., 2 * M_QK + M_V:2 * M_QK + M_V + D_MODEL]
    gates = (p[..., 2 * M_QK + M_V + D_MODEL:] + b_gate).astype(jnp.float32)
    ig = gates[..., :M_HEADS]
    lf = jax.nn.log_sigmoid(gates[..., M_HEADS:])
    h, carry = _mlstm_scan(q, k, v, ig, lf, carry)
    h = _rmsnorm(h, g_head).astype(xn.dtype).reshape(B, S, M_V) * jax.nn.sigmoid(o)
    return h @ w_out, carry


def _diff_block(q, k_all, v_all, q_pos, k_pos, rel_bias, lam):
    s = jnp.einsum('bqhjd,bkhjd->bhjqk', q, k_all).astype(jnp.float32) * (DA_DH ** -0.5)
    bias = rel_bias.astype(jnp.float32)[_t5_bucket(k_pos[None, :] - q_pos[:, None])]
    bias = bias.transpose(2, 0, 1)[None, :, None]
    mask = (k_pos[None, :] // CHUNK) <= (q_pos[:, None] // CHUNK)
    p = jax.nn.softmax(jnp.where(mask, s + bias, -jnp.inf), axis=-1)
    a = p[:, :, 0] - lam * p[:, :, 1]
    return jnp.einsum('bhqk,bkhe->bqhe', a.astype(v_all.dtype), v_all)


def _diff_mixer(xn, k_all, v_all, q_pos, k_pos, rel_bias, w_q, lq1, lk1, lq2, lk2, g_head, w_o, layer):
    B, S, _ = xn.shape
    lam_init = 0.8 - 0.6 * math.exp(-0.3 * layer)
    f32 = jnp.float32
    lam = (jnp.exp(jnp.sum(lq1.astype(f32) * lk1.astype(f32)))
           - jnp.exp(jnp.sum(lq2.astype(f32) * lk2.astype(f32))) + lam_init)
    q = (xn @ w_q).reshape(B, S, DA_HEADS, 2, DA_DH)
    if S <= Q_BLOCK:
        o = _diff_block(q, k_all, v_all, q_pos, k_pos, rel_bias, lam)
    else:
        nb = S // Q_BLOCK
        qb = q.reshape(B, nb, Q_BLOCK, DA_HEADS, 2, DA_DH).transpose(1, 0, 2, 3, 4, 5)
        pb = q_pos.reshape(nb, Q_BLOCK)
        ob = lax.map(lambda a: _diff_block(a[0], k_all, v_all, a[1], k_pos, rel_bias, lam), (qb, pb))
        o = ob.transpose(1, 0, 2, 3, 4).reshape(B, S, DA_HEADS, 2 * DA_DH)
    o = _rmsnorm(o, g_head) * (1.0 - lam_init)
    return o.reshape(B, S, DA_Q) @ w_o


def _swiglu(xn, w_in, w_out):
    gu = xn @ w_in
    return (jax.nn.silu(gu[..., :FFN_HIDDEN]) * gu[..., FFN_HIDDEN:]) @ w_out


def _trunk(x, c0, n0, m0, past_k, past_v, norm_g, mlstm_w_in, mlstm_b_gate, mlstm_g_head, mlstm_w_out,
           kv_g, kv_w, rel_bias, diff_w_q, diff_lam_q1, diff_lam_k1, diff_lam_q2, diff_lam_k2,
           diff_g_head, diff_w_o, ffn_w_in, ffn_w_out):
    B, S, _ = x.shape
    past = past_k.shape[1]
    q_pos = past + jnp.arange(S, dtype=jnp.int32)
    k_pos = jnp.arange(past + S, dtype=jnp.int32)
    cs, ns, ms = [], [], []
    new_k = new_v = k_all = v_all = None
    for layer in range(DEPTH):
        h = _rmsnorm(x, norm_g[layer, 0])
        if layer < N_A:
            a = layer
            h, (c, n, m) = _mlstm_mixer(h, mlstm_w_in[a], mlstm_b_gate[a], mlstm_g_head[a], mlstm_w_out[a],
                                        (c0[a], n0[a], m0[a]))
            cs.append(c); ns.append(n); ms.append(m)
        else:
            bi = layer - N_A
            if bi == 0:
                kv = _rmsnorm(x, kv_g) @ kv_w
                new_k = kv[..., :DA_Q].reshape(B, S, DA_HEADS, 2, DA_DH)
                new_v = kv[..., DA_Q:].reshape(B, S, DA_HEADS, 2 * DA_DH)
                k_all = jnp.concatenate([past_k.astype(new_k.dtype), new_k], axis=1)
                v_all = jnp.concatenate([past_v.astype(new_v.dtype), new_v], axis=1)
            h = _diff_mixer(h, k_all, v_all, q_pos, k_pos, rel_bias, diff_w_q[bi], diff_lam_q1[bi],
                            diff_lam_k1[bi], diff_lam_q2[bi], diff_lam_k2[bi], diff_g_head[bi],
                            diff_w_o[bi], layer)
        x = x + _rmsnorm(h, norm_g[layer, 1])
        h = _swiglu(_rmsnorm(x, norm_g[layer, 2]), ffn_w_in[layer], ffn_w_out[layer])
        x = x + _rmsnorm(h, norm_g[layer, 3])
    return x, jnp.stack(cs), jnp.stack(ns), jnp.stack(ms), new_k, new_v


def setup_inputs(seed: int = 0) -> dict:
    key = jax.random.key(seed)
    ks = jax.random.split(key, 32)
    nrm = lambda k, shape, s=1.0: jax.random.normal(k, shape, jnp.float32) * s
    D = D_MODEL
    b_gate = jnp.concatenate([nrm(ks[10], (N_A, M_HEADS), 0.1),
                              3.0 + 3.0 * jax.random.uniform(ks[11], (N_A, M_HEADS), jnp.float32)], axis=-1)
    return {
        'x_prompt': nrm(ks[0], (BATCH, SEQ, D)),
        'x_sample': nrm(ks[1], (DEC_BATCH, DEC_SEQ, D)),
        'state_C': nrm(ks[2], (N_A, DEC_BATCH, M_HEADS, M_DV, M_DK), 0.5),
        'state_n': nrm(ks[3], (N_A, DEC_BATCH, M_HEADS, M_DK)),
        'state_m': nrm(ks[4], (N_A, DEC_BATCH, M_HEADS)),
        'cache_k': nrm(ks[5], (DEC_BATCH, PAST_LEN, DA_HEADS, 2, DA_DH)),
        'cache_v': nrm(ks[6], (DEC_BATCH, PAST_LEN, DA_HEADS, 2 * DA_DH)),
        'norm_g': 1.0 + nrm(ks[7], (DEPTH, 4, D), 0.02),
        'mlstm_w_in': nrm(ks[8], (N_A, D, M_PROJ), D ** -0.5),
        'mlstm_b_gate': b_gate,
        'mlstm_g_head': 1.0 + nrm(ks[12], (N_A, M_HEADS, M_DV), 0.02),
        'mlstm_w_out': nrm(ks[13], (N_A, M_V, D), M_V ** -0.5),
        'kv_g': 1.0 + nrm(ks[14], (D,), 0.02),
        'kv_w': nrm(ks[15], (D, 2 * DA_Q), D ** -0.5),
        'rel_bias': nrm(ks[16], (N_BUCKETS, DA_HEADS), 0.5),
        'diff_w_q': nrm(ks[17], (N_B, D, DA_Q), D ** -0.5),
        'diff_lam_q1': nrm(ks[18], (N_B, DA_DH), 0.1),
        'diff_lam_k1': nrm(ks[19], (N_B, DA_DH), 0.1),
        'diff_lam_q2': nrm(ks[20], (N_B, DA_DH), 0.1),
        'diff_lam_k2': nrm(ks[21], (N_B, DA_DH), 0.1),
        'diff_g_head': 1.0 + nrm(ks[22], (N_B, 2 * DA_DH), 0.02),
        'diff_w_o': nrm(ks[23], (N_B, DA_Q, D), DA_Q ** -0.5),
        'ffn_w_in': nrm(ks[24], (DEPTH, D, 2 * FFN_HIDDEN), D ** -0.5),
        'ffn_w_out': nrm(ks[25], (DEPTH, FFN_HIDDEN, D), FFN_HIDDEN ** -0.5),
    }


def reference(x_prompt, x_sample, state_C, state_n, state_m, cache_k, cache_v, norm_g, mlstm_w_in,
              mlstm_b_gate, mlstm_g_head, mlstm_w_out, kv_g, kv_w, rel_bias, diff_w_q, diff_lam_q1,
              diff_lam_k1, diff_lam_q2, diff_lam_k2, diff_g_head, diff_w_o, ffn_w_in, ffn_w_out):
    f32 = jnp.float32
    B = x_prompt.shape[0]
    c0 = jnp.zeros((N_A, B, M_HEADS, M_DV, M_DK), f32)
    n0 = jnp.zeros((N_A, B, M_HEADS, M_DK), f32)
    m0 = jnp.zeros((N_A, B, M_HEADS), f32)
    empty_k = jnp.zeros((B, 0, DA_HEADS, 2, DA_DH), x_prompt.dtype)
    empty_v = jnp.zeros((B, 0, DA_HEADS, 2 * DA_DH), x_prompt.dtype)
    y_prompt, p_C, p_n, p_m, p_k, p_v = _trunk(
        x_prompt, c0, n0, m0, empty_k, empty_v, norm_g, mlstm_w_in, mlstm_b_gate, mlstm_g_head,
        mlstm_w_out, kv_g, kv_w, rel_bias, diff_w_q, diff_lam_q1, diff_lam_k1, diff_lam_q2, diff_lam_k2,
        diff_g_head, diff_w_o, ffn_w_in, ffn_w_out)
    y_sample, s_C, s_n, s_m, s_k, s_v = _trunk(
        x_sample, state_C, state_n, state_m, cache_k, cache_v, norm_g, mlstm_w_in, mlstm_b_gate,
        mlstm_g_head, mlstm_w_out, kv_g, kv_w, rel_bias, diff_w_q, diff_lam_q1, diff_lam_k1, diff_lam_q2,
        diff_lam_k2, diff_g_head, diff_w_o, ffn_w_in, ffn_w_out)
    return (y_prompt, y_sample, p_C, p_n, p_m, p_k, p_v, s_C, s_n, s_m, s_k, s_v)
```

```python
import functools
import math

import jax
import jax.numpy as jnp
from jax import lax
from jax.experimental import pallas as pl
from jax.experimental.pallas import tpu as pltpu

F32 = jnp.float32
BF16 = jnp.bfloat16

D_MODEL = 1024
ATTN_CHUNK = 64
M_HEADS = 4
M_DK = D_MODEL // (2 * M_HEADS)
M_DV = D_MODEL // M_HEADS
M_QK = M_HEADS * M_DK
M_V = M_HEADS * M_DV
DA_HEADS = 8
DA_DH = D_MODEL // (2 * DA_HEADS)
DA_Q = DA_HEADS * 2 * DA_DH
FFN_HIDDEN = -(-8 * D_MODEL // (3 * 256)) * 256
N_BUCKETS = 32
MAX_DISTANCE = 128
EPS = 1e-6
LOG2E = math.log2(math.e)

LANES = 128
GATE_ROWS = 16
VMEM_LIMIT = 56 * 1024 * 1024
FFN_COL_CHUNKS = (1024, 1024, 768)
assert sum(FFN_COL_CHUNKS) == FFN_HIDDEN


def _bucket_upper_bounds():
    half = N_BUCKETS // 2
    max_exact = half // 2
    ratio = MAX_DISTANCE // max_exact
    steps = half - max_exact
    bounds = [n + 1 for n in range(max_exact)]
    for k in range(1, steps):
        n = max_exact
        while n ** steps < (max_exact ** steps) * (ratio ** k):
            n += 1
        bounds.append(n)
    return bounds


_BUCKET_BOUNDS = _bucket_upper_bounds()


def _rms(x, g):
    return x * lax.rsqrt(jnp.mean(x * x, axis=-1, keepdims=True) + EPS) * g


def _log_sigmoid(x):
    return jnp.minimum(x, 0.0) - jnp.log1p(jnp.exp(-jnp.abs(x)))


def _split3(a):
    a1 = a.astype(BF16)
    r1 = a - a1.astype(F32)
    a2 = r1.astype(BF16)
    a3 = (r1 - a2.astype(F32)).astype(BF16)
    return a1, a2, a3


def _const_spec(shape):
    return pl.BlockSpec(shape, lambda *_: (0,) * len(shape), pipeline_mode=pl.Buffered(1))


def _params(n_axes):
    return pltpu.CompilerParams(dimension_semantics=("arbitrary",) * n_axes,
                                vmem_limit_bytes=VMEM_LIMIT)


def _inproj_kernel(x_ref, g_ref, w_ref, wg_ref, bg_ref,
                   q_ref, k_ref, v_ref, o_ref, gcol_ref, grow_ref, *, chunk):
    xn = _rms(x_ref[...], g_ref[...])
    xh = xn.astype(BF16)
    xl = (xn - xh.astype(F32)).astype(BF16)
    dot = functools.partial(jnp.dot, preferred_element_type=F32)
    q_ref[...] = dot(xh, w_ref[:, :M_QK]).astype(BF16)
    k_ref[...] = (dot(xh, w_ref[:, M_QK:2 * M_QK]) * (M_DK ** -0.5)).astype(BF16)
    v_ref[...] = dot(xh, w_ref[:, 2 * M_QK:2 * M_QK + M_V]).astype(BF16)
    o_ref[...] = dot(xh, w_ref[:, 2 * M_QK + M_V:])
    gh = dot(xh, wg_ref[...])
    gl = dot(xl, wg_ref[:, :LANES])
    gates = gh[:, :LANES] + gh[:, LANES:] + gl + bg_ref[...]
    lane = lax.broadcasted_iota(jnp.int32, gates.shape, 1)
    gv = jnp.where(lane < M_HEADS, gates, _log_sigmoid(gates))
    gcol_ref[...] = gv
    gt = gv.T
    for j in range(grow_ref.shape[0]):
        grow_ref[j] = gt[:GATE_ROWS, j * chunk:(j + 1) * chunk]


def _inproj(x2d, g, w_main, w_gate, b_gate, *, tile, chunk):
    rows = x2d.shape[0]
    grid = (rows // tile,)
    row_spec = lambda cols: pl.BlockSpec((tile, cols), lambda i: (i, 0))
    return pl.pallas_call(
        functools.partial(_inproj_kernel, chunk=chunk),
        grid=grid,
        in_specs=[row_spec(D_MODEL), _const_spec((1, D_MODEL)),
                  _const_spec(w_main.shape), _const_spec(w_gate.shape), _const_spec((1, LANES))],
        out_specs=[row_spec(M_QK), row_spec(M_QK), row_spec(M_V), row_spec(D_MODEL), row_spec(LANES),
                   pl.BlockSpec((tile // chunk, GATE_ROWS, chunk), lambda i: (i, 0, 0))],
        out_shape=[jax.ShapeDtypeStruct((rows, M_QK), BF16), jax.ShapeDtypeStruct((rows, M_QK), BF16),
                   jax.ShapeDtypeStruct((rows, M_V), BF16), jax.ShapeDtypeStruct((rows, D_MODEL), F32),
                   jax.ShapeDtypeStruct((rows, LANES), F32),
                   jax.ShapeDtypeStruct((rows // chunk, GATE_ROWS, chunk), F32)],
        compiler_params=_params(1),
        name="mlstm_inproj",
    )(x2d, g, w_main, w_gate, b_gate)


def _mlstm_kernel(q_ref, k_ref, v_ref, o_ref, gcol_ref, grow_ref, x_ref, c0_ref, n0_ref, m0_ref,
                  wout_ref, ghead_ref, gpost_ref,
                  x1_ref, cout_ref, nout_ref, mout_ref,
                  ct_s, n_s, m_s, hcat_s, *, chunk, n_chunks):
    L = chunk
    c = pl.program_id(1)
    dot = functools.partial(jnp.dot, preferred_element_type=F32)

    @pl.when(c == 0)
    def _():
        for h in range(M_HEADS):
            ct_s[h] = c0_ref[0, h].T
        n_s[...] = n0_ref[0]
        m_s[...] = m0_ref[0]

    row = lax.broadcasted_iota(jnp.int32, (L, L), 0)
    col = lax.broadcasted_iota(jnp.int32, (L, L), 1)
    causal = col <= row
    tri = jnp.where(causal, 1.0, 0.0).astype(BF16)
    tri_t = jnp.where(row <= col, 1.0, 0.0).astype(BF16)
    gcol = gcol_ref[...]
    grow = grow_ref[0]
    bcol = sum(dot(tri, part) for part in _split3(gcol))
    brow = sum(dot(part, tri_t) for part in _split3(grow))

    for h in range(M_HEADS):
        q = q_ref[:, h * M_DK:(h + 1) * M_DK]
        k = k_ref[:, h * M_DK:(h + 1) * M_DK]
        v = v_ref[:, h * M_DV:(h + 1) * M_DV]
        ig_row = grow[h:h + 1, :]
        b_row = brow[M_HEADS + h:M_HEADS + h + 1, :]
        ig_col = gcol[:, h:h + 1]
        b_col = bcol[:, M_HEADS + h:M_HEADS + h + 1]
        m0 = m_s[h:h + 1, 0:1]
        n0 = n_s[h:h + 1, :]
        ct = ct_s[h]

        d = jnp.where(causal, b_col - b_row + ig_row, -jnp.inf)
        g = b_col + m0
        m = jnp.maximum(g, jnp.max(d, axis=-1, keepdims=True))
        w_inter = jnp.exp(g - m)
        qk = lax.dot_general(q, k, (((1,), (1,)), ((), ())), preferred_element_type=F32)
        s = jnp.exp(d - m) * qk
        num = w_inter * dot(q, ct.astype(BF16)) + dot(s.astype(BF16), v)
        qn = jnp.sum(q.astype(F32) * n0, axis=-1, keepdims=True)
        den = w_inter * qn + jnp.sum(s, axis=-1, keepdims=True)
        hh = num / jnp.maximum(jnp.abs(den), jnp.exp(-m))

        m_last = m[L - 1:L, :]
        w_state = jnp.exp(g[L - 1:L, :] - m_last)
        w_rows = jnp.exp(b_col[L - 1:L, :] - b_col + ig_col - m_last)
        vw = (v.astype(F32) * w_rows).astype(BF16)
        upd = lax.dot_general(k, vw, (((0,), (0,)), ((), ())), preferred_element_type=F32)
        ct_s[h] = w_state * ct + upd
        n_s[h:h + 1, :] = w_state * n0 + jnp.sum(k.astype(F32) * w_rows, axis=0, keepdims=True)
        m_s[h:h + 1, :] = jnp.broadcast_to(m_last, (1, LANES))

        hn = _rms(hh, ghead_ref[:, h * M_DV:(h + 1) * M_DV])
        gate = jax.nn.sigmoid(o_ref[:, h * M_DV:(h + 1) * M_DV])
        hcat_s[:, h * M_DV:(h + 1) * M_DV] = (hn * gate).astype(BF16)

    y = dot(hcat_s[...], wout_ref[...])
    x1_ref[...] = x_ref[...] + _rms(y, gpost_ref[...])

    @pl.when(c == n_chunks - 1)
    def _():
        for h in range(M_HEADS):
            cout_ref[0, h] = ct_s[h].T
        nout_ref[0] = n_s[...]
        mout_ref[0] = m_s[...]


def _mlstm(q, k, v, o, gcol, grow, x2d, c0, n0, m0b, w_out, g_head, g_post, *, batch, chunk):
    rows = x2d.shape[0]
    n_chunks = rows // batch // chunk
    grid = (batch, n_chunks)
    row_spec = lambda cols: pl.BlockSpec((chunk, cols), lambda b, c: (b * n_chunks + c, 0))
    state4 = pl.BlockSpec((1, M_HEADS, M_DV, M_DK), lambda b, c: (b, 0, 0, 0))
    state3 = pl.BlockSpec((1, M_HEADS, LANES), lambda b, c: (b, 0, 0))
    return pl.pallas_call(
        functools.partial(_mlstm_kernel, chunk=chunk, n_chunks=n_chunks),
        grid=grid,
        in_specs=[row_spec(M_QK), row_spec(M_QK), row_spec(M_V), row_spec(D_MODEL), row_spec(LANES),
                  pl.BlockSpec((1, GATE_ROWS, chunk), lambda b, c: (b * n_chunks + c, 0, 0)),
                  row_spec(D_MODEL), state4, state3, state3,
                  _const_spec((M_V, D_MODEL)), _const_spec((1, M_V)), _const_spec((1, D_MODEL))],
        out_specs=[row_spec(D_MODEL), state4, state3, state3],
        out_shape=[jax.ShapeDtypeStruct((rows, D_MODEL), F32),
                   jax.ShapeDtypeStruct((batch, M_HEADS, M_DV, M_DK), F32),
                   jax.ShapeDtypeStruct((batch, M_HEADS, LANES), F32),
                   jax.ShapeDtypeStruct((batch, M_HEADS, LANES), F32)],
        scratch_shapes=[pltpu.VMEM((M_HEADS, M_DK, M_DV), F32), pltpu.VMEM((M_HEADS, LANES), F32),
                        pltpu.VMEM((M_HEADS, LANES), F32), pltpu.VMEM((chunk, M_V), BF16)],
        compiler_params=_params(2),
        name="mlstm_scan",
    )(q, k, v, o, gcol, grow, x2d, c0, n0, m0b, w_out, g_head, g_post)


def _ffn_body(x, gin_ref, win_ref, wout_ref, gout_ref):
    dot = functools.partial(jnp.dot, preferred_element_type=F32)
    xn = _rms(x, gin_ref[...]).astype(BF16)
    acc = None
    start = 0
    for width in FFN_COL_CHUNKS:
        gate = dot(xn, win_ref[:, start:start + width])
        up = dot(xn, win_ref[:, FFN_HIDDEN + start:FFN_HIDDEN + start + width])
        act = (jax.nn.silu(gate) * up).astype(BF16)
        part = dot(act, wout_ref[start:start + width, :])
        acc = part if acc is None else acc + part
        start += width
    return x + _rms(acc, gout_ref[...])


def _ffn_kernel(x_ref, gin_ref, win_ref, wout_ref, gout_ref, y_ref):
    y_ref[...] = _ffn_body(x_ref[...], gin_ref, win_ref, wout_ref, gout_ref)


def _attn_out_ffn_kernel(x_ref, a_ref, wo_ref, gpre_ref, gin_ref, win_ref, wout_ref, gout_ref, y_ref):
    mix = jnp.dot(a_ref[...], wo_ref[...], preferred_element_type=F32)
    x = x_ref[...] + _rms(mix, gpre_ref[...])
    y_ref[...] = _ffn_body(x, gin_ref, win_ref, wout_ref, gout_ref)


def _ffn(x2d, g_in, w_in, w_out, g_out, *, tile, attn=None, w_o=None, g_pre=None):
    rows = x2d.shape[0]
    row_spec = lambda cols: pl.BlockSpec((tile, cols), lambda i: (i, 0))
    vec = _const_spec((1, D_MODEL))
    ffn_specs = [vec, _const_spec(w_in.shape), _const_spec(w_out.shape), vec]
    if attn is None:
        kern, in_specs = _ffn_kernel, [row_spec(D_MODEL)] + ffn_specs
        args = (x2d, g_in, w_in, w_out, g_out)
    else:
        kern = _attn_out_ffn_kernel
        in_specs = [row_spec(D_MODEL), row_spec(DA_Q), _const_spec(w_o.shape), vec] + ffn_specs
        args = (x2d, attn, w_o, g_pre, g_in, w_in, w_out, g_out)
    return pl.pallas_call(
        kern, grid=(rows // tile,), in_specs=in_specs, out_specs=row_spec(D_MODEL),
        out_shape=jax.ShapeDtypeStruct((rows, D_MODEL), F32),
        compiler_params=_params(1), name="ffn",
    )(*args)


def _qkv_kernel(x_ref, gq_ref, gkv_ref, wq_ref, wkv_ref, q_ref, kf_ref, vf_ref, kb_ref, vb_ref):
    x = x_ref[...]
    xhat = x * lax.rsqrt(jnp.mean(x * x, axis=-1, keepdims=True) + EPS)
    xq = (xhat * gq_ref[...]).astype(BF16)
    xkv = (xhat * gkv_ref[...]).astype(BF16)
    dot = functools.partial(jnp.dot, preferred_element_type=F32)
    q_ref[...] = (dot(xq, wq_ref[...]) * (DA_DH ** -0.5 * LOG2E)).astype(BF16)
    kk = dot(xkv, wkv_ref[:, :DA_Q])
    kf_ref[...] = kk
    kb_ref[...] = kk.astype(BF16)
    vv = dot(xkv, wkv_ref[:, DA_Q:])
    vf_ref[...] = vv
    vb_ref[...] = vv.astype(BF16)


def _qkv(x2d, g_q, g_kv, w_q, w_kv, *, tile):
    rows = x2d.shape[0]
    row_spec = pl.BlockSpec((tile, D_MODEL), lambda i: (i, 0))
    vec = _const_spec((1, D_MODEL))
    shp = lambda dt: jax.ShapeDtypeStruct((rows, DA_Q), dt)
    return pl.pallas_call(
        _qkv_kernel, grid=(rows // tile,),
        in_specs=[row_spec, vec, vec, _const_spec(w_q.shape), _const_spec(w_kv.shape)],
        out_specs=[row_spec] * 5,
        out_shape=[shp(BF16), shp(F32), shp(F32), shp(BF16), shp(BF16)],
        compiler_params=_params(1), name="attn_qkv",
    )(x2d, g_q, g_kv, w_q, w_kv)


def _rel_bias_tile(rb_ref, head, q_pos, k_pos):
    rel = k_pos - q_pos
    n = jnp.abs(rel)
    half = N_BUCKETS // 2

    def table(offset):
        val = jnp.full(rel.shape, rb_ref[offset + half - 1, head], F32)
        for bucket in reversed(range(half - 1)):
            val = jnp.where(n < _BUCKET_BOUNDS[bucket], rb_ref[offset + bucket, head], val)
        return val

    bias = jnp.where(rel > 0, table(half), table(0)) * LOG2E
    shift = ATTN_CHUNK.bit_length() - 1
    visible = jnp.right_shift(k_pos, shift) <= jnp.right_shift(q_pos, shift)
    return jnp.where(visible, bias, -jnp.inf)


def _lambda(l1q_ref, l1k_ref, l2q_ref, l2k_ref, lam_init):
    a = jnp.exp(jnp.sum(l1q_ref[...] * l1k_ref[...], axis=-1, keepdims=True))
    b = jnp.exp(jnp.sum(l2q_ref[...] * l2k_ref[...], axis=-1, keepdims=True))
    return a - b + lam_init


def _stack_maps(q):
    lane = lax.broadcasted_iota(jnp.int32, q.shape, 1)
    zero = jnp.zeros_like(q)
    return jnp.concatenate([jnp.where(lane < DA_DH, q, zero), jnp.where(lane >= DA_DH, q, zero)], axis=0)


def _diff_finish(acc, l, lam, g, lam_init):
    t = acc.shape[0] // 2
    o = acc[:t] / l[:t] - lam * (acc[t:] / l[t:])
    return _rms(o, g) * (1.0 - lam_init)


def _prompt_attn_kernel(rb_ref, q_ref, k_ref, v_ref, l1q_ref, l1k_ref, l2q_ref, l2k_ref, g_ref,
                        o_ref, bias_s, m_s, l_s, acc_s, *, tile, lam_init):
    T = tile
    h = pl.program_id(0)
    b = pl.program_id(1)
    i = pl.program_id(2)
    nt = lambda a, bb: lax.dot_general(a, bb, (((1,), (1,)), ((), ())), preferred_element_type=F32)

    @pl.when((b == 0) & (i == 0))
    def _():
        qp = lax.broadcasted_iota(jnp.int32, (T, T), 0) + T
        kp = lax.broadcasted_iota(jnp.int32, (T, T), 1)
        bias_s[0] = _rel_bias_tile(rb_ref, h, qp, kp + T)
        bias_s[1] = _rel_bias_tile(rb_ref, h, qp, kp)

    qs = _stack_maps(q_ref[...])

    def block(j):
        kb = k_ref[pl.ds(pl.multiple_of(j * T, T), T), :]
        vb = v_ref[pl.ds(pl.multiple_of(j * T, T), T), :]
        return nt(qs, kb), vb

    s, vb = block(i)
    s = s + jnp.concatenate([bias_s[0], bias_s[0]], axis=0)
    m = jnp.max(s, axis=-1, keepdims=True)
    p = jnp.exp2(s - m)
    m_s[...] = m
    l_s[...] = jnp.sum(p, axis=-1, keepdims=True)
    acc_s[...] = jnp.dot(p.astype(BF16), vb, preferred_element_type=F32)

    def update(s, vb, shift):
        m_prev = m_s[...]
        m_new = jnp.maximum(m_prev, jnp.max(s, axis=-1, keepdims=True) + shift)
        alpha = jnp.exp2(m_prev - m_new)
        p = jnp.exp2(s - (m_new - shift))
        m_s[...] = m_new
        l_s[...] = alpha * l_s[...] + jnp.sum(p, axis=-1, keepdims=True)
        acc_s[...] = alpha * acc_s[...] + jnp.dot(p.astype(BF16), vb, preferred_element_type=F32)

    @pl.when(i >= 1)
    def _():
        s, vb = block(i - 1)
        update(s + jnp.concatenate([bias_s[1], bias_s[1]], axis=0), vb, 0.0)

    far = rb_ref[N_BUCKETS // 2 - 1, h] * LOG2E

    def far_step(j, carry):
        s, vb = block(j)
        update(s, vb, far)
        return carry

    lax.fori_loop(0, jnp.maximum(i - 1, 0), far_step, 0)

    lam = _lambda(l1q_ref, l1k_ref, l2q_ref, l2k_ref, lam_init)
    o_ref[...] = _diff_finish(acc_s[...], l_s[...], lam, g_ref[...], lam_init).astype(BF16)


def _prompt_attn(rel_bias, q, k, v, lams, g_head, *, batch, seq, tile, lam_init):
    assert tile >= MAX_DISTANCE and tile % ATTN_CHUNK == 0 and seq % tile == 0
    nq = seq // tile
    hd = 2 * DA_DH
    lam_spec = _const_spec((1, DA_DH))
    return pl.pallas_call(
        functools.partial(_prompt_attn_kernel, tile=tile, lam_init=lam_init),
        grid=(DA_HEADS, batch, nq),
        in_specs=[pl.BlockSpec(memory_space=pltpu.SMEM),
                  pl.BlockSpec((tile, hd), lambda h, b, i: (b * nq + i, h)),
                  pl.BlockSpec((seq, hd), lambda h, b, i: (b, h)),
                  pl.BlockSpec((seq, hd), lambda h, b, i: (b, h)),
                  lam_spec, lam_spec, lam_spec, lam_spec, _const_spec((1, hd))],
        out_specs=pl.BlockSpec((tile, hd), lambda h, b, i: (b * nq + i, h)),
        out_shape=jax.ShapeDtypeStruct((batch * seq, DA_Q), BF16),
        scratch_shapes=[pltpu.VMEM((2, tile, tile), F32), pltpu.VMEM((2 * tile, 1), F32),
                        pltpu.VMEM((2 * tile, 1), F32), pltpu.VMEM((2 * tile, hd), F32)],
        compiler_params=_params(3), name="diff_attn_prompt",
    )(rel_bias, q, k, v, *lams, g_head)


def _sample_attn_kernel(rb_ref, q_ref, kp_ref, vp_ref, kn_ref, vn_ref, l1q_ref, l1k_ref, l2q_ref, l2k_ref,
                        g_ref, o_ref, bias_p, bias_n, *, past, seq, lam_init):
    h = pl.program_id(0)
    b = pl.program_id(1)
    nt = lambda a, bb: lax.dot_general(a, bb, (((1,), (1,)), ((), ())), preferred_element_type=F32)

    @pl.when(b == 0)
    def _():
        qp = lax.broadcasted_iota(jnp.int32, (seq, past), 0) + past
        kp = lax.broadcasted_iota(jnp.int32, (seq, past), 1)
        tile = _rel_bias_tile(rb_ref, h, qp, kp)
        bias_p[...] = jnp.concatenate([tile, tile], axis=0)
        qn = lax.broadcasted_iota(jnp.int32, (seq, seq), 0) + past
        kn = lax.broadcasted_iota(jnp.int32, (seq, seq), 1) + past
        tile = _rel_bias_tile(rb_ref, h, qn, kn)
        bias_n[...] = jnp.concatenate([tile, tile], axis=0)

    qs = _stack_maps(q_ref[...])
    s_p = nt(qs, kp_ref[0].astype(BF16)) + bias_p[...]
    s_n = nt(qs, kn_ref[...]) + bias_n[...]
    m = jnp.maximum(jnp.max(s_p, axis=-1, keepdims=True), jnp.max(s_n, axis=-1, keepdims=True))
    p_p = jnp.exp2(s_p - m)
    p_n = jnp.exp2(s_n - m)
    l = jnp.sum(p_p, axis=-1, keepdims=True) + jnp.sum(p_n, axis=-1, keepdims=True)
    acc = (jnp.dot(p_p.astype(BF16), vp_ref[0].astype(BF16), preferred_element_type=F32)
           + jnp.dot(p_n.astype(BF16), vn_ref[...], preferred_element_type=F32))
    lam = _lambda(l1q_ref, l1k_ref, l2q_ref, l2k_ref, lam_init)
    o_ref[...] = _diff_finish(acc, l, lam, g_ref[...], lam_init).astype(BF16)


def _sample_attn(rel_bias, q, cache_k, cache_v, k_new, v_new, lams, g_head, *, batch, seq, lam_init):
    past = cache_k.shape[1]
    hd = 2 * DA_DH
    lam_spec = _const_spec((1, DA_DH))
    new_spec = pl.BlockSpec((seq, hd), lambda h, b: (b, h))
    past_spec = pl.BlockSpec((1, past, hd), lambda h, b: (b, 0, h))
    return pl.pallas_call(
        functools.partial(_sample_attn_kernel, past=past, seq=seq, lam_init=lam_init),
        grid=(DA_HEADS, batch),
        in_specs=[pl.BlockSpec(memory_space=pltpu.SMEM), new_spec, past_spec, past_spec, new_spec, new_spec,
                  lam_spec, lam_spec, lam_spec, lam_spec, _const_spec((1, hd))],
        out_specs=new_spec,
        out_shape=jax.ShapeDtypeStruct((batch * seq, DA_Q), BF16),
        scratch_shapes=[pltpu.VMEM((2 * seq, past), F32), pltpu.VMEM((2 * seq, seq), F32)],
        compiler_params=_params(2), name="diff_attn_sample",
    )(rel_bias, q, cache_k, cache_v, k_new, v_new, *lams, g_head)


def _prep_weights(norm_g, mlstm_w_in, mlstm_b_gate, mlstm_g_head, mlstm_w_out, kv_g, kv_w, diff_w_q,
                  diff_lam_q1, diff_lam_k1, diff_lam_q2, diff_lam_k2, diff_g_head, diff_w_o,
                  ffn_w_in, ffn_w_out):
    n_main = 2 * M_QK + M_V + D_MODEL
    w_in = mlstm_w_in[0]
    wg = jnp.pad(w_in[:, n_main:], ((0, 0), (0, LANES - 2 * M_HEADS)))
    wg_hi = wg.astype(BF16)
    wg_lo = (wg - wg_hi.astype(F32)).astype(BF16)
    row = lambda a: a.reshape(1, -1)
    return dict(
        norm=[[row(norm_g[l, j]) for j in range(4)] for l in range(2)],
        w_main=w_in[:, :n_main].astype(BF16),
        w_gate=jnp.concatenate([wg_hi, wg_lo], axis=1),
        b_gate=jnp.pad(mlstm_b_gate[0], (0, LANES - 2 * M_HEADS)).reshape(1, LANES),
        g_mhead=row(mlstm_g_head[0]),
        w_mout=mlstm_w_out[0].astype(BF16),
        kv_g=row(kv_g), kv_w=kv_w.astype(BF16), w_q=diff_w_q[0].astype(BF16),
        lams=[row(diff_lam_q1[0]), row(diff_lam_k1[0]), row(diff_lam_q2[0]), row(diff_lam_k2[0])],
        g_dhead=row(diff_g_head[0]), w_o=diff_w_o[0].astype(BF16),
        ffn_in=[ffn_w_in[l].astype(BF16) for l in range(2)],
        ffn_out=[ffn_w_out[l].astype(BF16) for l in range(2)],
    )


def _trunk(x, c0, n0, m0, past_k, past_v, rel_bias, w, *, row_tile, chunk, attn_tile):
    batch, seq, _ = x.shape
    rows = batch * seq
    x2d = x.reshape(rows, D_MODEL)
    lam_init = 0.8 - 0.6 * math.exp(-0.3 * 1)

    q, k, v, o, gcol, grow = _inproj(x2d, w["norm"][0][0], w["w_main"], w["w_gate"], w["b_gate"],
                                     tile=row_tile, chunk=chunk)
    m0b = jnp.broadcast_to(m0[..., None], (batch, M_HEADS, LANES))
    x1, c_new, n_new, m_new = _mlstm(q, k, v, o, gcol, grow, x2d, c0, n0, m0b, w["w_mout"], w["g_mhead"],
                                     w["norm"][0][1], batch=batch, chunk=chunk)
    x2 = _ffn(x1, w["norm"][0][2], w["ffn_in"][0], w["ffn_out"][0], w["norm"][0][3], tile=row_tile)

    aq, k_f32, v_f32, k_bf, v_bf = _qkv(x2, w["norm"][1][0], w["kv_g"], w["w_q"], w["kv_w"], tile=row_tile)
    if past_k is None:
        attn = _prompt_attn(rel_bias, aq, k_bf, v_bf, w["lams"], w["g_dhead"],
                            batch=batch, seq=seq, tile=attn_tile, lam_init=lam_init)
    else:
        past = past_k.shape[1]
        attn = _sample_attn(rel_bias, aq, past_k.reshape(batch, past, DA_Q), past_v.reshape(batch, past, DA_Q),
                            k_bf, v_bf, w["lams"], w["g_dhead"], batch=batch, seq=seq, lam_init=lam_init)
    y = _ffn(x2, w["norm"][1][2], w["ffn_in"][1], w["ffn_out"][1], w["norm"][1][3], tile=row_tile,
             attn=attn, w_o=w["w_o"], g_pre=w["norm"][1][1])

    return (y.reshape(batch, seq, D_MODEL), c_new[None], n_new[None], m_new[None, :, :, 0],
            k_f32.reshape(batch, seq, DA_HEADS, 2, DA_DH), v_f32.reshape(batch, seq, DA_HEADS, 2 * DA_DH))


def kernel(x_prompt, x_sample, state_C, state_n, state_m, cache_k, cache_v, norm_g, mlstm_w_in, mlstm_b_gate, mlstm_g_head, mlstm_w_out, kv_g, kv_w, rel_bias, diff_w_q, diff_lam_q1, diff_lam_k1, diff_lam_q2, diff_lam_k2, diff_g_head, diff_w_o, ffn_w_in, ffn_w_out):
    w = _prep_weights(norm_g, mlstm_w_in, mlstm_b_gate, mlstm_g_head, mlstm_w_out, kv_g, kv_w, diff_w_q,
                      diff_lam_q1, diff_lam_k1, diff_lam_q2, diff_lam_k2, diff_g_head, diff_w_o,
                      ffn_w_in, ffn_w_out)
    pb = x_prompt.shape[0]
    zeros = lambda *s: jnp.zeros(s, F32)
    y_p, p_c, p_n, p_m, p_k, p_v = _trunk(
        x_prompt, zeros(pb, M_HEADS, M_DV, M_DK), zeros(pb, M_HEADS, M_DK), zeros(pb, M_HEADS),
        None, None, rel_bias, w, row_tile=512, chunk=256, attn_tile=256)
    sb, ss, _ = x_sample.shape
    y_s, s_c, s_n, s_m, s_k, s_v = _trunk(
        x_sample, state_C[0], state_n[0], state_m[0], cache_k, cache_v, rel_bias, w,
        row_tile=sb * ss, chunk=ss, attn_tile=None)
    return (y_p, y_s, p_c, p_n, p_m, p_k, p_v, s_c, s_n, s_m, s_k, s_v)
```

```python
import functools
import math

import jax
import jax.numpy as jnp
from jax import lax
from jax.experimental import pallas as pl
from jax.experimental.pallas import tpu as pltpu

F32 = jnp.float32
BF16 = jnp.bfloat16

D_MODEL = 1024
ATTN_CHUNK = 64
M_HEADS = 4
M_DK = D_MODEL // (2 * M_HEADS)
M_DV = D_MODEL // M_HEADS
M_QK = M_HEADS * M_DK
M_V = M_HEADS * M_DV
DA_HEADS = 8
DA_DH = D_MODEL // (2 * DA_HEADS)
DA_Q = DA_HEADS * 2 * DA_DH
FFN_HIDDEN = -(-8 * D_MODEL // (3 * 256)) * 256
N_BUCKETS = 32
MAX_DISTANCE = 128
EPS = 1e-6
LOG2E = math.log2(math.e)

LANES = 128
GATE_ROWS = 16
VMEM_LIMIT = 56 * 1024 * 1024
ATTN_SPLIT_ROWS = 256
FFN_COL_CHUNKS = (1024, 1024, 768)
assert sum(FFN_COL_CHUNKS) == FFN_HIDDEN


def _bucket_upper_bounds():
    half = N_BUCKETS // 2
    max_exact = half // 2
    ratio = MAX_DISTANCE // max_exact
    steps = half - max_exact
    bounds = [n + 1 for n in range(max_exact)]
    for k in range(1, steps):
        n = max_exact
        while n ** steps < (max_exact ** steps) * (ratio ** k):
            n += 1
        bounds.append(n)
    return bounds


_BUCKET_BOUNDS = _bucket_upper_bounds()


def _rms(x, g):
    return x * lax.rsqrt(jnp.mean(x * x, axis=-1, keepdims=True) + EPS) * g


def _log_sigmoid(x):
    return jnp.minimum(x, 0.0) - jnp.log1p(jnp.exp(-jnp.abs(x)))


def _split3(a):
    a1 = a.astype(BF16)
    r1 = a - a1.astype(F32)
    a2 = r1.astype(BF16)
    a3 = (r1 - a2.astype(F32)).astype(BF16)
    return a1, a2, a3


def _const_spec(shape):
    return pl.BlockSpec(shape, lambda *_: (0,) * len(shape), pipeline_mode=pl.Buffered(1))


def _params(n_axes):
    return pltpu.CompilerParams(dimension_semantics=("arbitrary",) * n_axes,
                                vmem_limit_bytes=VMEM_LIMIT)


def _inproj_kernel(x_ref, g_ref, w_ref, wg_ref, bg_ref,
                   q_ref, k_ref, v_ref, o_ref, gcol_ref, grow_ref, *, chunk):
    xn = _rms(x_ref[...], g_ref[...])
    xh = xn.astype(BF16)
    xl = (xn - xh.astype(F32)).astype(BF16)
    dot = functools.partial(jnp.dot, preferred_element_type=F32)
    q_ref[...] = dot(xh, w_ref[:, :M_QK]).astype(BF16)
    k_ref[...] = (dot(xh, w_ref[:, M_QK:2 * M_QK]) * (M_DK ** -0.5)).astype(BF16)
    v_ref[...] = dot(xh, w_ref[:, 2 * M_QK:2 * M_QK + M_V]).astype(BF16)
    o_ref[...] = dot(xh, w_ref[:, 2 * M_QK + M_V:])
    gh = dot(xh, wg_ref[...])
    gl = dot(xl, wg_ref[:, :LANES])
    gates = gh[:, :LANES] + gh[:, LANES:] + gl + bg_ref[...]
    lane = lax.broadcasted_iota(jnp.int32, gates.shape, 1)
    gv = jnp.where(lane < M_HEADS, gates, _log_sigmoid(gates))
    gcol_ref[...] = gv
    gt = gv.T
    for j in range(grow_ref.shape[0]):
        grow_ref[j] = gt[:GATE_ROWS, j * chunk:(j + 1) * chunk]


def _inproj(x2d, g, w_main, w_gate, b_gate, *, tile, chunk):
    rows = x2d.shape[0]
    grid = (rows // tile,)
    row_spec = lambda cols: pl.BlockSpec((tile, cols), lambda i: (i, 0))
    return pl.pallas_call(
        functools.partial(_inproj_kernel, chunk=chunk),
        grid=grid,
        in_specs=[row_spec(D_MODEL), _const_spec((1, D_MODEL)),
                  _const_spec(w_main.shape), _const_spec(w_gate.shape), _const_spec((1, LANES))],
        out_specs=[row_spec(M_QK), row_spec(M_QK), row_spec(M_V), row_spec(D_MODEL), row_spec(LANES),
                   pl.BlockSpec((tile // chunk, GATE_ROWS, chunk), lambda i: (i, 0, 0))],
        out_shape=[jax.ShapeDtypeStruct((rows, M_QK), BF16), jax.ShapeDtypeStruct((rows, M_QK), BF16),
                   jax.ShapeDtypeStruct((rows, M_V), BF16), jax.ShapeDtypeStruct((rows, D_MODEL), F32),
                   jax.ShapeDtypeStruct((rows, LANES), F32),
                   jax.ShapeDtypeStruct((rows // chunk, GATE_ROWS, chunk), F32)],
        compiler_params=_params(1),
        name="mlstm_inproj",
    )(x2d, g, w_main, w_gate, b_gate)


def _mlstm_kernel(q_ref, k_ref, v_ref, o_ref, gcol_ref, grow_ref, x_ref, c0_ref, n0_ref, m0_ref,
                  wout_ref, ghead_ref, gpost_ref,
                  x1_ref, cout_ref, nout_ref, mout_ref,
                  ct_s, n_s, m_s, hcat_s, *, chunk, n_chunks):
    L = chunk
    c = pl.program_id(1)
    dot = functools.partial(jnp.dot, preferred_element_type=F32)

    @pl.when(c == 0)
    def _():
        for h in range(M_HEADS):
            ct_s[h] = c0_ref[0, h].T
        n_s[...] = n0_ref[0]
        m_s[...] = m0_ref[0]

    row = lax.broadcasted_iota(jnp.int32, (L, L), 0)
    col = lax.broadcasted_iota(jnp.int32, (L, L), 1)
    causal = col <= row
    tri = jnp.where(causal, 1.0, 0.0).astype(BF16)
    tri_t = jnp.where(row <= col, 1.0, 0.0).astype(BF16)
    gcol = gcol_ref[...]
    grow = grow_ref[0]
    bcol = sum(dot(tri, part) for part in _split3(gcol))
    brow = sum(dot(part, tri_t) for part in _split3(grow))

    for h in range(M_HEADS):
        q = q_ref[:, h * M_DK:(h + 1) * M_DK]
        k = k_ref[:, h * M_DK:(h + 1) * M_DK]
        v = v_ref[:, h * M_DV:(h + 1) * M_DV]
        ig_row = grow[h:h + 1, :]
        b_row = brow[M_HEADS + h:M_HEADS + h + 1, :]
        ig_col = gcol[:, h:h + 1]
        b_col = bcol[:, M_HEADS + h:M_HEADS + h + 1]
        m0 = m_s[h:h + 1, 0:1]
        n0 = n_s[h:h + 1, :]
        ct = ct_s[h]

        d = jnp.where(causal, b_col - b_row + ig_row, -jnp.inf)
        g = b_col + m0
        m = jnp.maximum(g, jnp.max(d, axis=-1, keepdims=True))
        w_inter = jnp.exp(g - m)
        qk = lax.dot_general(q, k, (((1,), (1,)), ((), ())), preferred_element_type=F32)
        s = jnp.exp(d - m) * qk
        num = w_inter * dot(q, ct.astype(BF16)) + dot(s.astype(BF16), v)
        qn = jnp.sum(q.astype(F32) * n0, axis=-1, keepdims=True)
        den = w_inter * qn + jnp.sum(s, axis=-1, keepdims=True)
        hh = num / jnp.maximum(jnp.abs(den), jnp.exp(-m))

        m_last = m[L - 1:L, :]
        w_state = jnp.exp(g[L - 1:L, :] - m_last)
        w_rows = jnp.exp(b_col[L - 1:L, :] - b_col + ig_col - m_last)
        vw = (v.astype(F32) * w_rows).astype(BF16)
        upd = lax.dot_general(k, vw, (((0,), (0,)), ((), ())), preferred_element_type=F32)
        ct_s[h] = w_state * ct + upd
        n_s[h:h + 1, :] = w_state * n0 + jnp.sum(k.astype(F32) * w_rows, axis=0, keepdims=True)
        m_s[h:h + 1, :] = jnp.broadcast_to(m_last, (1, LANES))

        hn = _rms(hh, ghead_ref[:, h * M_DV:(h + 1) * M_DV])
        gate = jax.nn.sigmoid(o_ref[:, h * M_DV:(h + 1) * M_DV])
        hcat_s[:, h * M_DV:(h + 1) * M_DV] = (hn * gate).astype(BF16)

    y = dot(hcat_s[...], wout_ref[...])
    x1_ref[...] = x_ref[...] + _rms(y, gpost_ref[...])

    @pl.when(c == n_chunks - 1)
    def _():
        for h in range(M_HEADS):
            cout_ref[0, h] = ct_s[h].T
        nout_ref[0] = n_s[...]
        mout_ref[0] = m_s[...]


def _mlstm(q, k, v, o, gcol, grow, x2d, c0, n0, m0b, w_out, g_head, g_post, *, batch, chunk):
    rows = x2d.shape[0]
    n_chunks = rows // batch // chunk
    grid = (batch, n_chunks)
    row_spec = lambda cols: pl.BlockSpec((chunk, cols), lambda b, c: (b * n_chunks + c, 0))
    state4 = pl.BlockSpec((1, M_HEADS, M_DV, M_DK), lambda b, c: (b, 0, 0, 0))
    state3 = pl.BlockSpec((1, M_HEADS, LANES), lambda b, c: (b, 0, 0))
    return pl.pallas_call(
        functools.partial(_mlstm_kernel, chunk=chunk, n_chunks=n_chunks),
        grid=grid,
        in_specs=[row_spec(M_QK), row_spec(M_QK), row_spec(M_V), row_spec(D_MODEL), row_spec(LANES),
                  pl.BlockSpec((1, GATE_ROWS, chunk), lambda b, c: (b * n_chunks + c, 0, 0)),
                  row_spec(D_MODEL), state4, state3, state3,
                  _const_spec((M_V, D_MODEL)), _const_spec((1, M_V)), _const_spec((1, D_MODEL))],
        out_specs=[row_spec(D_MODEL), state4, state3, state3],
        out_shape=[jax.ShapeDtypeStruct((rows, D_MODEL), F32),
                   jax.ShapeDtypeStruct((batch, M_HEADS, M_DV, M_DK), F32),
                   jax.ShapeDtypeStruct((batch, M_HEADS, LANES), F32),
                   jax.ShapeDtypeStruct((batch, M_HEADS, LANES), F32)],
        scratch_shapes=[pltpu.VMEM((M_HEADS, M_DK, M_DV), F32), pltpu.VMEM((M_HEADS, LANES), F32),
                        pltpu.VMEM((M_HEADS, LANES), F32), pltpu.VMEM((chunk, M_V), BF16)],
        compiler_params=_params(2),
        name="mlstm_scan",
    )(q, k, v, o, gcol, grow, x2d, c0, n0, m0b, w_out, g_head, g_post)


def _ffn_body(x, gin_ref, win_ref, wout_ref, gout_ref):
    dot = functools.partial(jnp.dot, preferred_element_type=F32)
    xn = _rms(x, gin_ref[...]).astype(BF16)
    acc = None
    start = 0
    for width in FFN_COL_CHUNKS:
        gate = dot(xn, win_ref[:, start:start + width])
        up = dot(xn, win_ref[:, FFN_HIDDEN + start:FFN_HIDDEN + start + width])
        act = (jax.nn.silu(gate) * up).astype(BF16)
        part = dot(act, wout_ref[start:start + width, :])
        acc = part if acc is None else acc + part
        start += width
    return x + _rms(acc, gout_ref[...])


def _ffn_kernel(x_ref, gin_ref, win_ref, wout_ref, gout_ref, y_ref):
    y_ref[...] = _ffn_body(x_ref[...], gin_ref, win_ref, wout_ref, gout_ref)


def _attn_out_ffn_kernel(x_ref, a_ref, wo_ref, gpre_ref, gin_ref, win_ref, wout_ref, gout_ref, y_ref):
    mix = jnp.dot(a_ref[...], wo_ref[...], preferred_element_type=F32)
    x = x_ref[...] + _rms(mix, gpre_ref[...])
    y_ref[...] = _ffn_body(x, gin_ref, win_ref, wout_ref, gout_ref)


def _ffn(x2d, g_in, w_in, w_out, g_out, *, tile, attn=None, w_o=None, g_pre=None):
    rows = x2d.shape[0]
    row_spec = lambda cols: pl.BlockSpec((tile, cols), lambda i: (i, 0))
    vec = _const_spec((1, D_MODEL))
    ffn_specs = [vec, _const_spec(w_in.shape), _const_spec(w_out.shape), vec]
    if attn is None:
        kern, in_specs = _ffn_kernel, [row_spec(D_MODEL)] + ffn_specs
        args = (x2d, g_in, w_in, w_out, g_out)
    else:
        kern = _attn_out_ffn_kernel
        in_specs = [row_spec(D_MODEL), row_spec(DA_Q), _const_spec(w_o.shape), vec] + ffn_specs
        args = (x2d, attn, w_o, g_pre, g_in, w_in, w_out, g_out)
    return pl.pallas_call(
        kern, grid=(rows // tile,), in_specs=in_specs, out_specs=row_spec(D_MODEL),
        out_shape=jax.ShapeDtypeStruct((rows, D_MODEL), F32),
        compiler_params=_params(1), name="ffn",
    )(*args)


def _qkv_kernel(x_ref, gq_ref, gkv_ref, wq_ref, wkv_ref, q_ref, kf_ref, vf_ref, kb_ref, vb_ref):
    x = x_ref[...]
    xhat = x * lax.rsqrt(jnp.mean(x * x, axis=-1, keepdims=True) + EPS)
    xq = (xhat * gq_ref[...]).astype(BF16)
    xkv = (xhat * gkv_ref[...]).astype(BF16)
    dot = functools.partial(jnp.dot, preferred_element_type=F32)
    q_ref[...] = (dot(xq, wq_ref[...]) * (DA_DH ** -0.5 * LOG2E)).astype(BF16)
    kk = dot(xkv, wkv_ref[:, :DA_Q])
    kf_ref[...] = kk
    kb_ref[...] = kk.astype(BF16)
    vv = dot(xkv, wkv_ref[:, DA_Q:])
    vf_ref[...] = vv
    vb_ref[...] = vv.astype(BF16)


def _qkv(x2d, g_q, g_kv, w_q, w_kv, *, tile):
    rows = x2d.shape[0]
    row_spec = pl.BlockSpec((tile, D_MODEL), lambda i: (i, 0))
    vec = _const_spec((1, D_MODEL))
    shp = lambda dt: jax.ShapeDtypeStruct((rows, DA_Q), dt)
    return pl.pallas_call(
        _qkv_kernel, grid=(rows // tile,),
        in_specs=[row_spec, vec, vec, _const_spec(w_q.shape), _const_spec(w_kv.shape)],
        out_specs=[row_spec] * 5,
        out_shape=[shp(BF16), shp(F32), shp(F32), shp(BF16), shp(BF16)],
        compiler_params=_params(1), name="attn_qkv",
    )(x2d, g_q, g_kv, w_q, w_kv)


def _rel_bias_tile(rb_ref, head, q_pos, k_pos):
    rel = k_pos - q_pos
    n = jnp.abs(rel)
    half = N_BUCKETS // 2

    def table(offset):
        val = jnp.full(rel.shape, rb_ref[offset + half - 1, head], F32)
        for bucket in reversed(range(half - 1)):
            val = jnp.where(n < _BUCKET_BOUNDS[bucket], rb_ref[offset + bucket, head], val)
        return val

    bias = jnp.where(rel > 0, table(half), table(0)) * LOG2E
    shift = ATTN_CHUNK.bit_length() - 1
    visible = jnp.right_shift(k_pos, shift) <= jnp.right_shift(q_pos, shift)
    return jnp.where(visible, bias, -jnp.inf)


def _lambda(l1q_ref, l1k_ref, l2q_ref, l2k_ref, lam_init):
    a = jnp.exp(jnp.sum(l1q_ref[...] * l1k_ref[...], axis=-1, keepdims=True))
    b = jnp.exp(jnp.sum(l2q_ref[...] * l2k_ref[...], axis=-1, keepdims=True))
    return a - b + lam_init


def _stack_maps(q):
    lane = lax.broadcasted_iota(jnp.int32, q.shape, 1)
    zero = jnp.zeros_like(q)
    return jnp.concatenate([jnp.where(lane < DA_DH, q, zero), jnp.where(lane >= DA_DH, q, zero)], axis=0)


def _diff_finish(acc, l, lam, g, lam_init):
    t = acc.shape[0] // 2
    o = acc[:t] / l[:t] - lam * (acc[t:] / l[t:])
    return _rms(o, g) * (1.0 - lam_init)


def _prompt_attn_kernel(rb_ref, q_ref, k_ref, v_ref, l1q_ref, l1k_ref, l2q_ref, l2k_ref, g_ref,
                        o_ref, bias_s, m_s, acc_s, *, tile, split_rows, lam_init):
    T = tile
    h = pl.program_id(0)
    b = pl.program_id(1)
    i = pl.program_id(2)
    nt = lambda a, bb: lax.dot_general(a, bb, (((1,), (1,)), ((), ())), preferred_element_type=F32)

    @pl.when((b == 0) & (i == 0))
    def _():
        qp = lax.broadcasted_iota(jnp.int32, (T, T), 0) + T
        kp = lax.broadcasted_iota(jnp.int32, (T, T), 1)
        diag = _rel_bias_tile(rb_ref, h, qp, kp + T)
        below = _rel_bias_tile(rb_ref, h, qp, kp)
        bias_s[0] = jnp.concatenate([diag, diag], axis=0)
        bias_s[1] = jnp.concatenate([below, below], axis=0)

    qs = _stack_maps(q_ref[...])
    hd = 2 * DA_DH

    n_split = 2 * T // split_rows

    def lanes2(a):
        return jnp.concatenate([a] * (T // LANES), axis=1)

    def update(j, bias_idx, shift, first=False):
        kb = k_ref[pl.ds(pl.multiple_of(j * T, T), T), :]
        vb = v_ref[pl.ds(pl.multiple_of(j * T, T), T), :]
        vext = jnp.concatenate([vb, jnp.ones_like(vb)], axis=1)
        for r in range(n_split):
            rows = slice(r * split_rows, (r + 1) * split_rows)
            s = nt(qs[rows], kb)
            if bias_idx is not None:
                s = s + bias_s[bias_idx, rows, :]
            m_cur = jnp.max(s, axis=-1, keepdims=True) + shift
            if first:
                m_new = jnp.broadcast_to(m_cur, (split_rows, LANES))
            else:
                m_prev = m_s[rows, :]
                m_new = jnp.maximum(m_prev, m_cur)
            p = jnp.exp2(s - lanes2(m_new - shift))
            pv = jnp.dot(p.astype(BF16), vext, preferred_element_type=F32)
            if first:
                acc_s[rows, :] = pv
            else:
                alpha = jnp.exp2(m_prev - m_new)
                acc_s[rows, :] = jnp.concatenate([alpha, alpha], axis=1) * acc_s[rows, :] + pv
            m_s[rows, :] = m_new

    update(i, 0, 0.0, first=True)

    @pl.when(i >= 1)
    def _():
        update(i - 1, 1, 0.0)

    far = rb_ref[N_BUCKETS // 2 - 1, h] * LOG2E

    def far_step(j, carry):
        update(j, None, far)
        return carry

    lax.fori_loop(0, jnp.maximum(i - 1, 0), far_step, 0)

    lam = _lambda(l1q_ref, l1k_ref, l2q_ref, l2k_ref, lam_init)
    acc = acc_s[...]
    o_ref[...] = _diff_finish(acc[:, :hd], acc[:, hd:], lam, g_ref[...], lam_init).astype(BF16)


def _prompt_attn(rel_bias, q, k, v, lams, g_head, *, batch, seq, tile, lam_init):
    assert tile >= MAX_DISTANCE and tile % ATTN_CHUNK == 0 and seq % tile == 0
    nq = seq // tile
    hd = 2 * DA_DH
    lam_spec = _const_spec((1, DA_DH))
    return pl.pallas_call(
        functools.partial(_prompt_attn_kernel, tile=tile, split_rows=ATTN_SPLIT_ROWS, lam_init=lam_init),
        grid=(DA_HEADS, batch, nq),
        in_specs=[pl.BlockSpec(memory_space=pltpu.SMEM),
                  pl.BlockSpec((tile, hd), lambda h, b, i: (b * nq + i, h)),
                  pl.BlockSpec((seq, hd), lambda h, b, i: (b, h)),
                  pl.BlockSpec((seq, hd), lambda h, b, i: (b, h)),
                  lam_spec, lam_spec, lam_spec, lam_spec, _const_spec((1, hd))],
        out_specs=pl.BlockSpec((tile, hd), lambda h, b, i: (b * nq + i, h)),
        out_shape=jax.ShapeDtypeStruct((batch * seq, DA_Q), BF16),
        scratch_shapes=[pltpu.VMEM((2, 2 * tile, tile), F32), pltpu.VMEM((2 * tile, LANES), F32),
                        pltpu.VMEM((2 * tile, 2 * hd), F32)],
        compiler_params=_params(3), name="diff_attn_prompt",
    )(rel_bias, q, k, v, *lams, g_head)


def _sample_attn_kernel(rb_ref, q_ref, kp_ref, vp_ref, kn_ref, vn_ref, l1q_ref, l1k_ref, l2q_ref, l2k_ref,
                        g_ref, o_ref, bias_p, bias_n, *, past, seq, lam_init):
    h = pl.program_id(0)
    b = pl.program_id(1)
    nt = lambda a, bb: lax.dot_general(a, bb, (((1,), (1,)), ((), ())), preferred_element_type=F32)

    @pl.when(b == 0)
    def _():
        qp = lax.broadcasted_iota(jnp.int32, (seq, past), 0) + past
        kp = lax.broadcasted_iota(jnp.int32, (seq, past), 1)
        tile = _rel_bias_tile(rb_ref, h, qp, kp)
        bias_p[...] = jnp.concatenate([tile, tile], axis=0)
        qn = lax.broadcasted_iota(jnp.int32, (seq, seq), 0) + past
        kn = lax.broadcasted_iota(jnp.int32, (seq, seq), 1) + past
        tile = _rel_bias_tile(rb_ref, h, qn, kn)
        bias_n[...] = jnp.concatenate([tile, tile], axis=0)

    qs = _stack_maps(q_ref[...])
    s_p = nt(qs, kp_ref[0].astype(BF16)) + bias_p[...]
    s_n = nt(qs, kn_ref[...]) + bias_n[...]
    m = jnp.maximum(jnp.max(s_p, axis=-1, keepdims=True), jnp.max(s_n, axis=-1, keepdims=True))
    p_p = jnp.exp2(s_p - m)
    p_n = jnp.exp2(s_n - m)
    l = jnp.sum(p_p, axis=-1, keepdims=True) + jnp.sum(p_n, axis=-1, keepdims=True)
    acc = (jnp.dot(p_p.astype(BF16), vp_ref[0].astype(BF16), preferred_element_type=F32)
           + jnp.dot(p_n.astype(BF16), vn_ref[...], preferred_element_type=F32))
    lam = _lambda(l1q_ref, l1k_ref, l2q_ref, l2k_ref, lam_init)
    o_ref[...] = _diff_finish(acc, l, lam, g_ref[...], lam_init).astype(BF16)


def _sample_attn(rel_bias, q, cache_k, cache_v, k_new, v_new, lams, g_head, *, batch, seq, lam_init):
    past = cache_k.shape[1]
    hd = 2 * DA_DH
    lam_spec = _const_spec((1, DA_DH))
    new_spec = pl.BlockSpec((seq, hd), lambda h, b: (b, h))
    past_spec = pl.BlockSpec((1, past, hd), lambda h, b: (b, 0, h))
    return pl.pallas_call(
        functools.partial(_sample_attn_kernel, past=past, seq=seq, lam_init=lam_init),
        grid=(DA_HEADS, batch),
        in_specs=[pl.BlockSpec(memory_space=pltpu.SMEM), new_spec, past_spec, past_spec, new_spec, new_spec,
                  lam_spec, lam_spec, lam_spec, lam_spec, _const_spec((1, hd))],
        out_specs=new_spec,
        out_shape=jax.ShapeDtypeStruct((batch * seq, DA_Q), BF16),
        scratch_shapes=[pltpu.VMEM((2 * seq, past), F32), pltpu.VMEM((2 * seq, seq), F32)],
        compiler_params=_params(2), name="diff_attn_sample",
    )(rel_bias, q, cache_k, cache_v, k_new, v_new, *lams, g_head)


def _prep_weights(norm_g, mlstm_w_in, mlstm_b_gate, mlstm_g_head, mlstm_w_out, kv_g, kv_w, diff_w_q,
                  diff_lam_q1, diff_lam_k1, diff_lam_q2, diff_lam_k2, diff_g_head, diff_w_o,
                  ffn_w_in, ffn_w_out):
    n_main = 2 * M_QK + M_V + D_MODEL
    w_in = mlstm_w_in[0]
    wg = jnp.pad(w_in[:, n_main:], ((0, 0), (0, LANES - 2 * M_HEADS)))
    wg_hi = wg.astype(BF16)
    wg_lo = (wg - wg_hi.astype(F32)).astype(BF16)
    row = lambda a: a.reshape(1, -1)
    return dict(
        norm=[[row(norm_g[l, j]) for j in range(4)] for l in range(2)],
        w_main=w_in[:, :n_main].astype(BF16),
        w_gate=jnp.concatenate([wg_hi, wg_lo], axis=1),
        b_gate=jnp.pad(mlstm_b_gate[0], (0, LANES - 2 * M_HEADS)).reshape(1, LANES),
        g_mhead=row(mlstm_g_head[0]),
        w_mout=mlstm_w_out[0].astype(BF16),
        kv_g=row(kv_g), kv_w=kv_w.astype(BF16), w_q=diff_w_q[0].astype(BF16),
        lams=[row(diff_lam_q1[0]), row(diff_lam_k1[0]), row(diff_lam_q2[0]), row(diff_lam_k2[0])],
        g_dhead=row(diff_g_head[0]), w_o=diff_w_o[0].astype(BF16),
        ffn_in=[ffn_w_in[l].astype(BF16) for l in range(2)],
        ffn_out=[ffn_w_out[l].astype(BF16) for l in range(2)],
    )


def _trunk(x, c0, n0, m0, past_k, past_v, rel_bias, w, *, row_tile, chunk, attn_tile):
    batch, seq, _ = x.shape
    rows = batch * seq
    x2d = x.reshape(rows, D_MODEL)
    lam_init = 0.8 - 0.6 * math.exp(-0.3 * 1)

    q, k, v, o, gcol, grow = _inproj(x2d, w["norm"][0][0], w["w_main"], w["w_gate"], w["b_gate"],
                                     tile=row_tile, chunk=chunk)
    m0b = jnp.broadcast_to(m0[..., None], (batch, M_HEADS, LANES))
    x1, c_new, n_new, m_new = _mlstm(q, k, v, o, gcol, grow, x2d, c0, n0, m0b, w["w_mout"], w["g_mhead"],
                                     w["norm"][0][1], batch=batch, chunk=chunk)
    x2 = _ffn(x1, w["norm"][0][2], w["ffn_in"][0], w["ffn_out"][0], w["norm"][0][3], tile=row_tile)

    aq, k_f32, v_f32, k_bf, v_bf = _qkv(x2, w["norm"][1][0], w["kv_g"], w["w_q"], w["kv_w"], tile=row_tile)
    if past_k is None:
        attn = _prompt_attn(rel_bias, aq, k_bf, v_bf, w["lams"], w["g_dhead"],
                            batch=batch, seq=seq, tile=attn_tile, lam_init=lam_init)
    else:
        past = past_k.shape[1]
        attn = _sample_attn(rel_bias, aq, past_k.reshape(batch, past, DA_Q), past_v.reshape(batch, past, DA_Q),
                            k_bf, v_bf, w["lams"], w["g_dhead"], batch=batch, seq=seq, lam_init=lam_init)
    y = _ffn(x2, w["norm"][1][2], w["ffn_in"][1], w["ffn_out"][1], w["norm"][1][3], tile=row_tile,
             attn=attn, w_o=w["w_o"], g_pre=w["norm"][1][1])

    return (y.reshape(batch, seq, D_MODEL), c_new[None], n_new[None], m_new[None, :, :, 0],
            k_f32.reshape(batch, seq, DA_HEADS, 2, DA_DH), v_f32.reshape(batch, seq, DA_HEADS, 2 * DA_DH))


def kernel(x_prompt, x_sample, state_C, state_n, state_m, cache_k, cache_v, norm_g, mlstm_w_in, mlstm_b_gate, mlstm_g_head, mlstm_w_out, kv_g, kv_w, rel_bias, diff_w_q, diff_lam_q1, diff_lam_k1, diff_lam_q2, diff_lam_k2, diff_g_head, diff_w_o, ffn_w_in, ffn_w_out):
    w = _prep_weights(norm_g, mlstm_w_in, mlstm_b_gate, mlstm_g_head, mlstm_w_out, kv_g, kv_w, diff_w_q,
                      diff_lam_q1, diff_lam_k1, diff_lam_q2, diff_lam_k2, diff_g_head, diff_w_o,
                      ffn_w_in, ffn_w_out)
    pb = x_prompt.shape[0]
    zeros = lambda *s: jnp.zeros(s, F32)
    y_p, p_c, p_n, p_m, p_k, p_v = _trunk(
        x_prompt, zeros(pb, M_HEADS, M_DV, M_DK), zeros(pb, M_HEADS, M_DK), zeros(pb, M_HEADS),
        None, None, rel_bias, w, row_tile=512, chunk=256, attn_tile=256)
    sb, ss, _ = x_sample.shape
    y_s, s_c, s_n, s_m, s_k, s_v = _trunk(
        x_sample, state_C[0], state_n[0], state_m[0], cache_k, cache_v, rel_bias, w,
        row_tile=sb * ss, chunk=ss, attn_tile=None)
    return (y_p, y_s, p_c, p_n, p_m, p_k, p_v, s_c, s_n, s_m, s_k, s_v)
```

```python
import functools
import math

import jax
import jax.numpy as jnp
from jax import lax
from jax.experimental import pallas as pl
from jax.experimental.pallas import tpu as pltpu

F32 = jnp.float32
BF16 = jnp.bfloat16

D_MODEL = 1024
ATTN_CHUNK = 64
M_HEADS = 4
M_DK = D_MODEL // (2 * M_HEADS)
M_DV = D_MODEL // M_HEADS
M_QK = M_HEADS * M_DK
M_V = M_HEADS * M_DV
DA_HEADS = 8
DA_DH = D_MODEL // (2 * DA_HEADS)
DA_Q = DA_HEADS * 2 * DA_DH
FFN_HIDDEN = -(-8 * D_MODEL // (3 * 256)) * 256
N_BUCKETS = 32
MAX_DISTANCE = 128
EPS = 1e-6
LOG2E = math.log2(math.e)

LANES = 128
GATE_ROWS = 16
VMEM_LIMIT = 56 * 1024 * 1024
ATTN_SPLIT_ROWS = 256
FFN_COL_CHUNKS = (1024, 1024, 768)
assert sum(FFN_COL_CHUNKS) == FFN_HIDDEN


def _bucket_upper_bounds():
    half = N_BUCKETS // 2
    max_exact = half // 2
    ratio = MAX_DISTANCE // max_exact
    steps = half - max_exact
    bounds = [n + 1 for n in range(max_exact)]
    for k in range(1, steps):
        n = max_exact
        while n ** steps < (max_exact ** steps) * (ratio ** k):
            n += 1
        bounds.append(n)
    return bounds


_BUCKET_BOUNDS = _bucket_upper_bounds()


def _rms(x, g):
    return x * lax.rsqrt(jnp.mean(x * x, axis=-1, keepdims=True) + EPS) * g


def _log_sigmoid(x):
    return jnp.minimum(x, 0.0) - jnp.log1p(jnp.exp(-jnp.abs(x)))


def _split3(a):
    a1 = a.astype(BF16)
    r1 = a - a1.astype(F32)
    a2 = r1.astype(BF16)
    a3 = (r1 - a2.astype(F32)).astype(BF16)
    return a1, a2, a3


def _const_spec(shape):
    return pl.BlockSpec(shape, lambda *_: (0,) * len(shape), pipeline_mode=pl.Buffered(1))


def _params(n_axes):
    return pltpu.CompilerParams(dimension_semantics=("arbitrary",) * n_axes,
                                vmem_limit_bytes=VMEM_LIMIT)


def _inproj_kernel(x_ref, g_ref, w_ref, wg_ref, bg_ref,
                   q_ref, k_ref, v_ref, o_ref, gcol_ref, grow_ref, *, chunk):
    xn = _rms(x_ref[...], g_ref[...])
    xh = xn.astype(BF16)
    xl = (xn - xh.astype(F32)).astype(BF16)
    dot = functools.partial(jnp.dot, preferred_element_type=F32)
    q_ref[...] = dot(xh, w_ref[:, :M_QK]).astype(BF16)
    k_ref[...] = (dot(xh, w_ref[:, M_QK:2 * M_QK]) * (M_DK ** -0.5)).astype(BF16)
    v_ref[...] = dot(xh, w_ref[:, 2 * M_QK:2 * M_QK + M_V]).astype(BF16)
    o_ref[...] = dot(xh, w_ref[:, 2 * M_QK + M_V:])
    gh = dot(xh, wg_ref[...])
    gl = dot(xl, wg_ref[:, :LANES])
    gates = gh[:, :LANES] + gh[:, LANES:] + gl + bg_ref[...]
    lane = lax.broadcasted_iota(jnp.int32, gates.shape, 1)
    gv = jnp.where(lane < M_HEADS, gates, _log_sigmoid(gates))
    gcol_ref[...] = gv
    gt = gv.T
    for j in range(grow_ref.shape[0]):
        grow_ref[j] = gt[:GATE_ROWS, j * chunk:(j + 1) * chunk]


def _inproj(x2d, g, w_main, w_gate, b_gate, *, tile, chunk):
    rows = x2d.shape[0]
    grid = (rows // tile,)
    row_spec = lambda cols: pl.BlockSpec((tile, cols), lambda i: (i, 0))
    return pl.pallas_call(
        functools.partial(_inproj_kernel, chunk=chunk),
        grid=grid,
        in_specs=[row_spec(D_MODEL), _const_spec((1, D_MODEL)),
                  _const_spec(w_main.shape), _const_spec(w_gate.shape), _const_spec((1, LANES))],
        out_specs=[row_spec(M_QK), row_spec(M_QK), row_spec(M_V), row_spec(D_MODEL), row_spec(LANES),
                   pl.BlockSpec((tile // chunk, GATE_ROWS, chunk), lambda i: (i, 0, 0))],
        out_shape=[jax.ShapeDtypeStruct((rows, M_QK), BF16), jax.ShapeDtypeStruct((rows, M_QK), BF16),
                   jax.ShapeDtypeStruct((rows, M_V), BF16), jax.ShapeDtypeStruct((rows, D_MODEL), F32),
                   jax.ShapeDtypeStruct((rows, LANES), F32),
                   jax.ShapeDtypeStruct((rows // chunk, GATE_ROWS, chunk), F32)],
        compiler_params=_params(1),
        name="mlstm_inproj",
    )(x2d, g, w_main, w_gate, b_gate)


def _mlstm_kernel(q_ref, k_ref, v_ref, o_ref, gcol_ref, grow_ref, x_ref, c0_ref, n0_ref, m0_ref,
                  wout_ref, ghead_ref, gpost_ref,
                  x1_ref, cout_ref, nout_ref, mout_ref,
                  ct_s, n_s, m_s, hcat_s, *, chunk, n_chunks):
    L = chunk
    c = pl.program_id(1)
    dot = functools.partial(jnp.dot, preferred_element_type=F32)

    @pl.when(c == 0)
    def _():
        for h in range(M_HEADS):
            ct_s[h] = c0_ref[0, h].T
        n_s[...] = n0_ref[0]
        m_s[...] = m0_ref[0]

    row = lax.broadcasted_iota(jnp.int32, (L, L), 0)
    col = lax.broadcasted_iota(jnp.int32, (L, L), 1)
    causal = col <= row
    tri = jnp.where(causal, 1.0, 0.0).astype(BF16)
    tri_t = jnp.where(row <= col, 1.0, 0.0).astype(BF16)
    gcol = gcol_ref[...]
    grow = grow_ref[0]
    bcol = sum(dot(tri, part) for part in _split3(gcol))
    brow = sum(dot(part, tri_t) for part in _split3(grow))

    for h in range(M_HEADS):
        q = q_ref[:, h * M_DK:(h + 1) * M_DK]
        k = k_ref[:, h * M_DK:(h + 1) * M_DK]
        v = v_ref[:, h * M_DV:(h + 1) * M_DV]
        ig_row = grow[h:h + 1, :]
        b_row = brow[M_HEADS + h:M_HEADS + h + 1, :]
        ig_col = gcol[:, h:h + 1]
        b_col = bcol[:, M_HEADS + h:M_HEADS + h + 1]
        m0 = m_s[h:h + 1, 0:1]
        n0 = n_s[h:h + 1, :]
        ct = ct_s[h]

        d = jnp.where(causal, b_col - b_row + ig_row, -jnp.inf)
        g = b_col + m0
        m = jnp.maximum(g, jnp.max(d, axis=-1, keepdims=True))
        w_inter = jnp.exp(g - m)
        qk = lax.dot_general(q, k, (((1,), (1,)), ((), ())), preferred_element_type=F32)
        s = jnp.exp(d - m) * qk
        num = w_inter * dot(q, ct.astype(BF16)) + dot(s.astype(BF16), v)
        qn = jnp.sum(q.astype(F32) * n0, axis=-1, keepdims=True)
        den = w_inter * qn + jnp.sum(s, axis=-1, keepdims=True)
        hh = num / jnp.maximum(jnp.abs(den), jnp.exp(-m))

        m_last = m[L - 1:L, :]
        w_state = jnp.exp(g[L - 1:L, :] - m_last)
        w_rows = jnp.exp(b_col[L - 1:L, :] - b_col + ig_col - m_last)
        vw = (v.astype(F32) * w_rows).astype(BF16)
        upd = lax.dot_general(k, vw, (((0,), (0,)), ((), ())), preferred_element_type=F32)
        ct_s[h] = w_state * ct + upd
        n_s[h:h + 1, :] = w_state * n0 + jnp.sum(k.astype(F32) * w_rows, axis=0, keepdims=True)
        m_s[h:h + 1, :] = jnp.broadcast_to(m_last, (1, LANES))

        hn = _rms(hh, ghead_ref[:, h * M_DV:(h + 1) * M_DV])
        gate = jax.nn.sigmoid(o_ref[:, h * M_DV:(h + 1) * M_DV])
        hcat_s[:, h * M_DV:(h + 1) * M_DV] = (hn * gate).astype(BF16)

    y = dot(hcat_s[...], wout_ref[...])
    x1_ref[...] = x_ref[...] + _rms(y, gpost_ref[...])

    @pl.when(c == n_chunks - 1)
    def _():
        for h in range(M_HEADS):
            cout_ref[0, h] = ct_s[h].T
        nout_ref[0] = n_s[...]
        mout_ref[0] = m_s[...]


def _mlstm(q, k, v, o, gcol, grow, x2d, c0, n0, m0b, w_out, g_head, g_post, *, batch, chunk):
    rows = x2d.shape[0]
    n_chunks = rows // batch // chunk
    grid = (batch, n_chunks)
    row_spec = lambda cols: pl.BlockSpec((chunk, cols), lambda b, c: (b * n_chunks + c, 0))
    state4 = pl.BlockSpec((1, M_HEADS, M_DV, M_DK), lambda b, c: (b, 0, 0, 0))
    state3 = pl.BlockSpec((1, M_HEADS, LANES), lambda b, c: (b, 0, 0))
    return pl.pallas_call(
        functools.partial(_mlstm_kernel, chunk=chunk, n_chunks=n_chunks),
        grid=grid,
        in_specs=[row_spec(M_QK), row_spec(M_QK), row_spec(M_V), row_spec(D_MODEL), row_spec(LANES),
                  pl.BlockSpec((1, GATE_ROWS, chunk), lambda b, c: (b * n_chunks + c, 0, 0)),
                  row_spec(D_MODEL), state4, state3, state3,
                  _const_spec((M_V, D_MODEL)), _const_spec((1, M_V)), _const_spec((1, D_MODEL))],
        out_specs=[row_spec(D_MODEL), state4, state3, state3],
        out_shape=[jax.ShapeDtypeStruct((rows, D_MODEL), F32),
                   jax.ShapeDtypeStruct((batch, M_HEADS, M_DV, M_DK), F32),
                   jax.ShapeDtypeStruct((batch, M_HEADS, LANES), F32),
                   jax.ShapeDtypeStruct((batch, M_HEADS, LANES), F32)],
        scratch_shapes=[pltpu.VMEM((M_HEADS, M_DK, M_DV), F32), pltpu.VMEM((M_HEADS, LANES), F32),
                        pltpu.VMEM((M_HEADS, LANES), F32), pltpu.VMEM((chunk, M_V), BF16)],
        compiler_params=_params(2),
        name="mlstm_scan",
    )(q, k, v, o, gcol, grow, x2d, c0, n0, m0b, w_out, g_head, g_post)


def _ffn_body(x, gin_ref, win_ref, wout_ref, gout_ref):
    dot = functools.partial(jnp.dot, preferred_element_type=F32)
    xn = _rms(x, gin_ref[...]).astype(BF16)
    acc = None
    start = 0
    for width in FFN_COL_CHUNKS:
        gate = dot(xn, win_ref[:, start:start + width])
        up = dot(xn, win_ref[:, FFN_HIDDEN + start:FFN_HIDDEN + start + width])
        act = (jax.nn.silu(gate) * up).astype(BF16)
        part = dot(act, wout_ref[start:start + width, :])
        acc = part if acc is None else acc + part
        start += width
    return x + _rms(acc, gout_ref[...])


def _ffn_kernel(x_ref, gin_ref, win_ref, wout_ref, gout_ref, y_ref):
    y_ref[...] = _ffn_body(x_ref[...], gin_ref, win_ref, wout_ref, gout_ref)


def _attn_out_ffn_kernel(x_ref, a_ref, wo_ref, gpre_ref, gin_ref, win_ref, wout_ref, gout_ref, y_ref):
    mix = jnp.dot(a_ref[...], wo_ref[...], preferred_element_type=F32)
    x = x_ref[...] + _rms(mix, gpre_ref[...])
    y_ref[...] = _ffn_body(x, gin_ref, win_ref, wout_ref, gout_ref)


def _ffn(x2d, g_in, w_in, w_out, g_out, *, tile, attn=None, w_o=None, g_pre=None):
    rows = x2d.shape[0]
    row_spec = lambda cols: pl.BlockSpec((tile, cols), lambda i: (i, 0))
    vec = _const_spec((1, D_MODEL))
    ffn_specs = [vec, _const_spec(w_in.shape), _const_spec(w_out.shape), vec]
    if attn is None:
        kern, in_specs = _ffn_kernel, [row_spec(D_MODEL)] + ffn_specs
        args = (x2d, g_in, w_in, w_out, g_out)
    else:
        kern = _attn_out_ffn_kernel
        in_specs = [row_spec(D_MODEL), row_spec(DA_Q), _const_spec(w_o.shape), vec] + ffn_specs
        args = (x2d, attn, w_o, g_pre, g_in, w_in, w_out, g_out)
    return pl.pallas_call(
        kern, grid=(rows // tile,), in_specs=in_specs, out_specs=row_spec(D_MODEL),
        out_shape=jax.ShapeDtypeStruct((rows, D_MODEL), F32),
        compiler_params=_params(1), name="ffn",
    )(*args)


def _qkv_kernel(x_ref, gq_ref, gkv_ref, wq_ref, wkv_ref, q_ref, kf_ref, vf_ref, kb_ref, vb_ref, *, k_transposed):
    x = x_ref[...]
    xhat = x * lax.rsqrt(jnp.mean(x * x, axis=-1, keepdims=True) + EPS)
    xq = (xhat * gq_ref[...]).astype(BF16)
    xkv = (xhat * gkv_ref[...]).astype(BF16)
    dot = functools.partial(jnp.dot, preferred_element_type=F32)
    q_ref[...] = (dot(xq, wq_ref[...]) * (DA_DH ** -0.5 * LOG2E)).astype(BF16)
    kk = dot(xkv, wkv_ref[:, :DA_Q])
    if k_transposed:
        kt = kk.T
        kf_ref[0] = kt
        kb_ref[0] = kt.astype(BF16)
    else:
        kf_ref[...] = kk
        kb_ref[...] = kk.astype(BF16)
    vv = dot(xkv, wkv_ref[:, DA_Q:])
    vf_ref[...] = vv
    vb_ref[...] = vv.astype(BF16)


def _qkv(x2d, g_q, g_kv, w_q, w_kv, *, tile, batch, k_transposed):
    rows = x2d.shape[0]
    seq = rows // batch
    row_spec = pl.BlockSpec((tile, D_MODEL), lambda i: (i, 0))
    vec = _const_spec((1, D_MODEL))
    shp = lambda dt: jax.ShapeDtypeStruct((rows, DA_Q), dt)
    if k_transposed:
        tiles_per_seq = seq // tile
        k_spec = pl.BlockSpec((1, DA_Q, tile), lambda i: (i // tiles_per_seq, 0, i % tiles_per_seq))
        k_shp = lambda dt: jax.ShapeDtypeStruct((batch, DA_Q, seq), dt)
    else:
        k_spec, k_shp = row_spec, shp
    return pl.pallas_call(
        functools.partial(_qkv_kernel, k_transposed=k_transposed), grid=(rows // tile,),
        in_specs=[row_spec, vec, vec, _const_spec(w_q.shape), _const_spec(w_kv.shape)],
        out_specs=[row_spec, k_spec, row_spec, k_spec, row_spec],
        out_shape=[shp(BF16), k_shp(F32), shp(F32), k_shp(BF16), shp(BF16)],
        compiler_params=_params(1), name="attn_qkv",
    )(x2d, g_q, g_kv, w_q, w_kv)


def _rel_bias_tile(rb_ref, head, q_pos, k_pos):
    rel = k_pos - q_pos
    n = jnp.abs(rel)
    half = N_BUCKETS // 2

    def table(offset):
        val = jnp.full(rel.shape, rb_ref[offset + half - 1, head], F32)
        for bucket in reversed(range(half - 1)):
            val = jnp.where(n < _BUCKET_BOUNDS[bucket], rb_ref[offset + bucket, head], val)
        return val

    bias = jnp.where(rel > 0, table(half), table(0)) * LOG2E
    shift = ATTN_CHUNK.bit_length() - 1
    visible = jnp.right_shift(k_pos, shift) <= jnp.right_shift(q_pos, shift)
    return jnp.where(visible, bias, -jnp.inf)


def _lambda(l1q_ref, l1k_ref, l2q_ref, l2k_ref, lam_init):
    a = jnp.exp(jnp.sum(l1q_ref[...] * l1k_ref[...], axis=-1, keepdims=True))
    b = jnp.exp(jnp.sum(l2q_ref[...] * l2k_ref[...], axis=-1, keepdims=True))
    return a - b + lam_init


def _stack_maps(q):
    lane = lax.broadcasted_iota(jnp.int32, q.shape, 1)
    zero = jnp.zeros_like(q)
    return jnp.concatenate([jnp.where(lane < DA_DH, q, zero), jnp.where(lane >= DA_DH, q, zero)], axis=0)


def _diff_finish(acc, l, lam, g, lam_init):
    t = acc.shape[0] // 2
    o = acc[:t] / l[:t] - lam * (acc[t:] / l[t:])
    return _rms(o, g) * (1.0 - lam_init)


def _prompt_attn_kernel(rb_ref, q_ref, k_ref, v_ref, l1q_ref, l1k_ref, l2q_ref, l2k_ref, g_ref,
                        o_ref, bias_s, qm_s, m_s, acc_s, *, tile, n_tiles, unroll, lam_init):
    T = tile
    hd = 2 * DA_DH
    h = pl.program_id(0)
    b = pl.program_id(1)
    dot = functools.partial(jnp.dot, preferred_element_type=F32)

    @pl.when(b == 0)
    def _():
        qp = lax.broadcasted_iota(jnp.int32, (T, T), 0) + T
        kp = lax.broadcasted_iota(jnp.int32, (T, T), 1)
        diag = _rel_bias_tile(rb_ref, h, qp, kp + T)
        below = _rel_bias_tile(rb_ref, h, qp, kp)
        bias_s[0] = jnp.concatenate([diag, diag], axis=0)
        bias_s[1] = jnp.concatenate([below, below], axis=0)
        bias_s[2] = jnp.full((2 * T, T), rb_ref[N_BUCKETS // 2 - 1, h] * LOG2E, F32)

    q = q_ref[...]
    lane = lax.broadcasted_iota(jnp.int32, q.shape, 1)
    zero = jnp.zeros_like(q)
    qm_s[0] = jnp.where(lane < DA_DH, q, zero)
    qm_s[1] = jnp.where(lane >= DA_DH, q, zero)
    m_s[...] = jnp.full(m_s.shape, -jnp.inf, F32)
    acc_s[...] = jnp.zeros(acc_s.shape, F32)

    def pair(i, jj):
        q_rows = pl.ds(pl.multiple_of(i * T, T), T)
        k_pos = pl.ds(pl.multiple_of((i - jj) * T, T), T)
        kb = k_ref[0, :, k_pos]
        vb = v_ref[k_pos, :]
        vext = jnp.concatenate([vb, jnp.ones_like(vb)], axis=1)
        s = jnp.concatenate([dot(qm_s[0, q_rows, :], kb), dot(qm_s[1, q_rows, :], kb)], axis=0)
        s = s + bias_s[jnp.minimum(jj, 2)]
        m_prev = m_s[i]
        m_new = jnp.maximum(m_prev, jnp.max(s, axis=-1, keepdims=True))
        alpha = jnp.exp2(m_prev - m_new)
        p = jnp.exp2(s - jnp.concatenate([m_new] * (T // LANES), axis=1))
        pv = jnp.dot(p.astype(BF16), vext, preferred_element_type=F32)
        acc_s[i] = jnp.concatenate([alpha, alpha], axis=1) * acc_s[i] + pv
        m_s[i] = m_new

    def body(_, carry):
        i, jj = carry
        for _u in range(unroll):
            pair(i, jj)
            wrap = jj >= i
            i, jj = jnp.where(wrap, i + 1, i), jnp.where(wrap, 0, jj + 1)
        return i, jj

    n_pairs = n_tiles * (n_tiles + 1) // 2
    lax.fori_loop(0, n_pairs // unroll, body, (jnp.int32(0), jnp.int32(0)))

    lam = _lambda(l1q_ref, l1k_ref, l2q_ref, l2k_ref, lam_init)
    for i in range(n_tiles):
        acc = acc_s[i]
        o_ref[i * T:(i + 1) * T, :] = _diff_finish(acc[:, :hd], acc[:, hd:], lam, g_ref[...], lam_init).astype(BF16)


def _prompt_attn(rel_bias, q, k, v, lams, g_head, *, batch, seq, tile, lam_init):
    assert tile >= MAX_DISTANCE and tile % ATTN_CHUNK == 0 and seq % tile == 0
    nq = seq // tile
    n_pairs = nq * (nq + 1) // 2
    unroll = max(u for u in (8, 4, 2, 1) if n_pairs % u == 0)
    hd = 2 * DA_DH
    lam_spec = _const_spec((1, DA_DH))
    seq_spec = pl.BlockSpec((seq, hd), lambda h, b: (b, h))
    return pl.pallas_call(
        functools.partial(_prompt_attn_kernel, tile=tile, n_tiles=nq, unroll=unroll, lam_init=lam_init),
        grid=(DA_HEADS, batch),
        in_specs=[pl.BlockSpec(memory_space=pltpu.SMEM), seq_spec,
                  pl.BlockSpec((1, hd, seq), lambda h, b: (b, h, 0)), seq_spec,
                  lam_spec, lam_spec, lam_spec, lam_spec, _const_spec((1, hd))],
        out_specs=seq_spec,
        out_shape=jax.ShapeDtypeStruct((batch * seq, DA_Q), BF16),
        scratch_shapes=[pltpu.VMEM((3, 2 * tile, tile), F32), pltpu.VMEM((2, seq, hd), BF16),
                        pltpu.VMEM((nq, 2 * tile, LANES), F32), pltpu.VMEM((nq, 2 * tile, 2 * hd), F32)],
        compiler_params=_params(2), name="diff_attn_prompt",
    )(rel_bias, q, k, v, *lams, g_head)


def _sample_attn_kernel(rb_ref, q_ref, kp_ref, vp_ref, kn_ref, vn_ref, l1q_ref, l1k_ref, l2q_ref, l2k_ref,
                        g_ref, o_ref, bias_p, bias_n, *, past, seq, lam_init):
    b = pl.program_id(0)
    hd = 2 * DA_DH
    dot = functools.partial(jnp.dot, preferred_element_type=F32)
    nt = lambda a, bb: lax.dot_general(a, bb, (((1,), (1,)), ((), ())), preferred_element_type=F32)

    @pl.when(b == 0)
    def _():
        qp = lax.broadcasted_iota(jnp.int32, (seq, past), 0) + past
        kp = lax.broadcasted_iota(jnp.int32, (seq, past), 1)
        qn = lax.broadcasted_iota(jnp.int32, (seq, seq), 0) + past
        kn = lax.broadcasted_iota(jnp.int32, (seq, seq), 1) + past
        for h in range(DA_HEADS):
            tile = _rel_bias_tile(rb_ref, h, qp, kp)
            bias_p[h] = jnp.concatenate([tile, tile], axis=0)
            tile = _rel_bias_tile(rb_ref, h, qn, kn)
            bias_n[h] = jnp.concatenate([tile, tile], axis=0)

    lam = _lambda(l1q_ref, l1k_ref, l2q_ref, l2k_ref, lam_init)
    for h in range(DA_HEADS):
        cols = slice(h * hd, (h + 1) * hd)
        qs = _stack_maps(q_ref[:, cols])
        s_p = dot(qs, kp_ref[0, cols, :].astype(BF16)) + bias_p[h]
        s_n = nt(qs, kn_ref[:, cols]) + bias_n[h]
        m = jnp.maximum(jnp.max(s_p, axis=-1, keepdims=True), jnp.max(s_n, axis=-1, keepdims=True))
        p_p = jnp.exp2(s_p - m)
        p_n = jnp.exp2(s_n - m)
        l = jnp.sum(p_p, axis=-1, keepdims=True) + jnp.sum(p_n, axis=-1, keepdims=True)
        acc = (dot(p_p.astype(BF16), vp_ref[0, :, h, :].astype(BF16))
               + dot(p_n.astype(BF16), vn_ref[:, cols]))
        o_ref[:, cols] = _diff_finish(acc, l, lam, g_ref[...], lam_init).astype(BF16)


def _sample_attn(rel_bias, q, cache_kt, cache_v, k_new, v_new, lams, g_head, *, batch, seq, lam_init):
    past = cache_kt.shape[2]
    hd = 2 * DA_DH
    lam_spec = _const_spec((1, DA_DH))
    new_spec = pl.BlockSpec((seq, DA_Q), lambda b: (b, 0))
    return pl.pallas_call(
        functools.partial(_sample_attn_kernel, past=past, seq=seq, lam_init=lam_init),
        grid=(batch,),
        in_specs=[pl.BlockSpec(memory_space=pltpu.SMEM), new_spec,
                  pl.BlockSpec((1, DA_Q, past), lambda b: (b, 0, 0)),
                  pl.BlockSpec((1, past, DA_HEADS, hd), lambda b: (b, 0, 0, 0)),
                  new_spec, new_spec, lam_spec, lam_spec, lam_spec, lam_spec, _const_spec((1, hd))],
        out_specs=new_spec,
        out_shape=jax.ShapeDtypeStruct((batch * seq, DA_Q), BF16),
        scratch_shapes=[pltpu.VMEM((DA_HEADS, 2 * seq, past), F32), pltpu.VMEM((DA_HEADS, 2 * seq, seq), F32)],
        compiler_params=_params(1), name="diff_attn_sample",
    )(rel_bias, q, cache_kt, cache_v, k_new, v_new, *lams, g_head)


def _prep_weights(norm_g, mlstm_w_in, mlstm_b_gate, mlstm_g_head, mlstm_w_out, kv_g, kv_w, diff_w_q,
                  diff_lam_q1, diff_lam_k1, diff_lam_q2, diff_lam_k2, diff_g_head, diff_w_o,
                  ffn_w_in, ffn_w_out):
    n_main = 2 * M_QK + M_V + D_MODEL
    w_in = mlstm_w_in[0]
    wg = jnp.pad(w_in[:, n_main:], ((0, 0), (0, LANES - 2 * M_HEADS)))
    wg_hi = wg.astype(BF16)
    wg_lo = (wg - wg_hi.astype(F32)).astype(BF16)
    row = lambda a: a.reshape(1, -1)
    return dict(
        norm=[[row(norm_g[l, j]) for j in range(4)] for l in range(2)],
        w_main=w_in[:, :n_main].astype(BF16),
        w_gate=jnp.concatenate([wg_hi, wg_lo], axis=1),
        b_gate=jnp.pad(mlstm_b_gate[0], (0, LANES - 2 * M_HEADS)).reshape(1, LANES),
        g_mhead=row(mlstm_g_head[0]),
        w_mout=mlstm_w_out[0].astype(BF16),
        kv_g=row(kv_g), kv_w=kv_w.astype(BF16), w_q=diff_w_q[0].astype(BF16),
        lams=[row(diff_lam_q1[0]), row(diff_lam_k1[0]), row(diff_lam_q2[0]), row(diff_lam_k2[0])],
        g_dhead=row(diff_g_head[0]), w_o=diff_w_o[0].astype(BF16),
        ffn_in=[ffn_w_in[l].astype(BF16) for l in range(2)],
        ffn_out=[ffn_w_out[l].astype(BF16) for l in range(2)],
    )


def _trunk(x, c0, n0, m0, past_k, past_v, rel_bias, w, *, row_tile, chunk, attn_tile):
    batch, seq, _ = x.shape
    rows = batch * seq
    x2d = x.reshape(rows, D_MODEL)
    lam_init = 0.8 - 0.6 * math.exp(-0.3 * 1)

    q, k, v, o, gcol, grow = _inproj(x2d, w["norm"][0][0], w["w_main"], w["w_gate"], w["b_gate"],
                                     tile=row_tile, chunk=chunk)
    m0b = jnp.broadcast_to(m0[..., None], (batch, M_HEADS, LANES))
    x1, c_new, n_new, m_new = _mlstm(q, k, v, o, gcol, grow, x2d, c0, n0, m0b, w["w_mout"], w["g_mhead"],
                                     w["norm"][0][1], batch=batch, chunk=chunk)
    x2 = _ffn(x1, w["norm"][0][2], w["ffn_in"][0], w["ffn_out"][0], w["norm"][0][3], tile=row_tile)

    aq, k_f32, v_f32, k_bf, v_bf = _qkv(x2, w["norm"][1][0], w["kv_g"], w["w_q"], w["kv_w"], tile=row_tile,
                                        batch=batch, k_transposed=past_k is None)
    if past_k is None:
        attn = _prompt_attn(rel_bias, aq, k_bf, v_bf, w["lams"], w["g_dhead"],
                            batch=batch, seq=seq, tile=attn_tile, lam_init=lam_init)
        k_f32 = k_f32.reshape(batch, DA_HEADS, 2, DA_DH, seq).transpose(0, 4, 1, 2, 3)
    else:
        past = past_k.shape[1]
        past_kt = past_k.transpose(0, 2, 3, 4, 1).reshape(batch, DA_Q, past)
        attn = _sample_attn(rel_bias, aq, past_kt, past_v, k_bf, v_bf, w["lams"], w["g_dhead"],
                            batch=batch, seq=seq, lam_init=lam_init)
    y = _ffn(x2, w["norm"][1][2], w["ffn_in"][1], w["ffn_out"][1], w["norm"][1][3], tile=row_tile,
             attn=attn, w_o=w["w_o"], g_pre=w["norm"][1][1])

    return (y.reshape(batch, seq, D_MODEL), c_new[None], n_new[None], m_new[None, :, :, 0],
            k_f32.reshape(batch, seq, DA_HEADS, 2, DA_DH), v_f32.reshape(batch, seq, DA_HEADS, 2 * DA_DH))


def kernel(x_prompt, x_sample, state_C, state_n, state_m, cache_k, cache_v, norm_g, mlstm_w_in, mlstm_b_gate, mlstm_g_head, mlstm_w_out, kv_g, kv_w, rel_bias, diff_w_q, diff_lam_q1, diff_lam_k1, diff_lam_q2, diff_lam_k2, diff_g_head, diff_w_o, ffn_w_in, ffn_w_out):
    w = _prep_weights(norm_g, mlstm_w_in, mlstm_b_gate, mlstm_g_head, mlstm_w_out, kv_g, kv_w, diff_w_q,
                      diff_lam_q1, diff_lam_k1, diff_lam_q2, diff_lam_k2, diff_g_head, diff_w_o,
                      ffn_w_in, ffn_w_out)
    pb = x_prompt.shape[0]
    zeros = lambda *s: jnp.zeros(s, F32)
    y_p, p_c, p_n, p_m, p_k, p_v = _trunk(
        x_prompt, zeros(pb, M_HEADS, M_DV, M_DK), zeros(pb, M_HEADS, M_DK), zeros(pb, M_HEADS),
        None, None, rel_bias, w, row_tile=512, chunk=256, attn_tile=256)
    sb, ss, _ = x_sample.shape
    y_s, s_c, s_n, s_m, s_k, s_v = _trunk(
        x_sample, state_C[0], state_n[0], state_m[0], cache_k, cache_v, rel_bias, w,
        row_tile=sb * ss, chunk=ss, attn_tile=None)
    return (y_p, y_s, p_c, p_n, p_m, p_k, p_v, s_c, s_n, s_m, s_k, s_v)
```

```python
import functools
import math

import jax
import jax.numpy as jnp
from jax import lax
from jax.experimental import pallas as pl
from jax.experimental.pallas import tpu as pltpu

F32 = jnp.float32
BF16 = jnp.bfloat16

D_MODEL = 1024
ATTN_CHUNK = 64
M_HEADS = 4
M_DK = D_MODEL // (2 * M_HEADS)
M_DV = D_MODEL // M_HEADS
M_QK = M_HEADS * M_DK
M_V = M_HEADS * M_DV
DA_HEADS = 8
DA_DH = D_MODEL // (2 * DA_HEADS)
DA_Q = DA_HEADS * 2 * DA_DH
FFN_HIDDEN = -(-8 * D_MODEL // (3 * 256)) * 256
N_BUCKETS = 32
MAX_DISTANCE = 128
EPS = 1e-6
LOG2E = math.log2(math.e)

LANES = 128
GATE_ROWS = 16
VMEM_LIMIT = 56 * 1024 * 1024
ATTN_SPLIT_ROWS = 256
FFN_COL_CHUNKS = (1024, 1024, 768)
assert sum(FFN_COL_CHUNKS) == FFN_HIDDEN


def _bucket_upper_bounds():
    half = N_BUCKETS // 2
    max_exact = half // 2
    ratio = MAX_DISTANCE // max_exact
    steps = half - max_exact
    bounds = [n + 1 for n in range(max_exact)]
    for k in range(1, steps):
        n = max_exact
        while n ** steps < (max_exact ** steps) * (ratio ** k):
            n += 1
        bounds.append(n)
    return bounds


_BUCKET_BOUNDS = _bucket_upper_bounds()


def _rms(x, g):
    return x * lax.rsqrt(jnp.mean(x * x, axis=-1, keepdims=True) + EPS) * g


def _log_sigmoid(x):
    return jnp.minimum(x, 0.0) - jnp.log1p(jnp.exp(-jnp.abs(x)))


def _split3(a):
    a1 = a.astype(BF16)
    r1 = a - a1.astype(F32)
    a2 = r1.astype(BF16)
    a3 = (r1 - a2.astype(F32)).astype(BF16)
    return a1, a2, a3


def _const_spec(shape):
    return pl.BlockSpec(shape, lambda *_: (0,) * len(shape), pipeline_mode=pl.Buffered(1))


def _params(n_axes):
    return pltpu.CompilerParams(dimension_semantics=("arbitrary",) * n_axes,
                                vmem_limit_bytes=VMEM_LIMIT)


def _inproj_kernel(x_ref, g_ref, w_ref, wg_ref, bg_ref,
                   q_ref, k_ref, v_ref, o_ref, gcol_ref, grow_ref, *, chunk):
    xn = _rms(x_ref[...], g_ref[...])
    xh = xn.astype(BF16)
    xl = (xn - xh.astype(F32)).astype(BF16)
    dot = functools.partial(jnp.dot, preferred_element_type=F32)
    q_ref[...] = dot(xh, w_ref[:, :M_QK]).astype(BF16)
    k_ref[...] = (dot(xh, w_ref[:, M_QK:2 * M_QK]) * (M_DK ** -0.5)).astype(BF16)
    v_ref[...] = dot(xh, w_ref[:, 2 * M_QK:2 * M_QK + M_V]).astype(BF16)
    o_ref[...] = dot(xh, w_ref[:, 2 * M_QK + M_V:])
    gh = dot(xh, wg_ref[...])
    gl = dot(xl, wg_ref[:, :LANES])
    gates = gh[:, :LANES] + gh[:, LANES:] + gl + bg_ref[...]
    lane = lax.broadcasted_iota(jnp.int32, gates.shape, 1)
    gv = jnp.where(lane < M_HEADS, gates, _log_sigmoid(gates))
    gcol_ref[...] = gv
    gt = gv.T
    for j in range(grow_ref.shape[0]):
        grow_ref[j] = gt[:GATE_ROWS, j * chunk:(j + 1) * chunk]


def _inproj(x2d, g, w_main, w_gate, b_gate, *, tile, chunk):
    rows = x2d.shape[0]
    grid = (rows // tile,)
    row_spec = lambda cols: pl.BlockSpec((tile, cols), lambda i: (i, 0))
    return pl.pallas_call(
        functools.partial(_inproj_kernel, chunk=chunk),
        grid=grid,
        in_specs=[row_spec(D_MODEL), _const_spec((1, D_MODEL)),
                  _const_spec(w_main.shape), _const_spec(w_gate.shape), _const_spec((1, LANES))],
        out_specs=[row_spec(M_QK), row_spec(M_QK), row_spec(M_V), row_spec(D_MODEL), row_spec(LANES),
                   pl.BlockSpec((tile // chunk, GATE_ROWS, chunk), lambda i: (i, 0, 0))],
        out_shape=[jax.ShapeDtypeStruct((rows, M_QK), BF16), jax.ShapeDtypeStruct((rows, M_QK), BF16),
                   jax.ShapeDtypeStruct((rows, M_V), BF16), jax.ShapeDtypeStruct((rows, D_MODEL), F32),
                   jax.ShapeDtypeStruct((rows, LANES), F32),
                   jax.ShapeDtypeStruct((rows // chunk, GATE_ROWS, chunk), F32)],
        compiler_params=_params(1),
        name="mlstm_inproj",
    )(x2d, g, w_main, w_gate, b_gate)


def _mlstm_kernel(q_ref, k_ref, v_ref, o_ref, gcol_ref, grow_ref, x_ref, c0_ref, n0_ref, m0_ref,
                  wout_ref, ghead_ref, gpost_ref,
                  x1_ref, cout_ref, nout_ref, mout_ref,
                  ct_s, m_s, hcat_s, *, chunk, n_chunks):
    L = chunk
    c = pl.program_id(1)
    dot = functools.partial(jnp.dot, preferred_element_type=F32)

    def wide(rep, n):
        return rep[:, :n] if n < LANES else jnp.concatenate([rep] * (n // LANES), axis=1)

    @pl.when(c == 0)
    def _():
        for h in range(M_HEADS):
            ct_s[h, :, :M_DV] = c0_ref[0, h].T
            ct_s[h, :, M_DV:] = jnp.broadcast_to(n0_ref[0, h:h + 1, :], (LANES, M_DK)).T
        m_s[...] = m0_ref[0]

    row = lax.broadcasted_iota(jnp.int32, (L, L), 0)
    col = lax.broadcasted_iota(jnp.int32, (L, L), 1)
    causal = col <= row
    tri = jnp.where(causal, 1.0, 0.0).astype(BF16)
    tri_t = jnp.where(row <= col, 1.0, 0.0).astype(BF16)
    gcol = gcol_ref[...]
    grow = grow_ref[0]
    bcol = sum(dot(tri, part) for part in _split3(gcol))
    brow = sum(dot(part, tri_t) for part in _split3(grow))

    ones = jnp.ones((L, LANES), BF16)

    for h in range(M_HEADS):
        q = q_ref[:, h * M_DK:(h + 1) * M_DK]
        k = k_ref[:, h * M_DK:(h + 1) * M_DK]
        v = v_ref[:, h * M_DV:(h + 1) * M_DV]
        ig_row = grow[h:h + 1, :]
        b_row = brow[M_HEADS + h:M_HEADS + h + 1, :]
        ig = jnp.broadcast_to(gcol[:, h:h + 1], (L, LANES))
        b = jnp.broadcast_to(bcol[:, M_HEADS + h:M_HEADS + h + 1], (L, LANES))
        m0 = m_s[h:h + 1, :]
        ct = ct_s[h]

        d = jnp.where(causal, wide(b, L) - b_row + ig_row, -jnp.inf)
        g = b + m0
        m = jnp.maximum(g, jnp.max(d, axis=-1, keepdims=True))
        w_inter = jnp.exp(g - m)
        qk = lax.dot_general(q, k, (((1,), (1,)), ((), ())), preferred_element_type=F32)
        s = (jnp.exp(d - wide(m, L)) * qk).astype(BF16)
        inter = dot(q, ct.astype(BF16))
        num = wide(w_inter, M_DV) * inter[:, :M_DV] + dot(s, v)
        den = w_inter * inter[:, M_DV:] + dot(s, ones)
        inv = 1.0 / jnp.maximum(jnp.abs(den), jnp.exp(-m))
        hh = num * wide(inv, M_DV)

        m_last = m[L - 1:L, :]
        w_state = jnp.exp(g[L - 1:L, :] - m_last)
        w_rows = jnp.exp(b[L - 1:L, :] - b + ig - m_last)
        vw = jnp.concatenate([(v.astype(F32) * wide(w_rows, M_DV)).astype(BF16), w_rows.astype(BF16)], axis=1)
        upd = lax.dot_general(k, vw, (((0,), (0,)), ((), ())), preferred_element_type=F32)
        ct_s[h] = wide(w_state, M_DV + LANES) * ct + upd
        m_s[h:h + 1, :] = m_last

        hn = _rms(hh, ghead_ref[:, h * M_DV:(h + 1) * M_DV])
        gate = jax.nn.sigmoid(o_ref[:, h * M_DV:(h + 1) * M_DV])
        hcat_s[:, h * M_DV:(h + 1) * M_DV] = (hn * gate).astype(BF16)

    y = dot(hcat_s[...], wout_ref[...])
    x1_ref[...] = x_ref[...] + _rms(y, gpost_ref[...])

    @pl.when(c == n_chunks - 1)
    def _():
        for h in range(M_HEADS):
            t = ct_s[h].T
            cout_ref[0, h] = t[:M_DV]
            nout_ref[0, h:h + 1, :] = t[M_DV:M_DV + 1]
        mout_ref[0] = m_s[...]


def _mlstm(q, k, v, o, gcol, grow, x2d, c0, n0, m0b, w_out, g_head, g_post, *, batch, chunk):
    rows = x2d.shape[0]
    n_chunks = rows // batch // chunk
    grid = (batch, n_chunks)
    row_spec = lambda cols: pl.BlockSpec((chunk, cols), lambda b, c: (b * n_chunks + c, 0))
    state4 = pl.BlockSpec((1, M_HEADS, M_DV, M_DK), lambda b, c: (b, 0, 0, 0))
    state3 = pl.BlockSpec((1, M_HEADS, LANES), lambda b, c: (b, 0, 0))
    return pl.pallas_call(
        functools.partial(_mlstm_kernel, chunk=chunk, n_chunks=n_chunks),
        grid=grid,
        in_specs=[row_spec(M_QK), row_spec(M_QK), row_spec(M_V), row_spec(D_MODEL), row_spec(LANES),
                  pl.BlockSpec((1, GATE_ROWS, chunk), lambda b, c: (b * n_chunks + c, 0, 0)),
                  row_spec(D_MODEL), state4, state3, state3,
                  _const_spec((M_V, D_MODEL)), _const_spec((1, M_V)), _const_spec((1, D_MODEL))],
        out_specs=[row_spec(D_MODEL), state4, state3, state3],
        out_shape=[jax.ShapeDtypeStruct((rows, D_MODEL), F32),
                   jax.ShapeDtypeStruct((batch, M_HEADS, M_DV, M_DK), F32),
                   jax.ShapeDtypeStruct((batch, M_HEADS, LANES), F32),
                   jax.ShapeDtypeStruct((batch, M_HEADS, LANES), F32)],
        scratch_shapes=[pltpu.VMEM((M_HEADS, M_DK, M_DV + LANES), F32), pltpu.VMEM((M_HEADS, LANES), F32),
                        pltpu.VMEM((chunk, M_V), BF16)],
        compiler_params=_params(2),
        name="mlstm_scan",
    )(q, k, v, o, gcol, grow, x2d, c0, n0, m0b, w_out, g_head, g_post)


def _ffn_body(x, gin_ref, win_ref, wout_ref, gout_ref):
    dot = functools.partial(jnp.dot, preferred_element_type=F32)
    xn = _rms(x, gin_ref[...]).astype(BF16)
    acc = None
    start = 0
    for width in FFN_COL_CHUNKS:
        gate = dot(xn, win_ref[:, start:start + width])
        up = dot(xn, win_ref[:, FFN_HIDDEN + start:FFN_HIDDEN + start + width])
        act = (jax.nn.silu(gate) * up).astype(BF16)
        part = dot(act, wout_ref[start:start + width, :])
        acc = part if acc is None else acc + part
        start += width
    return x + _rms(acc, gout_ref[...])


def _ffn_kernel(x_ref, gin_ref, win_ref, wout_ref, gout_ref, y_ref):
    y_ref[...] = _ffn_body(x_ref[...], gin_ref, win_ref, wout_ref, gout_ref)


def _attn_out_ffn_kernel(x_ref, a_ref, wo_ref, gpre_ref, gin_ref, win_ref, wout_ref, gout_ref, y_ref):
    mix = jnp.dot(a_ref[...], wo_ref[...], preferred_element_type=F32)
    x = x_ref[...] + _rms(mix, gpre_ref[...])
    y_ref[...] = _ffn_body(x, gin_ref, win_ref, wout_ref, gout_ref)


def _ffn(x2d, g_in, w_in, w_out, g_out, *, tile, attn=None, w_o=None, g_pre=None):
    rows = x2d.shape[0]
    row_spec = lambda cols: pl.BlockSpec((tile, cols), lambda i: (i, 0))
    vec = _const_spec((1, D_MODEL))
    ffn_specs = [vec, _const_spec(w_in.shape), _const_spec(w_out.shape), vec]
    if attn is None:
        kern, in_specs = _ffn_kernel, [row_spec(D_MODEL)] + ffn_specs
        args = (x2d, g_in, w_in, w_out, g_out)
    else:
        kern = _attn_out_ffn_kernel
        in_specs = [row_spec(D_MODEL), row_spec(DA_Q), _const_spec(w_o.shape), vec] + ffn_specs
        args = (x2d, attn, w_o, g_pre, g_in, w_in, w_out, g_out)
    return pl.pallas_call(
        kern, grid=(rows // tile,), in_specs=in_specs, out_specs=row_spec(D_MODEL),
        out_shape=jax.ShapeDtypeStruct((rows, D_MODEL), F32),
        compiler_params=_params(1), name="ffn",
    )(*args)


def _qkv_kernel(x_ref, gq_ref, gkv_ref, wq_ref, wkv_ref, q_ref, kf_ref, vf_ref, kb_ref, vb_ref, *, k_transposed):
    x = x_ref[...]
    xhat = x * lax.rsqrt(jnp.mean(x * x, axis=-1, keepdims=True) + EPS)
    xq = (xhat * gq_ref[...]).astype(BF16)
    xkv = (xhat * gkv_ref[...]).astype(BF16)
    dot = functools.partial(jnp.dot, preferred_element_type=F32)
    q_ref[...] = (dot(xq, wq_ref[...]) * (DA_DH ** -0.5 * LOG2E)).astype(BF16)
    kk = dot(xkv, wkv_ref[:, :DA_Q])
    if k_transposed:
        kt = kk.T
        kf_ref[0] = kt
        kb_ref[0] = kt.astype(BF16)
    else:
        kf_ref[...] = kk
        kb_ref[...] = kk.astype(BF16)
    vv = dot(xkv, wkv_ref[:, DA_Q:])
    vf_ref[...] = vv
    vb_ref[...] = vv.astype(BF16)


def _qkv(x2d, g_q, g_kv, w_q, w_kv, *, tile, batch, k_transposed):
    rows = x2d.shape[0]
    seq = rows // batch
    row_spec = pl.BlockSpec((tile, D_MODEL), lambda i: (i, 0))
    vec = _const_spec((1, D_MODEL))
    shp = lambda dt: jax.ShapeDtypeStruct((rows, DA_Q), dt)
    if k_transposed:
        tiles_per_seq = seq // tile
        k_spec = pl.BlockSpec((1, DA_Q, tile), lambda i: (i // tiles_per_seq, 0, i % tiles_per_seq))
        k_shp = lambda dt: jax.ShapeDtypeStruct((batch, DA_Q, seq), dt)
    else:
        k_spec, k_shp = row_spec, shp
    return pl.pallas_call(
        functools.partial(_qkv_kernel, k_transposed=k_transposed), grid=(rows // tile,),
        in_specs=[row_spec, vec, vec, _const_spec(w_q.shape), _const_spec(w_kv.shape)],
        out_specs=[row_spec, k_spec, row_spec, k_spec, row_spec],
        out_shape=[shp(BF16), k_shp(F32), shp(F32), k_shp(BF16), shp(BF16)],
        compiler_params=_params(1), name="attn_qkv",
    )(x2d, g_q, g_kv, w_q, w_kv)


def _rel_bias_tile(rb_ref, head, q_pos, k_pos):
    rel = k_pos - q_pos
    n = jnp.abs(rel)
    half = N_BUCKETS // 2

    def table(offset):
        val = jnp.full(rel.shape, rb_ref[offset + half - 1, head], F32)
        for bucket in reversed(range(half - 1)):
            val = jnp.where(n < _BUCKET_BOUNDS[bucket], rb_ref[offset + bucket, head], val)
        return val

    bias = jnp.where(rel > 0, table(half), table(0)) * LOG2E
    shift = ATTN_CHUNK.bit_length() - 1
    visible = jnp.right_shift(k_pos, shift) <= jnp.right_shift(q_pos, shift)
    return jnp.where(visible, bias, -jnp.inf)


def _lambda(l1q_ref, l1k_ref, l2q_ref, l2k_ref, lam_init):
    a = jnp.exp(jnp.sum(l1q_ref[...] * l1k_ref[...], axis=-1, keepdims=True))
    b = jnp.exp(jnp.sum(l2q_ref[...] * l2k_ref[...], axis=-1, keepdims=True))
    return a - b + lam_init


def _stack_maps(q):
    lane = lax.broadcasted_iota(jnp.int32, q.shape, 1)
    zero = jnp.zeros_like(q)
    return jnp.concatenate([jnp.where(lane < DA_DH, q, zero), jnp.where(lane >= DA_DH, q, zero)], axis=0)


def _diff_finish(acc, l, lam, g, lam_init):
    t = acc.shape[0] // 2
    o = acc[:t] / l[:t] - lam * (acc[t:] / l[t:])
    return _rms(o, g) * (1.0 - lam_init)


def _largest_divisor(n, candidates):
    return max(u for u in candidates if n % u == 0)


def _prompt_attn_kernel(rb_ref, q_ref, k_ref, v_ref, l1q_ref, l1k_ref, l2q_ref, l2k_ref, g_ref,
                        o_ref, bias_s, qm_s, m_s, acc_s, *, tile, n_tiles, lam_init):
    T = tile
    hd = 2 * DA_DH
    h = pl.program_id(0)
    b = pl.program_id(1)
    dot = functools.partial(jnp.dot, preferred_element_type=F32)

    @pl.when(b == 0)
    def _():
        qp = lax.broadcasted_iota(jnp.int32, (T, 2 * T), 0) + T
        kp = lax.broadcasted_iota(jnp.int32, (T, 2 * T), 1)
        near = _rel_bias_tile(rb_ref, h, qp, kp)
        bias_s[...] = jnp.concatenate([near, near], axis=0)

    q = q_ref[...]
    lane = lax.broadcasted_iota(jnp.int32, q.shape, 1)
    zero = jnp.zeros_like(q)
    qm_s[0] = jnp.where(lane < DA_DH, q, zero)
    qm_s[1] = jnp.where(lane >= DA_DH, q, zero)
    m_s[...] = jnp.full(m_s.shape, -jnp.inf, F32)
    acc_s[...] = jnp.zeros(acc_s.shape, F32)

    def update(i, key_block, n_blocks, bias):
        q_rows = pl.ds(pl.multiple_of(i * T, T), T)
        k_pos = pl.ds(pl.multiple_of(key_block * T, T), n_blocks * T)
        kb = k_ref[0, :, k_pos]
        vb = v_ref[k_pos, :]
        vext = jnp.concatenate([vb, jnp.ones_like(vb)], axis=1)
        s = jnp.concatenate([dot(qm_s[0, q_rows, :], kb), dot(qm_s[1, q_rows, :], kb)], axis=0)
        if bias is not None:
            s = s + bias
        m_prev = m_s[i]
        m_new = jnp.maximum(m_prev, jnp.max(s, axis=-1, keepdims=True))
        alpha = jnp.exp2(m_prev - m_new)
        p = jnp.exp2(s - jnp.concatenate([m_new] * (n_blocks * T // LANES), axis=1))
        pv = jnp.dot(p.astype(BF16), vext, preferred_element_type=F32)
        acc_s[i] = jnp.concatenate([alpha, alpha], axis=1) * acc_s[i] + pv
        m_s[i] = m_new

    update(0, 0, 1, bias_s[:, T:])
    near_unroll = _largest_divisor(n_tiles - 1, (3, 2, 1))

    def near_body(it, carry):
        for u in range(near_unroll):
            i = 1 + it * near_unroll + u
            update(i, i - 1, 2, bias_s[...])
        return carry

    lax.fori_loop(0, (n_tiles - 1) // near_unroll, near_body, 0)

    m_s[...] = m_s[...] - rb_ref[N_BUCKETS // 2 - 1, h] * LOG2E
    for i in range(2, n_tiles, 2):
        update(i, i - 2, 1, None)
    n_double = sum((i - 1) // 2 for i in range(1, n_tiles))
    far_unroll = _largest_divisor(n_double, (7, 5, 4, 3, 2, 1))

    def far_body(_, carry):
        i, t = carry
        for _u in range(far_unroll):
            update(i, 2 * t, 2, None)
            wrap = t + 1 >= jnp.right_shift(i - 1, 1)
            i, t = jnp.where(wrap, i + 1, i), jnp.where(wrap, 0, t + 1)
        return i, t

    lax.fori_loop(0, n_double // far_unroll, far_body, (jnp.int32(3), jnp.int32(0)))

    lam = _lambda(l1q_ref, l1k_ref, l2q_ref, l2k_ref, lam_init)
    for i in range(n_tiles):
        acc = acc_s[i]
        o_ref[i * T:(i + 1) * T, :] = _diff_finish(acc[:, :hd], acc[:, hd:], lam, g_ref[...], lam_init).astype(BF16)


def _prompt_attn(rel_bias, q, k, v, lams, g_head, *, batch, seq, tile, lam_init):
    assert tile >= MAX_DISTANCE and tile % ATTN_CHUNK == 0 and seq % tile == 0
    nq = seq // tile
    hd = 2 * DA_DH
    lam_spec = _const_spec((1, DA_DH))
    seq_spec = pl.BlockSpec((seq, hd), lambda h, b: (b, h))
    return pl.pallas_call(
        functools.partial(_prompt_attn_kernel, tile=tile, n_tiles=nq, lam_init=lam_init),
        grid=(DA_HEADS, batch),
        in_specs=[pl.BlockSpec(memory_space=pltpu.SMEM), seq_spec,
                  pl.BlockSpec((1, hd, seq), lambda h, b: (b, h, 0)), seq_spec,
                  lam_spec, lam_spec, lam_spec, lam_spec, _const_spec((1, hd))],
        out_specs=seq_spec,
        out_shape=jax.ShapeDtypeStruct((batch * seq, DA_Q), BF16),
        scratch_shapes=[pltpu.VMEM((2 * tile, 2 * tile), F32), pltpu.VMEM((2, seq, hd), BF16),
                        pltpu.VMEM((nq, 2 * tile, LANES), F32), pltpu.VMEM((nq, 2 * tile, 2 * hd), F32)],
        compiler_params=_params(2), name="diff_attn_prompt",
    )(rel_bias, q, k, v, *lams, g_head)


def _sample_attn_kernel(rb_ref, q_ref, kp_ref, vp_ref, kn_ref, vn_ref, l1q_ref, l1k_ref, l2q_ref, l2k_ref,
                        g_ref, o_ref, bias_p, bias_n, *, past, seq, lam_init):
    b = pl.program_id(0)
    hd = 2 * DA_DH
    dot = functools.partial(jnp.dot, preferred_element_type=F32)
    nt = lambda a, bb: lax.dot_general(a, bb, (((1,), (1,)), ((), ())), preferred_element_type=F32)

    @pl.when(b == 0)
    def _():
        qp = lax.broadcasted_iota(jnp.int32, (seq, past), 0) + past
        kp = lax.broadcasted_iota(jnp.int32, (seq, past), 1)
        qn = lax.broadcasted_iota(jnp.int32, (seq, seq), 0) + past
        kn = lax.broadcasted_iota(jnp.int32, (seq, seq), 1) + past
        for h in range(DA_HEADS):
            tile = _rel_bias_tile(rb_ref, h, qp, kp)
            bias_p[h] = jnp.concatenate([tile, tile], axis=0)
            tile = _rel_bias_tile(rb_ref, h, qn, kn)
            bias_n[h] = jnp.concatenate([tile, tile], axis=0)

    lam = _lambda(l1q_ref, l1k_ref, l2q_ref, l2k_ref, lam_init)
    v_heads = pltpu.einshape("khd->hkd", vp_ref[0])
    for h in range(DA_HEADS):
        cols = slice(h * hd, (h + 1) * hd)
        qs = _stack_maps(q_ref[:, cols])
        s_p = dot(qs, kp_ref[0, cols, :].astype(BF16)) + bias_p[h]
        s_n = nt(qs, kn_ref[:, cols]) + bias_n[h]
        m = jnp.maximum(jnp.max(s_p, axis=-1, keepdims=True), jnp.max(s_n, axis=-1, keepdims=True))
        p_p = jnp.exp2(s_p - m)
        p_n = jnp.exp2(s_n - m)
        l = jnp.sum(p_p, axis=-1, keepdims=True) + jnp.sum(p_n, axis=-1, keepdims=True)
        acc = (dot(p_p.astype(BF16), v_heads[h].astype(BF16))
               + dot(p_n.astype(BF16), vn_ref[:, cols]))
        o_ref[:, cols] = _diff_finish(acc, l, lam, g_ref[...], lam_init).astype(BF16)


def _sample_attn(rel_bias, q, cache_kt, cache_v, k_new, v_new, lams, g_head, *, batch, seq, lam_init):
    past = cache_kt.shape[2]
    hd = 2 * DA_DH
    lam_spec = _const_spec((1, DA_DH))
    new_spec = pl.BlockSpec((seq, DA_Q), lambda b: (b, 0))
    return pl.pallas_call(
        functools.partial(_sample_attn_kernel, past=past, seq=seq, lam_init=lam_init),
        grid=(batch,),
        in_specs=[pl.BlockSpec(memory_space=pltpu.SMEM), new_spec,
                  pl.BlockSpec((1, DA_Q, past), lambda b: (b, 0, 0)),
                  pl.BlockSpec((1, past, DA_HEADS, hd), lambda b: (b, 0, 0, 0)),
                  new_spec, new_spec, lam_spec, lam_spec, lam_spec, lam_spec, _const_spec((1, hd))],
        out_specs=new_spec,
        out_shape=jax.ShapeDtypeStruct((batch * seq, DA_Q), BF16),
        scratch_shapes=[pltpu.VMEM((DA_HEADS, 2 * seq, past), F32), pltpu.VMEM((DA_HEADS, 2 * seq, seq), F32)],
        compiler_params=_params(1), name="diff_attn_sample",
    )(rel_bias, q, cache_kt, cache_v, k_new, v_new, *lams, g_head)


def _prep_weights(norm_g, mlstm_w_in, mlstm_b_gate, mlstm_g_head, mlstm_w_out, kv_g, kv_w, diff_w_q,
                  diff_lam_q1, diff_lam_k1, diff_lam_q2, diff_lam_k2, diff_g_head, diff_w_o,
                  ffn_w_in, ffn_w_out):
    n_main = 2 * M_QK + M_V + D_MODEL
    w_in = mlstm_w_in[0]
    wg = jnp.pad(w_in[:, n_main:], ((0, 0), (0, LANES - 2 * M_HEADS)))
    wg_hi = wg.astype(BF16)
    wg_lo = (wg - wg_hi.astype(F32)).astype(BF16)
    row = lambda a: a.reshape(1, -1)
    return dict(
        norm=[[row(norm_g[l, j]) for j in range(4)] for l in range(2)],
        w_main=w_in[:, :n_main].astype(BF16),
        w_gate=jnp.concatenate([wg_hi, wg_lo], axis=1),
        b_gate=jnp.pad(mlstm_b_gate[0], (0, LANES - 2 * M_HEADS)).reshape(1, LANES),
        g_mhead=row(mlstm_g_head[0]),
        w_mout=mlstm_w_out[0].astype(BF16),
        kv_g=row(kv_g), kv_w=kv_w.astype(BF16), w_q=diff_w_q[0].astype(BF16),
        lams=[row(diff_lam_q1[0]), row(diff_lam_k1[0]), row(diff_lam_q2[0]), row(diff_lam_k2[0])],
        g_dhead=row(diff_g_head[0]), w_o=diff_w_o[0].astype(BF16),
        ffn_in=[ffn_w_in[l].astype(BF16) for l in range(2)],
        ffn_out=[ffn_w_out[l].astype(BF16) for l in range(2)],
    )


def _trunk(x, c0, n0, m0, past_k, past_v, rel_bias, w, *, row_tile, chunk, attn_tile):
    batch, seq, _ = x.shape
    rows = batch * seq
    x2d = x.reshape(rows, D_MODEL)
    lam_init = 0.8 - 0.6 * math.exp(-0.3 * 1)

    q, k, v, o, gcol, grow = _inproj(x2d, w["norm"][0][0], w["w_main"], w["w_gate"], w["b_gate"],
                                     tile=row_tile, chunk=chunk)
    m0b = jnp.broadcast_to(m0[..., None], (batch, M_HEADS, LANES))
    x1, c_new, n_new, m_new = _mlstm(q, k, v, o, gcol, grow, x2d, c0, n0, m0b, w["w_mout"], w["g_mhead"],
                                     w["norm"][0][1], batch=batch, chunk=chunk)
    x2 = _ffn(x1, w["norm"][0][2], w["ffn_in"][0], w["ffn_out"][0], w["norm"][0][3], tile=row_tile)

    aq, k_f32, v_f32, k_bf, v_bf = _qkv(x2, w["norm"][1][0], w["kv_g"], w["w_q"], w["kv_w"], tile=row_tile,
                                        batch=batch, k_transposed=past_k is None)
    if past_k is None:
        attn = _prompt_attn(rel_bias, aq, k_bf, v_bf, w["lams"], w["g_dhead"],
                            batch=batch, seq=seq, tile=attn_tile, lam_init=lam_init)
        k_f32 = k_f32.reshape(batch, DA_HEADS, 2, DA_DH, seq).transpose(0, 4, 1, 2, 3)
    else:
        past = past_k.shape[1]
        past_kt = past_k.transpose(0, 2, 3, 4, 1).reshape(batch, DA_Q, past)
        attn = _sample_attn(rel_bias, aq, past_kt, past_v, k_bf, v_bf, w["lams"], w["g_dhead"],
                            batch=batch, seq=seq, lam_init=lam_init)
    y = _ffn(x2, w["norm"][1][2], w["ffn_in"][1], w["ffn_out"][1], w["norm"][1][3], tile=row_tile,
             attn=attn, w_o=w["w_o"], g_pre=w["norm"][1][1])

    return (y.reshape(batch, seq, D_MODEL), c_new[None], n_new[None], m_new[None, :, :, 0],
            k_f32.reshape(batch, seq, DA_HEADS, 2, DA_DH), v_f32.reshape(batch, seq, DA_HEADS, 2 * DA_DH))


def kernel(x_prompt, x_sample, state_C, state_n, state_m, cache_k, cache_v, norm_g, mlstm_w_in, mlstm_b_gate, mlstm_g_head, mlstm_w_out, kv_g, kv_w, rel_bias, diff_w_q, diff_lam_q1, diff_lam_k1, diff_lam_q2, diff_lam_k2, diff_g_head, diff_w_o, ffn_w_in, ffn_w_out):
    w = _prep_weights(norm_g, mlstm_w_in, mlstm_b_gate, mlstm_g_head, mlstm_w_out, kv_g, kv_w, diff_w_q,
                      diff_lam_q1, diff_lam_k1, diff_lam_q2, diff_lam_k2, diff_g_head, diff_w_o,
                      ffn_w_in, ffn_w_out)
    pb = x_prompt.shape[0]
    zeros = lambda *s: jnp.zeros(s, F32)
    y_p, p_c, p_n, p_m, p_k, p_v = _trunk(
        x_prompt, zeros(pb, M_HEADS, M_DV, M_DK), zeros(pb, M_HEADS, M_DK), zeros(pb, M_HEADS),
        None, None, rel_bias, w, row_tile=512, chunk=256, attn_tile=256)
    sb, ss, _ = x_sample.shape
    y_s, s_c, s_n, s_m, s_k, s_v = _trunk(
        x_sample, state_C[0], state_n[0], state_m[0], cache_k, cache_v, rel_bias, w,
        row_tile=sb * ss, chunk=ss, attn_tile=None)
    return (y_p, y_s, p_c, p_n, p_m, p_k, p_v, s_c, s_n, s_m, s_k, s_v)
```

```python
import functools
import math

import jax
import jax.numpy as jnp
from jax import lax
from jax.experimental import pallas as pl
from jax.experimental.pallas import tpu as pltpu

F32 = jnp.float32
BF16 = jnp.bfloat16

D_MODEL = 1024
ATTN_CHUNK = 64
M_HEADS = 4
M_DK = D_MODEL // (2 * M_HEADS)
M_DV = D_MODEL // M_HEADS
M_QK = M_HEADS * M_DK
M_V = M_HEADS * M_DV
DA_HEADS = 8
DA_DH = D_MODEL // (2 * DA_HEADS)
DA_Q = DA_HEADS * 2 * DA_DH
FFN_HIDDEN = -(-8 * D_MODEL // (3 * 256)) * 256
N_BUCKETS = 32
MAX_DISTANCE = 128
EPS = 1e-6
LOG2E = math.log2(math.e)

LANES = 128
GATE_ROWS = 16
VMEM_LIMIT = 56 * 1024 * 1024
FAR_GROUP = 4
MLSTM_STREAMS = 2
FFN_COL_CHUNKS = (1024, 1024, 768)
assert sum(FFN_COL_CHUNKS) == FFN_HIDDEN


def _bucket_upper_bounds():
    half = N_BUCKETS // 2
    max_exact = half // 2
    ratio = MAX_DISTANCE // max_exact
    steps = half - max_exact
    bounds = [n + 1 for n in range(max_exact)]
    for k in range(1, steps):
        n = max_exact
        while n ** steps < (max_exact ** steps) * (ratio ** k):
            n += 1
        bounds.append(n)
    return bounds


_BUCKET_BOUNDS = _bucket_upper_bounds()


def _rms(x, g):
    return x * lax.rsqrt(jnp.mean(x * x, axis=-1, keepdims=True) + EPS) * g


def _log_sigmoid(x):
    return jnp.minimum(x, 0.0) - jnp.log1p(jnp.exp(-jnp.abs(x)))


def _split3(a):
    a1 = a.astype(BF16)
    r1 = a - a1.astype(F32)
    a2 = r1.astype(BF16)
    a3 = (r1 - a2.astype(F32)).astype(BF16)
    return a1, a2, a3


def _const_spec(shape):
    return pl.BlockSpec(shape, lambda *_: (0,) * len(shape), pipeline_mode=pl.Buffered(1))


def _params(n_axes):
    return pltpu.CompilerParams(dimension_semantics=("arbitrary",) * n_axes,
                                vmem_limit_bytes=VMEM_LIMIT)


def _inproj_kernel(x_ref, g_ref, w_ref, wg_ref, bg_ref,
                   q_ref, k_ref, v_ref, o_ref, gcol_ref, grow_ref, *, chunk):
    xn = _rms(x_ref[...], g_ref[...])
    xh = xn.astype(BF16)
    xl = (xn - xh.astype(F32)).astype(BF16)
    dot = functools.partial(jnp.dot, preferred_element_type=F32)
    q_ref[...] = dot(xh, w_ref[:, :M_QK]).astype(BF16)
    k_ref[...] = (dot(xh, w_ref[:, M_QK:2 * M_QK]) * (M_DK ** -0.5)).astype(BF16)
    v_ref[...] = dot(xh, w_ref[:, 2 * M_QK:2 * M_QK + M_V]).astype(BF16)
    o_ref[...] = dot(xh, w_ref[:, 2 * M_QK + M_V:2 * M_QK + M_V + D_MODEL])
    gh = dot(xh, wg_ref[...])
    gl = dot(xl, wg_ref[:, :LANES])
    gates = gh[:, :LANES] + gh[:, LANES:] + gl + bg_ref[...]
    lane = lax.broadcasted_iota(jnp.int32, gates.shape, 1)
    gv = jnp.where(lane < M_HEADS, gates, _log_sigmoid(gates))
    gcol_ref[...] = gv
    gt = gv.T
    for j in range(grow_ref.shape[0]):
        grow_ref[j] = gt[:GATE_ROWS, j * chunk:(j + 1) * chunk]


def _inproj(x2d, g, w_main, w_gate, b_gate, *, tile, chunk):
    rows = x2d.shape[0]
    grid = (rows // tile,)
    row_spec = lambda cols: pl.BlockSpec((tile, cols), lambda i: (i, 0))
    return pl.pallas_call(
        functools.partial(_inproj_kernel, chunk=chunk),
        grid=grid,
        in_specs=[row_spec(D_MODEL), _const_spec((1, D_MODEL)),
                  _const_spec(w_main.shape), _const_spec(w_gate.shape), _const_spec((1, LANES))],
        out_specs=[row_spec(M_QK), row_spec(M_QK), row_spec(M_V), row_spec(D_MODEL), row_spec(LANES),
                   pl.BlockSpec((tile // chunk, GATE_ROWS, chunk), lambda i: (i, 0, 0))],
        out_shape=[jax.ShapeDtypeStruct((rows, M_QK), BF16), jax.ShapeDtypeStruct((rows, M_QK), BF16),
                   jax.ShapeDtypeStruct((rows, M_V), BF16), jax.ShapeDtypeStruct((rows, D_MODEL), F32),
                   jax.ShapeDtypeStruct((rows, LANES), F32),
                   jax.ShapeDtypeStruct((rows // chunk, GATE_ROWS, chunk), F32)],
        compiler_params=_params(1),
        name="mlstm_inproj",
    )(x2d, g, w_main, w_gate, b_gate)


def _mlstm_kernel(q_ref, k_ref, v_ref, o_ref, gcol_ref, grow_ref, x_ref, c0_ref, n0_ref, m0_ref,
                  wout_ref, ghead_ref, gpost_ref,
                  x1_ref, cout_ref, nout_ref, mout_ref,
                  ct_s, m_s, hcat_s, *, chunk, n_chunks, n_streams):
    L = chunk
    c = pl.program_id(1)
    dot = functools.partial(jnp.dot, preferred_element_type=F32)

    def wide(rep, n):
        return rep[:, :n] if n < LANES else jnp.concatenate([rep] * (n // LANES), axis=1)

    @pl.when(c == 0)
    def _():
        for st in range(n_streams):
            for h in range(M_HEADS):
                ct_s[st * M_HEADS + h, :, :M_DV] = c0_ref[st, h].T
                ct_s[st * M_HEADS + h, :, M_DV:] = jnp.broadcast_to(n0_ref[st, h:h + 1, :], (LANES, M_DK)).T
            m_s[st] = m0_ref[st]

    row = lax.broadcasted_iota(jnp.int32, (L, L), 0)
    col = lax.broadcasted_iota(jnp.int32, (L, L), 1)
    causal = col <= row
    tri = jnp.where(causal, 1.0, 0.0).astype(BF16)
    tri_t = jnp.where(row <= col, 1.0, 0.0).astype(BF16)
    ones = jnp.ones((L, LANES), BF16)

    for st in range(n_streams):
        gcol = gcol_ref[st]
        grow = grow_ref[st, 0]
        bcol = sum(dot(tri, part) for part in _split3(gcol))
        brow = sum(dot(part, tri_t) for part in _split3(grow))

        for h in range(M_HEADS):
            q = q_ref[st, :, h * M_DK:(h + 1) * M_DK]
            k = k_ref[st, :, h * M_DK:(h + 1) * M_DK]
            v = v_ref[st, :, h * M_DV:(h + 1) * M_DV]
            ig_row = grow[h:h + 1, :]
            b_row = brow[M_HEADS + h:M_HEADS + h + 1, :]
            ig = jnp.broadcast_to(gcol[:, h:h + 1], (L, LANES))
            b = jnp.broadcast_to(bcol[:, M_HEADS + h:M_HEADS + h + 1], (L, LANES))
            m0 = m_s[st, h:h + 1, :]
            ct = ct_s[st * M_HEADS + h]

            d = jnp.where(causal, wide(b, L) - b_row + ig_row, -jnp.inf)
            g = b + m0
            m = jnp.maximum(g, jnp.max(d, axis=-1, keepdims=True))
            w_inter = jnp.exp(g - m)
            qk = lax.dot_general(q, k, (((1,), (1,)), ((), ())), preferred_element_type=F32)
            s = (jnp.exp(d - wide(m, L)) * qk).astype(BF16)
            inter = dot(q, ct.astype(BF16))
            num = wide(w_inter, M_DV) * inter[:, :M_DV] + dot(s, v)
            den = w_inter * inter[:, M_DV:] + dot(s, ones)
            inv = 1.0 / jnp.maximum(jnp.abs(den), jnp.exp(-m))
            hh = num * wide(inv, M_DV)

            m_last = m[L - 1:L, :]
            w_state = jnp.exp(g[L - 1:L, :] - m_last)
            w_rows = jnp.exp(b[L - 1:L, :] - b + ig - m_last)
            vw = jnp.concatenate([(v.astype(F32) * wide(w_rows, M_DV)).astype(BF16), w_rows.astype(BF16)], axis=1)
            upd = lax.dot_general(k, vw, (((0,), (0,)), ((), ())), preferred_element_type=F32)
            ct_s[st * M_HEADS + h] = wide(w_state, M_DV + LANES) * ct + upd
            m_s[st, h:h + 1, :] = m_last

            hn = _rms(hh, ghead_ref[:, h * M_DV:(h + 1) * M_DV])
            gate = jax.nn.sigmoid(o_ref[st, :, h * M_DV:(h + 1) * M_DV])
            hcat_s[st * L:(st + 1) * L, h * M_DV:(h + 1) * M_DV] = (hn * gate).astype(BF16)

    y = dot(hcat_s[...], wout_ref[...])
    for st in range(n_streams):
        x1_ref[st] = x_ref[st] + _rms(y[st * L:(st + 1) * L], gpost_ref[...])

    @pl.when(c == n_chunks - 1)
    def _():
        for st in range(n_streams):
            for h in range(M_HEADS):
                t = ct_s[st * M_HEADS + h].T
                cout_ref[st, h] = t[:M_DV]
                nout_ref[st, h:h + 1, :] = t[M_DV:M_DV + 1]
            mout_ref[st] = m_s[st]


def _mlstm(q, k, v, o, gcol, grow, x2d, c0, n0, m0b, w_out, g_head, g_post, *, batch, chunk):
    rows = x2d.shape[0]
    seq = rows // batch
    n_chunks = seq // chunk
    ns = MLSTM_STREAMS
    assert batch % ns == 0
    per_stream = lambda a: a.reshape(batch, seq, a.shape[-1])
    row_spec = lambda cols: pl.BlockSpec((ns, chunk, cols), lambda b, c: (b, c, 0))
    state4 = pl.BlockSpec((ns, M_HEADS, M_DV, M_DK), lambda b, c: (b, 0, 0, 0))
    state3 = pl.BlockSpec((ns, M_HEADS, LANES), lambda b, c: (b, 0, 0))
    x1, c_new, n_new, m_new = pl.pallas_call(
        functools.partial(_mlstm_kernel, chunk=chunk, n_chunks=n_chunks, n_streams=ns),
        grid=(batch // ns, n_chunks),
        in_specs=[row_spec(M_QK), row_spec(M_QK), row_spec(M_V), row_spec(D_MODEL), row_spec(LANES),
                  pl.BlockSpec((ns, 1, GATE_ROWS, chunk), lambda b, c: (b, c, 0, 0)),
                  row_spec(D_MODEL), state4, state3, state3,
                  _const_spec((M_V, D_MODEL)), _const_spec((1, M_V)), _const_spec((1, D_MODEL))],
        out_specs=[row_spec(D_MODEL), state4, state3, state3],
        out_shape=[jax.ShapeDtypeStruct((batch, seq, D_MODEL), F32),
                   jax.ShapeDtypeStruct((batch, M_HEADS, M_DV, M_DK), F32),
                   jax.ShapeDtypeStruct((batch, M_HEADS, LANES), F32),
                   jax.ShapeDtypeStruct((batch, M_HEADS, LANES), F32)],
        scratch_shapes=[pltpu.VMEM((ns * M_HEADS, M_DK, M_DV + LANES), F32), pltpu.VMEM((ns, M_HEADS, LANES), F32),
                        pltpu.VMEM((ns * chunk, M_V), BF16)],
        compiler_params=_params(2),
        name="mlstm_scan",
    )(per_stream(q), per_stream(k), per_stream(v), per_stream(o), per_stream(gcol),
      grow.reshape(batch, n_chunks, GATE_ROWS, chunk), per_stream(x2d), c0, n0, m0b, w_out, g_head, g_post)
    return x1.reshape(rows, D_MODEL), c_new, n_new, m_new


def _ffn_body(x, gin_ref, win_ref, wout_ref, gout_ref):
    dot = functools.partial(jnp.dot, preferred_element_type=F32)
    xn = _rms(x, gin_ref[...]).astype(BF16)
    acc = None
    start = 0
    for width in FFN_COL_CHUNKS:
        gate = dot(xn, win_ref[:, start:start + width])
        up = dot(xn, win_ref[:, FFN_HIDDEN + start:FFN_HIDDEN + start + width])
        act = (jax.nn.silu(gate) * up).astype(BF16)
        part = dot(act, wout_ref[start:start + width, :])
        acc = part if acc is None else acc + part
        start += width
    return x + _rms(acc, gout_ref[...])


def _ffn_kernel(x_ref, gin_ref, win_ref, wout_ref, gout_ref, y_ref):
    y_ref[...] = _ffn_body(x_ref[...], gin_ref, win_ref, wout_ref, gout_ref)


def _attn_out_ffn_kernel(x_ref, a_ref, wo_ref, gpre_ref, gin_ref, win_ref, wout_ref, gout_ref, y_ref):
    mix = jnp.dot(a_ref[...], wo_ref[...], preferred_element_type=F32)
    x = x_ref[...] + _rms(mix, gpre_ref[...])
    y_ref[...] = _ffn_body(x, gin_ref, win_ref, wout_ref, gout_ref)


def _ffn(x2d, g_in, w_in, w_out, g_out, *, layer, tile, attn=None, w_o=None, g_pre=None):
    rows = x2d.shape[0]
    row_spec = lambda cols: pl.BlockSpec((tile, cols), lambda i: (i, 0))
    vec = _const_spec((1, D_MODEL))
    layer_spec = lambda w: pl.BlockSpec((None,) + w.shape[1:], lambda i: (layer, 0, 0),
                                        pipeline_mode=pl.Buffered(1))
    ffn_specs = [vec, layer_spec(w_in), layer_spec(w_out), vec]
    if attn is None:
        kern, in_specs = _ffn_kernel, [row_spec(D_MODEL)] + ffn_specs
        args = (x2d, g_in, w_in, w_out, g_out)
    else:
        kern = _attn_out_ffn_kernel
        in_specs = [row_spec(D_MODEL), row_spec(DA_Q), _const_spec(w_o.shape), vec] + ffn_specs
        args = (x2d, attn, w_o, g_pre, g_in, w_in, w_out, g_out)
    return pl.pallas_call(
        kern, grid=(rows // tile,), in_specs=in_specs, out_specs=row_spec(D_MODEL),
        out_shape=jax.ShapeDtypeStruct((rows, D_MODEL), F32),
        compiler_params=_params(1), name="ffn",
    )(*args)


def _qkv_kernel(x_ref, gq_ref, gkv_ref, wq_ref, wkv_ref, q_ref, kf_ref, vf_ref, kb_ref, vb_ref, *, k_transposed):
    x = x_ref[...]
    xhat = x * lax.rsqrt(jnp.mean(x * x, axis=-1, keepdims=True) + EPS)
    xq = (xhat * gq_ref[...]).astype(BF16)
    xkv = (xhat * gkv_ref[...]).astype(BF16)
    dot = functools.partial(jnp.dot, preferred_element_type=F32)
    q_ref[...] = (dot(xq, wq_ref[...]) * (DA_DH ** -0.5 * LOG2E)).astype(BF16)
    kk = dot(xkv, wkv_ref[:, :DA_Q])
    if k_transposed:
        kt = kk.T
        kf_ref[0] = kt
        kb_ref[0] = kt.astype(BF16)
    else:
        kf_ref[...] = kk
        kb_ref[...] = kk.astype(BF16)
    vv = dot(xkv, wkv_ref[:, DA_Q:])
    vf_ref[...] = vv
    vb_ref[...] = vv.astype(BF16)


def _qkv(x2d, g_q, g_kv, w_q, w_kv, *, tile, batch, k_transposed):
    rows = x2d.shape[0]
    seq = rows // batch
    row_spec = pl.BlockSpec((tile, D_MODEL), lambda i: (i, 0))
    vec = _const_spec((1, D_MODEL))
    shp = lambda dt: jax.ShapeDtypeStruct((rows, DA_Q), dt)
    if k_transposed:
        tiles_per_seq = seq // tile
        k_spec = pl.BlockSpec((1, DA_Q, tile), lambda i: (i // tiles_per_seq, 0, i % tiles_per_seq))
        k_shp = lambda dt: jax.ShapeDtypeStruct((batch, DA_Q, seq), dt)
    else:
        k_spec, k_shp = row_spec, shp
    return pl.pallas_call(
        functools.partial(_qkv_kernel, k_transposed=k_transposed), grid=(rows // tile,),
        in_specs=[row_spec, vec, vec, _const_spec(w_q.shape), _const_spec(w_kv.shape)],
        out_specs=[row_spec, k_spec, row_spec, k_spec, row_spec],
        out_shape=[shp(BF16), k_shp(F32), shp(F32), k_shp(BF16), shp(BF16)],
        compiler_params=_params(1), name="attn_qkv",
    )(x2d, g_q, g_kv, w_q, w_kv)


def _rel_bias_tile(rb_ref, head, q_pos, k_pos):
    rel = k_pos - q_pos
    n = jnp.abs(rel)
    half = N_BUCKETS // 2

    def table(offset):
        val = jnp.full(rel.shape, rb_ref[offset + half - 1, head], F32)
        for bucket in reversed(range(half - 1)):
            val = jnp.where(n < _BUCKET_BOUNDS[bucket], rb_ref[offset + bucket, head], val)
        return val

    bias = jnp.where(rel > 0, table(half), table(0)) * LOG2E
    shift = ATTN_CHUNK.bit_length() - 1
    visible = jnp.right_shift(k_pos, shift) <= jnp.right_shift(q_pos, shift)
    return jnp.where(visible, bias, -jnp.inf)


def _lambda(l1q_ref, l1k_ref, l2q_ref, l2k_ref, lam_init):
    a = jnp.exp(jnp.sum(l1q_ref[...] * l1k_ref[...], axis=-1, keepdims=True))
    b = jnp.exp(jnp.sum(l2q_ref[...] * l2k_ref[...], axis=-1, keepdims=True))
    return a - b + lam_init


def _stack_maps(q):
    lane = lax.broadcasted_iota(jnp.int32, q.shape, 1)
    zero = jnp.zeros_like(q)
    return jnp.concatenate([jnp.where(lane < DA_DH, q, zero), jnp.where(lane >= DA_DH, q, zero)], axis=0)


def _diff_finish(acc, l, lam, g, lam_init):
    t = acc.shape[0] // 2
    o = acc[:t] / l[:t] - lam * (acc[t:] / l[t:])
    return _rms(o, g) * (1.0 - lam_init)


def _largest_divisor(n, candidates):
    return max(u for u in candidates if n % u == 0)


def _prompt_attn_kernel(rb_ref, q_ref, k_ref, v_ref, l1q_ref, l1k_ref, l2q_ref, l2k_ref, g_ref,
                        o_ref, bias_s, qm_s, m_s, acc_s, snext_s, *, tile, n_tiles, lam_init):
    T = tile
    hd = 2 * DA_DH
    h = pl.program_id(0)
    b = pl.program_id(1)
    dot = functools.partial(jnp.dot, preferred_element_type=F32)

    @pl.when(b == 0)
    def _():
        qp = lax.broadcasted_iota(jnp.int32, (T, 2 * T), 0) + T
        kp = lax.broadcasted_iota(jnp.int32, (T, 2 * T), 1)
        near = _rel_bias_tile(rb_ref, h, qp, kp)
        bias_s[...] = jnp.concatenate([near, near], axis=0)

    q = q_ref[...]
    lane = lax.broadcasted_iota(jnp.int32, q.shape, 1)
    zero = jnp.zeros_like(q)
    qm_s[0] = jnp.where(lane < DA_DH, q, zero)
    qm_s[1] = jnp.where(lane >= DA_DH, q, zero)
    far_bias = rb_ref[N_BUCKETS // 2 - 1, h] * LOG2E

    def update(i, key_block, n_blocks, bias):
        apply(i, scores(i, key_block, n_blocks), key_block, n_blocks, bias)

    def scores(i, key_block, n_blocks):
        q_rows = pl.ds(pl.multiple_of(i * T, T), T)
        kb = k_ref[0, :, pl.ds(pl.multiple_of(key_block * T, T), n_blocks * T)]
        return jnp.concatenate([dot(qm_s[0, q_rows, :], kb), dot(qm_s[1, q_rows, :], kb)], axis=0)

    def apply(i, s, key_block, n_blocks, bias):
        vb = v_ref[pl.ds(pl.multiple_of(key_block * T, T), n_blocks * T), :]
        vext = jnp.concatenate([vb, jnp.ones_like(vb)], axis=1)
        first = bias is not None
        if first:
            s = s + bias
            m_new = jnp.broadcast_to(jnp.max(s, axis=-1, keepdims=True), (2 * T, LANES))
        else:
            m_prev = m_s[i]
            m_new = jnp.maximum(m_prev, jnp.max(s, axis=-1, keepdims=True))
        p = jnp.exp2(s - jnp.concatenate([m_new] * (n_blocks * T // LANES), axis=1))
        pv = jnp.dot(p.astype(BF16), vext, preferred_element_type=F32)
        if first:
            acc_s[i] = pv
            m_s[i] = m_new - far_bias
        else:
            alpha = jnp.exp2(m_prev - m_new)
            acc_s[i] = jnp.concatenate([alpha, alpha], axis=1) * acc_s[i] + pv
            m_s[i] = m_new

    update(0, 0, 1, bias_s[:, T:])
    near_unroll = _largest_divisor(n_tiles - 1, (5, 3, 2, 1))

    last_tile = n_tiles - 1
    if n_tiles > 1:
        snext_s[:, :2 * T] = scores(1, 0, 2)

    def near_body(it, carry):
        s = snext_s[:, :2 * T]
        for u in range(near_unroll):
            i = 1 + it * near_unroll + u
            i_next = jnp.minimum(i + 1, last_tile)
            s_next = scores(i_next, i_next - 1, 2)
            apply(i, s, i - 1, 2, bias_s[...])
            s = s_next
        snext_s[:, :2 * T] = s
        return carry

    lax.fori_loop(0, (n_tiles - 1) // near_unroll, near_body, 0)

    for i in range(2, n_tiles):
        n_far = i - 1
        if n_far % 2:
            update(i, n_far - 1, 1, None)
        if n_far % FAR_GROUP >= 2:
            update(i, n_far - n_far % FAR_GROUP, 2, None)
    n_group = sum((i - 1) // FAR_GROUP for i in range(1, n_tiles))
    far_unroll = _largest_divisor(n_group, (7, 5, 4, 3, 2, 1))
    shift = FAR_GROUP.bit_length() - 1

    def far_body(_, carry):
        i, t = carry
        s = snext_s[...]
        for _u in range(far_unroll):
            wrap = t + 1 >= jnp.right_shift(i - 1, shift)
            i_next, t_next = jnp.where(wrap, i + 1, i), jnp.where(wrap, 0, t + 1)
            t_next = jnp.where(i_next > last_tile, 0, t_next)
            i_next = jnp.minimum(i_next, last_tile)
            s_next = scores(i_next, FAR_GROUP * t_next, FAR_GROUP)
            apply(i, s, FAR_GROUP * t, FAR_GROUP, None)
            s, i, t = s_next, i_next, t_next
        snext_s[...] = s
        return i, t

    if n_group:
        snext_s[...] = scores(FAR_GROUP + 1, 0, FAR_GROUP)
        lax.fori_loop(0, n_group // far_unroll, far_body, (jnp.int32(FAR_GROUP + 1), jnp.int32(0)))

    lam = _lambda(l1q_ref, l1k_ref, l2q_ref, l2k_ref, lam_init)
    for i in range(n_tiles):
        acc = acc_s[i]
        o_ref[i * T:(i + 1) * T, :] = _diff_finish(acc[:, :hd], acc[:, hd:], lam, g_ref[...], lam_init).astype(BF16)


def _prompt_attn(rel_bias, q, k, v, lams, g_head, *, batch, seq, tile, lam_init):
    assert tile >= MAX_DISTANCE and tile % ATTN_CHUNK == 0 and seq % tile == 0
    nq = seq // tile
    hd = 2 * DA_DH
    lam_spec = _const_spec((1, DA_DH))
    seq_spec = pl.BlockSpec((seq, hd), lambda h, b: (b, h))
    feat_spec = pl.BlockSpec((1, hd, seq), lambda h, b: (b, h, 0))
    return pl.pallas_call(
        functools.partial(_prompt_attn_kernel, tile=tile, n_tiles=nq, lam_init=lam_init),
        grid=(DA_HEADS, batch),
        in_specs=[pl.BlockSpec(memory_space=pltpu.SMEM), seq_spec,
                  feat_spec, seq_spec,
                  lam_spec, lam_spec, lam_spec, lam_spec, _const_spec((1, hd))],
        out_specs=seq_spec,
        out_shape=jax.ShapeDtypeStruct((batch * seq, DA_Q), BF16),
        scratch_shapes=[pltpu.VMEM((2 * tile, 2 * tile), F32), pltpu.VMEM((2, seq, hd), BF16),
                        pltpu.VMEM((nq, 2 * tile, LANES), F32), pltpu.VMEM((nq, 2 * tile, 2 * hd), F32),
                        pltpu.VMEM((2 * tile, FAR_GROUP * tile), F32)],
        compiler_params=_params(2), name="diff_attn_prompt",
    )(rel_bias, q, k, v, *lams, g_head)


def _sample_attn_kernel(rb_ref, q_ref, kp_ref, vp_ref, kn_ref, vn_ref, l1q_ref, l1k_ref, l2q_ref, l2k_ref,
                        g_ref, o_ref, bias_p, bias_n, *, past, seq, lam_init):
    b = pl.program_id(0)
    hd = 2 * DA_DH
    dot = functools.partial(jnp.dot, preferred_element_type=F32)
    nt = lambda a, bb: lax.dot_general(a, bb, (((1,), (1,)), ((), ())), preferred_element_type=F32)

    @pl.when(b == 0)
    def _():
        qp = lax.broadcasted_iota(jnp.int32, (seq, past), 0) + past
        kp = lax.broadcasted_iota(jnp.int32, (seq, past), 1)
        qn = lax.broadcasted_iota(jnp.int32, (seq, seq), 0) + past
        kn = lax.broadcasted_iota(jnp.int32, (seq, seq), 1) + past
        for h in range(DA_HEADS):
            tile = _rel_bias_tile(rb_ref, h, qp, kp)
            bias_p[h] = jnp.concatenate([tile, tile], axis=0)
            tile = _rel_bias_tile(rb_ref, h, qn, kn)
            bias_n[h] = jnp.concatenate([tile, tile], axis=0)

    lam = _lambda(l1q_ref, l1k_ref, l2q_ref, l2k_ref, lam_init)
    v_heads = pltpu.einshape("khd->hkd", vp_ref[0])
    for h in range(DA_HEADS):
        cols = slice(h * hd, (h + 1) * hd)
        qs = _stack_maps(q_ref[:, cols])
        s_p = dot(qs, kp_ref[0, cols, :].astype(BF16)) + bias_p[h]
        s_n = nt(qs, kn_ref[:, cols]) + bias_n[h]
        m = jnp.maximum(jnp.max(s_p, axis=-1, keepdims=True), jnp.max(s_n, axis=-1, keepdims=True))
        p_p = jnp.exp2(s_p - m)
        p_n = jnp.exp2(s_n - m)
        l = jnp.sum(p_p, axis=-1, keepdims=True) + jnp.sum(p_n, axis=-1, keepdims=True)
        acc = (dot(p_p.astype(BF16), v_heads[h].astype(BF16))
               + dot(p_n.astype(BF16), vn_ref[:, cols]))
        o_ref[:, cols] = _diff_finish(acc, l, lam, g_ref[...], lam_init).astype(BF16)


def _sample_attn(rel_bias, q, cache_kt, cache_v, k_new, v_new, lams, g_head, *, batch, seq, lam_init):
    past = cache_kt.shape[2]
    hd = 2 * DA_DH
    lam_spec = _const_spec((1, DA_DH))
    new_spec = pl.BlockSpec((seq, DA_Q), lambda b: (b, 0))
    return pl.pallas_call(
        functools.partial(_sample_attn_kernel, past=past, seq=seq, lam_init=lam_init),
        grid=(batch,),
        in_specs=[pl.BlockSpec(memory_space=pltpu.SMEM), new_spec,
                  pl.BlockSpec((1, DA_Q, past), lambda b: (b, 0, 0)),
                  pl.BlockSpec((1, past, DA_HEADS, hd), lambda b: (b, 0, 0, 0)),
                  new_spec, new_spec, lam_spec, lam_spec, lam_spec, lam_spec, _const_spec((1, hd))],
        out_specs=new_spec,
        out_shape=jax.ShapeDtypeStruct((batch * seq, DA_Q), BF16),
        scratch_shapes=[pltpu.VMEM((DA_HEADS, 2 * seq, past), F32), pltpu.VMEM((DA_HEADS, 2 * seq, seq), F32)],
        compiler_params=_params(1), name="diff_attn_sample",
    )(rel_bias, q, cache_kt, cache_v, k_new, v_new, *lams, g_head)


def _prep_weights(norm_g, mlstm_w_in, mlstm_b_gate, mlstm_g_head, mlstm_w_out, kv_g, kv_w, diff_w_q,
                  diff_lam_q1, diff_lam_k1, diff_lam_q2, diff_lam_k2, diff_g_head, diff_w_o,
                  ffn_w_in, ffn_w_out):
    n_main = 2 * M_QK + M_V + D_MODEL
    w_in = mlstm_w_in[0]
    wg = jnp.pad(w_in[:, n_main:], ((0, 0), (0, LANES - 2 * M_HEADS)))
    wg_hi = wg.astype(BF16)
    wg_lo = (wg - wg_hi.astype(F32)).astype(BF16)
    row = lambda a: a.reshape(1, -1)
    return dict(
        norm=[[row(norm_g[l, j]) for j in range(4)] for l in range(2)],
        w_main=w_in.astype(BF16),
        w_gate=jnp.concatenate([wg_hi, wg_lo], axis=1),
        b_gate=jnp.pad(mlstm_b_gate[0], (0, LANES - 2 * M_HEADS)).reshape(1, LANES),
        g_mhead=row(mlstm_g_head[0]),
        w_mout=mlstm_w_out[0].astype(BF16),
        kv_g=row(kv_g), kv_w=kv_w.astype(BF16), w_q=diff_w_q[0].astype(BF16),
        lams=[row(diff_lam_q1[0]), row(diff_lam_k1[0]), row(diff_lam_q2[0]), row(diff_lam_k2[0])],
        g_dhead=row(diff_g_head[0]), w_o=diff_w_o[0].astype(BF16),
        ffn_in=ffn_w_in.astype(BF16), ffn_out=ffn_w_out.astype(BF16),
    )


def _trunk(x, c0, n0, m0, past_k, past_v, rel_bias, w, *, row_tile, chunk, attn_tile):
    batch, seq, _ = x.shape
    rows = batch * seq
    x2d = x.reshape(rows, D_MODEL)
    lam_init = 0.8 - 0.6 * math.exp(-0.3 * 1)

    q, k, v, o, gcol, grow = _inproj(x2d, w["norm"][0][0], w["w_main"], w["w_gate"], w["b_gate"],
                                     tile=row_tile, chunk=chunk)
    m0b = jnp.broadcast_to(m0[..., None], (batch, M_HEADS, LANES))
    x1, c_new, n_new, m_new = _mlstm(q, k, v, o, gcol, grow, x2d, c0, n0, m0b, w["w_mout"], w["g_mhead"],
                                     w["norm"][0][1], batch=batch, chunk=chunk)
    x2 = _ffn(x1, w["norm"][0][2], w["ffn_in"], w["ffn_out"], w["norm"][0][3], layer=0, tile=row_tile)

    aq, k_f32, v_f32, k_bf, v_bf = _qkv(x2, w["norm"][1][0], w["kv_g"], w["w_q"], w["kv_w"], tile=row_tile,
                                        batch=batch, k_transposed=past_k is None)
    if past_k is None:
        attn = _prompt_attn(rel_bias, aq, k_bf, v_bf, w["lams"], w["g_dhead"],
                            batch=batch, seq=seq, tile=attn_tile, lam_init=lam_init)
        k_f32 = k_f32.reshape(batch, DA_HEADS, 2, DA_DH, seq).transpose(0, 4, 1, 2, 3)
    else:
        past = past_k.shape[1]
        past_kt = past_k.transpose(0, 2, 3, 4, 1).reshape(batch, DA_Q, past)
        attn = _sample_attn(rel_bias, aq, past_kt, past_v, k_bf, v_bf, w["lams"], w["g_dhead"],
                            batch=batch, seq=seq, lam_init=lam_init)
    y = _ffn(x2, w["norm"][1][2], w["ffn_in"], w["ffn_out"], w["norm"][1][3], layer=1, tile=row_tile,
             attn=attn, w_o=w["w_o"], g_pre=w["norm"][1][1])

    return (y.reshape(batch, seq, D_MODEL), c_new[None], n_new[None], m_new[None, :, :, 0],
            k_f32.reshape(batch, seq, DA_HEADS, 2, DA_DH), v_f32.reshape(batch, seq, DA_HEADS, 2 * DA_DH))


def kernel(x_prompt, x_sample, state_C, state_n, state_m, cache_k, cache_v, norm_g, mlstm_w_in, mlstm_b_gate, mlstm_g_head, mlstm_w_out, kv_g, kv_w, rel_bias, diff_w_q, diff_lam_q1, diff_lam_k1, diff_lam_q2, diff_lam_k2, diff_g_head, diff_w_o, ffn_w_in, ffn_w_out):
    w = _prep_weights(norm_g, mlstm_w_in, mlstm_b_gate, mlstm_g_head, mlstm_w_out, kv_g, kv_w, diff_w_q,
                      diff_lam_q1, diff_lam_k1, diff_lam_q2, diff_lam_k2, diff_g_head, diff_w_o,
                      ffn_w_in, ffn_w_out)
    pb = x_prompt.shape[0]
    zeros = lambda *s: jnp.zeros(s, F32)
    y_p, p_c, p_n, p_m, p_k, p_v = _trunk(
        x_prompt, zeros(pb, M_HEADS, M_DV, M_DK), zeros(pb, M_HEADS, M_DK), zeros(pb, M_HEADS),
        None, None, rel_bias, w, row_tile=512, chunk=256, attn_tile=256)
    sb, ss, _ = x_sample.shape
    y_s, s_c, s_n, s_m, s_k, s_v = _trunk(
        x_sample, state_C[0], state_n[0], state_m[0], cache_k, cache_v, rel_bias, w,
        row_tile=sb * ss, chunk=ss, attn_tile=None)
    return (y_p, y_s, p_c, p_n, p_m, p_k, p_v, s_c, s_n, s_m, s_k, s_v)
```

```python
import functools
import math

import jax
import jax.numpy as jnp
from jax import lax
from jax.experimental import pallas as pl
from jax.experimental.pallas import tpu as pltpu

F32 = jnp.float32
BF16 = jnp.bfloat16

D_MODEL = 1024
ATTN_CHUNK = 64
M_HEADS = 4
M_DK = D_MODEL // (2 * M_HEADS)
M_DV = D_MODEL // M_HEADS
M_QK = M_HEADS * M_DK
M_V = M_HEADS * M_DV
DA_HEADS = 8
DA_DH = D_MODEL // (2 * DA_HEADS)
DA_Q = DA_HEADS * 2 * DA_DH
FFN_HIDDEN = -(-8 * D_MODEL // (3 * 256)) * 256
N_BUCKETS = 32
MAX_DISTANCE = 128
EPS = 1e-6
LOG2E = math.log2(math.e)

LANES = 128
GATE_ROWS = 16
VMEM_LIMIT = 56 * 1024 * 1024
FAR_GROUP = 4
FFN_COL_CHUNKS = (1024, 1024, 768)
assert sum(FFN_COL_CHUNKS) == FFN_HIDDEN


def _bucket_upper_bounds():
    half = N_BUCKETS // 2
    max_exact = half // 2
    ratio = MAX_DISTANCE // max_exact
    steps = half - max_exact
    bounds = [n + 1 for n in range(max_exact)]
    for k in range(1, steps):
        n = max_exact
        while n ** steps < (max_exact ** steps) * (ratio ** k):
            n += 1
        bounds.append(n)
    return bounds


_BUCKET_BOUNDS = _bucket_upper_bounds()


def _rms(x, g):
    return x * lax.rsqrt(jnp.mean(x * x, axis=-1, keepdims=True) + EPS) * g


def _log_sigmoid(x):
    return jnp.minimum(x, 0.0) - jnp.log1p(jnp.exp(-jnp.abs(x)))


def _split3(a):
    a1 = a.astype(BF16)
    r1 = a - a1.astype(F32)
    a2 = r1.astype(BF16)
    a3 = (r1 - a2.astype(F32)).astype(BF16)
    return a1, a2, a3


def _const_spec(shape):
    return pl.BlockSpec(shape, lambda *_: (0,) * len(shape), pipeline_mode=pl.Buffered(1))


def _params(n_axes):
    return pltpu.CompilerParams(dimension_semantics=("arbitrary",) * n_axes,
                                vmem_limit_bytes=VMEM_LIMIT)


def _mlstm_kernel(x_ref, xnext_ref, g_ref, w_ref, wg_ref, bg_ref, c0_ref, n0_ref, m0_ref,
                  wout_ref, ghead_ref, gpost_ref,
                  x1_ref, cout_ref, nout_ref, mout_ref,
                  q_s, k_s, v_s, o_s, gcol_s, grow_s, ct_s, m_s, hcat_s, *, chunk, chunks_per_step, n_steps):
    L = chunk
    c = pl.program_id(1)
    dot = functools.partial(jnp.dot, preferred_element_type=F32)

    def wide(rep, n):
        return rep[:, :n] if n < LANES else jnp.concatenate([rep] * (n // LANES), axis=1)

    def project(x, slot):
        xn = _rms(x, g_ref[...])
        xh = xn.astype(BF16)
        xl = (xn - xh.astype(F32)).astype(BF16)
        q_s[slot] = dot(xh, w_ref[:, :M_QK]).astype(BF16)
        k_s[slot] = (dot(xh, w_ref[:, M_QK:2 * M_QK]) * (M_DK ** -0.5)).astype(BF16)
        v_s[slot] = dot(xh, w_ref[:, 2 * M_QK:2 * M_QK + M_V]).astype(BF16)
        o_s[slot] = dot(xh, w_ref[:, 2 * M_QK + M_V:2 * M_QK + M_V + D_MODEL])
        gh = dot(xh, wg_ref[...])
        gl = dot(xl, wg_ref[:, :LANES])
        gates = gh[:, :LANES] + gh[:, LANES:] + gl + bg_ref[...]
        lane = lax.broadcasted_iota(jnp.int32, gates.shape, 1)
        gv = jnp.where(lane < M_HEADS, gates, _log_sigmoid(gates))
        gcol_s[slot] = gv
        if L < LANES:
            gv = jnp.concatenate([gv, jnp.zeros((LANES - L, LANES), F32)], axis=0)
        grow_s[slot] = gv.T[:GATE_ROWS, :L]

    row = lax.broadcasted_iota(jnp.int32, (L, L), 0)
    col = lax.broadcasted_iota(jnp.int32, (L, L), 1)
    causal = col <= row
    tri = jnp.where(causal, 1.0, 0.0).astype(BF16)
    tri_t = jnp.where(row <= col, 1.0, 0.0).astype(BF16)
    ones = jnp.ones((L, LANES), BF16)

    def scan(slot, rows):
        gcol = gcol_s[slot]
        grow = grow_s[slot]
        bcol = sum(dot(tri, part) for part in _split3(gcol))
        brow = sum(dot(part, tri_t) for part in _split3(grow))

        for h in range(M_HEADS):
            q = q_s[slot, :, h * M_DK:(h + 1) * M_DK]
            k = k_s[slot, :, h * M_DK:(h + 1) * M_DK]
            v = v_s[slot, :, h * M_DV:(h + 1) * M_DV]
            ig_row = grow[h:h + 1, :]
            b_row = brow[M_HEADS + h:M_HEADS + h + 1, :]
            ig = jnp.broadcast_to(gcol[:, h:h + 1], (L, LANES))
            b = jnp.broadcast_to(bcol[:, M_HEADS + h:M_HEADS + h + 1], (L, LANES))
            m0 = m_s[h:h + 1, :]
            ct = ct_s[h]

            d = jnp.where(causal, wide(b, L) - b_row + ig_row, -jnp.inf)
            g = b + m0
            m = jnp.maximum(g, jnp.max(d, axis=-1, keepdims=True))
            w_inter = jnp.exp(g - m)
            qk = lax.dot_general(q, k, (((1,), (1,)), ((), ())), preferred_element_type=F32)
            s = (jnp.exp(d - wide(m, L)) * qk).astype(BF16)
            inter = dot(q, ct.astype(BF16))
            num = wide(w_inter, M_DV) * inter[:, :M_DV] + dot(s, v)
            den = w_inter * inter[:, M_DV:] + dot(s, ones)
            inv = 1.0 / jnp.maximum(jnp.abs(den), jnp.exp(-m))
            hh = num * wide(inv, M_DV)

            m_last = m[L - 1:L, :]
            w_state = jnp.exp(g[L - 1:L, :] - m_last)
            w_rows = jnp.exp(b[L - 1:L, :] - b + ig - m_last)
            vw = jnp.concatenate([(v.astype(F32) * wide(w_rows, M_DV)).astype(BF16), w_rows.astype(BF16)], axis=1)
            upd = lax.dot_general(k, vw, (((0,), (0,)), ((), ())), preferred_element_type=F32)
            ct_s[h] = wide(w_state, M_DV + LANES) * ct + upd
            m_s[h:h + 1, :] = m_last

            hn = _rms(hh, ghead_ref[:, h * M_DV:(h + 1) * M_DV])
            gate = jax.nn.sigmoid(o_s[slot, :, h * M_DV:(h + 1) * M_DV])
            hcat_s[slot, :, h * M_DV:(h + 1) * M_DV] = (hn * gate).astype(BF16)

        y = dot(hcat_s[slot], wout_ref[...])
        x1_ref[0, rows, :] = x_ref[0, rows, :] + _rms(y, gpost_ref[...])

    @pl.when(c == 0)
    def _():
        for h in range(M_HEADS):
            ct_s[h, :, :M_DV] = c0_ref[0, h].T
            ct_s[h, :, M_DV:] = jnp.broadcast_to(n0_ref[0, h:h + 1, :], (LANES, M_DK)).T
        m_s[...] = m0_ref[0]
        project(x_ref[0, :L, :], 0)

    for j in range(chunks_per_step):
        if j + 1 < chunks_per_step:
            project(x_ref[0, (j + 1) * L:(j + 2) * L, :], (j + 1) % 2)
        scan(j % 2, slice(j * L, (j + 1) * L))
    if n_steps > 1:
        project(xnext_ref[0], chunks_per_step % 2)

    @pl.when(c == n_steps - 1)
    def _():
        for h in range(M_HEADS):
            t = ct_s[h].T
            cout_ref[0, h] = t[:M_DV]
            nout_ref[0, h:h + 1, :] = t[M_DV:M_DV + 1]
        mout_ref[0] = m_s[...]


def _mlstm(x2d, g_in, w_main, w_gate, b_gate, c0, n0, m0b, w_out, g_head, g_post, *, batch, chunk):
    rows = x2d.shape[0]
    seq = rows // batch
    n_chunks = seq // chunk
    cps = 2 if n_chunks % 2 == 0 else 1
    assert cps == 2 or n_chunks == 1
    n_steps = n_chunks // cps
    x3d = x2d.reshape(batch, seq, D_MODEL)
    step_spec = pl.BlockSpec((1, cps * chunk, D_MODEL), lambda b, c: (b, c, 0))
    next_spec = pl.BlockSpec((1, chunk, D_MODEL), lambda b, c: (b, jnp.minimum(cps * (c + 1), n_chunks - 1), 0))
    state4 = pl.BlockSpec((1, M_HEADS, M_DV, M_DK), lambda b, c: (b, 0, 0, 0))
    state3 = pl.BlockSpec((1, M_HEADS, LANES), lambda b, c: (b, 0, 0))
    vec = _const_spec((1, D_MODEL))
    x1, c_new, n_new, m_new = pl.pallas_call(
        functools.partial(_mlstm_kernel, chunk=chunk, chunks_per_step=cps, n_steps=n_steps),
        grid=(batch, n_steps),
        in_specs=[step_spec, next_spec, vec, _const_spec(w_main.shape), _const_spec(w_gate.shape),
                  _const_spec((1, LANES)), state4, state3, state3,
                  _const_spec((M_V, D_MODEL)), _const_spec((1, M_V)), vec],
        out_specs=[step_spec, state4, state3, state3],
        out_shape=[jax.ShapeDtypeStruct((batch, seq, D_MODEL), F32),
                   jax.ShapeDtypeStruct((batch, M_HEADS, M_DV, M_DK), F32),
                   jax.ShapeDtypeStruct((batch, M_HEADS, LANES), F32),
                   jax.ShapeDtypeStruct((batch, M_HEADS, LANES), F32)],
        scratch_shapes=[pltpu.VMEM((2, chunk, M_QK), BF16), pltpu.VMEM((2, chunk, M_QK), BF16),
                        pltpu.VMEM((2, chunk, M_V), BF16), pltpu.VMEM((2, chunk, D_MODEL), F32),
                        pltpu.VMEM((2, chunk, LANES), F32), pltpu.VMEM((2, GATE_ROWS, chunk), F32),
                        pltpu.VMEM((M_HEADS, M_DK, M_DV + LANES), F32), pltpu.VMEM((M_HEADS, LANES), F32),
                        pltpu.VMEM((2, chunk, M_V), BF16)],
        compiler_params=_params(2),
        name="mlstm",
    )(x3d, x3d, g_in, w_main, w_gate, b_gate, c0, n0, m0b, w_out, g_head, g_post)
    return x1.reshape(rows, D_MODEL), c_new, n_new, m_new


def _ffn_body(x, gin_ref, win_ref, wout_ref, gout_ref):
    dot = functools.partial(jnp.dot, preferred_element_type=F32)
    xn = _rms(x, gin_ref[...]).astype(BF16)
    acc = None
    start = 0
    for width in FFN_COL_CHUNKS:
        gate = dot(xn, win_ref[:, start:start + width])
        up = dot(xn, win_ref[:, FFN_HIDDEN + start:FFN_HIDDEN + start + width])
        act = (jax.nn.silu(gate) * up).astype(BF16)
        part = dot(act, wout_ref[start:start + width, :])
        acc = part if acc is None else acc + part
        start += width
    return x + _rms(acc, gout_ref[...])


def _ffn_kernel(x_ref, gin_ref, win_ref, wout_ref, gout_ref, y_ref):
    y_ref[...] = _ffn_body(x_ref[...], gin_ref, win_ref, wout_ref, gout_ref)


def _attn_out_ffn_kernel(x_ref, a_ref, wo_ref, gpre_ref, gin_ref, win_ref, wout_ref, gout_ref, y_ref):
    mix = jnp.dot(a_ref[...], wo_ref[...], preferred_element_type=F32)
    x = x_ref[...] + _rms(mix, gpre_ref[...])
    y_ref[...] = _ffn_body(x, gin_ref, win_ref, wout_ref, gout_ref)


def _ffn(x2d, g_in, w_in, w_out, g_out, *, layer, tile, attn=None, w_o=None, g_pre=None):
    rows = x2d.shape[0]
    row_spec = lambda cols: pl.BlockSpec((tile, cols), lambda i: (i, 0))
    vec = _const_spec((1, D_MODEL))
    layer_spec = lambda w: pl.BlockSpec((None,) + w.shape[1:], lambda i: (layer, 0, 0),
                                        pipeline_mode=pl.Buffered(1))
    ffn_specs = [vec, layer_spec(w_in), layer_spec(w_out), vec]
    if attn is None:
        kern, in_specs = _ffn_kernel, [row_spec(D_MODEL)] + ffn_specs
        args = (x2d, g_in, w_in, w_out, g_out)
    else:
        kern = _attn_out_ffn_kernel
        in_specs = [row_spec(D_MODEL), row_spec(DA_Q), _const_spec(w_o.shape), vec] + ffn_specs
        args = (x2d, attn, w_o, g_pre, g_in, w_in, w_out, g_out)
    return pl.pallas_call(
        kern, grid=(rows // tile,), in_specs=in_specs, out_specs=row_spec(D_MODEL),
        out_shape=jax.ShapeDtypeStruct((rows, D_MODEL), F32),
        compiler_params=_params(1), name="ffn",
    )(*args)


def _qkv_kernel(x_ref, gq_ref, gkv_ref, wq_ref, wkv_ref, q_ref, kf_ref, vf_ref, kb_ref, vb_ref, *, k_transposed):
    x = x_ref[...]
    xhat = x * lax.rsqrt(jnp.mean(x * x, axis=-1, keepdims=True) + EPS)
    xq = (xhat * gq_ref[...]).astype(BF16)
    xkv = (xhat * gkv_ref[...]).astype(BF16)
    dot = functools.partial(jnp.dot, preferred_element_type=F32)
    q_ref[...] = (dot(xq, wq_ref[...]) * (DA_DH ** -0.5 * LOG2E)).astype(BF16)
    kk = dot(xkv, wkv_ref[:, :DA_Q])
    if k_transposed:
        kt = kk.T
        kf_ref[0] = kt
        kb_ref[0] = kt.astype(BF16)
    else:
        kf_ref[...] = kk
        kb_ref[...] = kk.astype(BF16)
    vv = dot(xkv, wkv_ref[:, DA_Q:])
    vf_ref[...] = vv
    vb_ref[...] = vv.astype(BF16)


def _qkv(x2d, g_q, g_kv, w_q, w_kv, *, tile, batch, k_transposed):
    rows = x2d.shape[0]
    seq = rows // batch
    row_spec = pl.BlockSpec((tile, D_MODEL), lambda i: (i, 0))
    vec = _const_spec((1, D_MODEL))
    shp = lambda dt: jax.ShapeDtypeStruct((rows, DA_Q), dt)
    if k_transposed:
        tiles_per_seq = seq // tile
        k_spec = pl.BlockSpec((1, DA_Q, tile), lambda i: (i // tiles_per_seq, 0, i % tiles_per_seq))
        k_shp = lambda dt: jax.ShapeDtypeStruct((batch, DA_Q, seq), dt)
    else:
        k_spec, k_shp = row_spec, shp
    return pl.pallas_call(
        functools.partial(_qkv_kernel, k_transposed=k_transposed), grid=(rows // tile,),
        in_specs=[row_spec, vec, vec, _const_spec(w_q.shape), _const_spec(w_kv.shape)],
        out_specs=[row_spec, k_spec, row_spec, k_spec, row_spec],
        out_shape=[shp(BF16), k_shp(F32), shp(F32), k_shp(BF16), shp(BF16)],
        compiler_params=_params(1), name="attn_qkv",
    )(x2d, g_q, g_kv, w_q, w_kv)


def _rel_bias_tile(rb_ref, head, q_pos, k_pos):
    rel = k_pos - q_pos
    n = jnp.abs(rel)
    half = N_BUCKETS // 2

    def table(offset):
        val = jnp.full(rel.shape, rb_ref[offset + half - 1, head], F32)
        for bucket in reversed(range(half - 1)):
            val = jnp.where(n < _BUCKET_BOUNDS[bucket], rb_ref[offset + bucket, head], val)
        return val

    bias = jnp.where(rel > 0, table(half), table(0)) * LOG2E
    shift = ATTN_CHUNK.bit_length() - 1
    visible = jnp.right_shift(k_pos, shift) <= jnp.right_shift(q_pos, shift)
    return jnp.where(visible, bias, -jnp.inf)


def _lambda(l1q_ref, l1k_ref, l2q_ref, l2k_ref, lam_init):
    a = jnp.exp(jnp.sum(l1q_ref[...] * l1k_ref[...], axis=-1, keepdims=True))
    b = jnp.exp(jnp.sum(l2q_ref[...] * l2k_ref[...], axis=-1, keepdims=True))
    return a - b + lam_init


def _stack_maps(q):
    lane = lax.broadcasted_iota(jnp.int32, q.shape, 1)
    zero = jnp.zeros_like(q)
    return jnp.concatenate([jnp.where(lane < DA_DH, q, zero), jnp.where(lane >= DA_DH, q, zero)], axis=0)


def _diff_finish(acc, l, lam, g, lam_init):
    t = acc.shape[0] // 2
    o = acc[:t] / l[:t] - lam * (acc[t:] / l[t:])
    return _rms(o, g) * (1.0 - lam_init)


def _largest_divisor(n, candidates):
    return max(u for u in candidates if n % u == 0)


def _prompt_attn_kernel(rb_ref, q_ref, k_ref, v_ref, l1q_ref, l1k_ref, l2q_ref, l2k_ref, g_ref,
                        o_ref, bias_s, qm_s, m_s, acc_s, snext_s, *, tile, n_tiles, lam_init):
    T = tile
    hd = 2 * DA_DH
    h = pl.program_id(0)
    b = pl.program_id(1)
    dot = functools.partial(jnp.dot, preferred_element_type=F32)

    @pl.when(b == 0)
    def _():
        qp = lax.broadcasted_iota(jnp.int32, (T, 2 * T), 0) + T
        kp = lax.broadcasted_iota(jnp.int32, (T, 2 * T), 1)
        near = _rel_bias_tile(rb_ref, h, qp, kp)
        bias_s[...] = jnp.concatenate([near, near], axis=0)

    q = q_ref[...]
    lane = lax.broadcasted_iota(jnp.int32, q.shape, 1)
    zero = jnp.zeros_like(q)
    qm_s[0] = jnp.where(lane < DA_DH, q, zero)
    qm_s[1] = jnp.where(lane >= DA_DH, q, zero)
    far_bias = rb_ref[N_BUCKETS // 2 - 1, h] * LOG2E

    def update(i, key_block, n_blocks, bias):
        apply(i, scores(i, key_block, n_blocks), key_block, n_blocks, bias)

    def scores(i, key_block, n_blocks):
        q_rows = pl.ds(pl.multiple_of(i * T, T), T)
        kb = k_ref[0, :, pl.ds(pl.multiple_of(key_block * T, T), n_blocks * T)]
        return jnp.concatenate([dot(qm_s[0, q_rows, :], kb), dot(qm_s[1, q_rows, :], kb)], axis=0)

    def apply(i, s, key_block, n_blocks, bias):
        vb = v_ref[pl.ds(pl.multiple_of(key_block * T, T), n_blocks * T), :]
        vext = jnp.concatenate([vb, jnp.ones_like(vb)], axis=1)
        first = bias is not None
        if first:
            s = s + bias
            m_new = jnp.broadcast_to(jnp.max(s, axis=-1, keepdims=True), (2 * T, LANES))
        else:
            m_prev = m_s[i]
            m_new = jnp.maximum(m_prev, jnp.max(s, axis=-1, keepdims=True))
        p = jnp.exp2(s - jnp.concatenate([m_new] * (n_blocks * T // LANES), axis=1))
        pv = jnp.dot(p.astype(BF16), vext, preferred_element_type=F32)
        if first:
            acc_s[i] = pv
            m_s[i] = m_new - far_bias
        else:
            alpha = jnp.exp2(m_prev - m_new)
            acc_s[i] = jnp.concatenate([alpha, alpha], axis=1) * acc_s[i] + pv
            m_s[i] = m_new

    update(0, 0, 1, bias_s[:, T:])
    near_unroll = _largest_divisor(n_tiles - 1, (5, 3, 2, 1))

    last_tile = n_tiles - 1
    if n_tiles > 1:
        snext_s[:, :2 * T] = scores(1, 0, 2)

    def near_body(it, carry):
        s = snext_s[:, :2 * T]
        for u in range(near_unroll):
            i = 1 + it * near_unroll + u
            i_next = jnp.minimum(i + 1, last_tile)
            s_next = scores(i_next, i_next - 1, 2)
            apply(i, s, i - 1, 2, bias_s[...])
            s = s_next
        snext_s[:, :2 * T] = s
        return carry

    lax.fori_loop(0, (n_tiles - 1) // near_unroll, near_body, 0)

    rest = []
    for i in range(2, n_tiles):
        n_far = i - 1
        if n_far % 2:
            rest.append((i, n_far - 1, 1))
        if n_far % FAR_GROUP >= 2:
            rest.append((i, n_far - n_far % FAR_GROUP, 2))
    s = scores(*rest[0]) if rest else None
    for idx, (i, key_block, n_blocks) in enumerate(rest):
        s_next = scores(*rest[idx + 1]) if idx + 1 < len(rest) else None
        apply(i, s, key_block, n_blocks, None)
        s = s_next
    n_group = sum((i - 1) // FAR_GROUP for i in range(1, n_tiles))
    far_unroll = _largest_divisor(n_group, (7, 5, 4, 3, 2, 1))
    shift = FAR_GROUP.bit_length() - 1

    def far_body(_, carry):
        i, t = carry
        s = snext_s[...]
        for _u in range(far_unroll):
            wrap = t + 1 >= jnp.right_shift(i - 1, shift)
            i_next, t_next = jnp.where(wrap, i + 1, i), jnp.where(wrap, 0, t + 1)
            t_next = jnp.where(i_next > last_tile, 0, t_next)
            i_next = jnp.minimum(i_next, last_tile)
            s_next = scores(i_next, FAR_GROUP * t_next, FAR_GROUP)
            apply(i, s, FAR_GROUP * t, FAR_GROUP, None)
            s, i, t = s_next, i_next, t_next
        snext_s[...] = s
        return i, t

    if n_group:
        snext_s[...] = scores(FAR_GROUP + 1, 0, FAR_GROUP)
        lax.fori_loop(0, n_group // far_unroll, far_body, (jnp.int32(FAR_GROUP + 1), jnp.int32(0)))

    lam = _lambda(l1q_ref, l1k_ref, l2q_ref, l2k_ref, lam_init)
    for i in range(n_tiles):
        acc = acc_s[i]
        o_ref[i * T:(i + 1) * T, :] = _diff_finish(acc[:, :hd], acc[:, hd:], lam, g_ref[...], lam_init).astype(BF16)


def _prompt_attn(rel_bias, q, k, v, lams, g_head, *, batch, seq, tile, lam_init):
    assert tile >= MAX_DISTANCE and tile % ATTN_CHUNK == 0 and seq % tile == 0
    nq = seq // tile
    hd = 2 * DA_DH
    lam_spec = _const_spec((1, DA_DH))
    seq_spec = pl.BlockSpec((seq, hd), lambda h, b: (b, h))
    feat_spec = pl.BlockSpec((1, hd, seq), lambda h, b: (b, h, 0))
    return pl.pallas_call(
        functools.partial(_prompt_attn_kernel, tile=tile, n_tiles=nq, lam_init=lam_init),
        grid=(DA_HEADS, batch),
        in_specs=[pl.BlockSpec(memory_space=pltpu.SMEM), seq_spec,
                  feat_spec, seq_spec,
                  lam_spec, lam_spec, lam_spec, lam_spec, _const_spec((1, hd))],
        out_specs=seq_spec,
        out_shape=jax.ShapeDtypeStruct((batch * seq, DA_Q), BF16),
        scratch_shapes=[pltpu.VMEM((2 * tile, 2 * tile), F32), pltpu.VMEM((2, seq, hd), BF16),
                        pltpu.VMEM((nq, 2 * tile, LANES), F32), pltpu.VMEM((nq, 2 * tile, 2 * hd), F32),
                        pltpu.VMEM((2 * tile, FAR_GROUP * tile), F32)],
        compiler_params=_params(2), name="diff_attn_prompt",
    )(rel_bias, q, k, v, *lams, g_head)


def _sample_attn_kernel(rb_ref, q_ref, kp_ref, vp_ref, kn_ref, vn_ref, l1q_ref, l1k_ref, l2q_ref, l2k_ref,
                        g_ref, o_ref, bias_p, bias_n, *, past, seq, lam_init):
    b = pl.program_id(0)
    hd = 2 * DA_DH
    dot = functools.partial(jnp.dot, preferred_element_type=F32)
    nt = lambda a, bb: lax.dot_general(a, bb, (((1,), (1,)), ((), ())), preferred_element_type=F32)

    @pl.when(b == 0)
    def _():
        qp = lax.broadcasted_iota(jnp.int32, (seq, past), 0) + past
        kp = lax.broadcasted_iota(jnp.int32, (seq, past), 1)
        qn = lax.broadcasted_iota(jnp.int32, (seq, seq), 0) + past
        kn = lax.broadcasted_iota(jnp.int32, (seq, seq), 1) + past
        for h in range(DA_HEADS):
            tile = _rel_bias_tile(rb_ref, h, qp, kp)
            bias_p[h] = jnp.concatenate([tile, tile], axis=0)
            tile = _rel_bias_tile(rb_ref, h, qn, kn)
            bias_n[h] = jnp.concatenate([tile, tile], axis=0)

    lam = _lambda(l1q_ref, l1k_ref, l2q_ref, l2k_ref, lam_init)
    v_heads = pltpu.einshape("khd->hkd", vp_ref[0])
    for h in range(DA_HEADS):
        cols = slice(h * hd, (h + 1) * hd)
        qs = _stack_maps(q_ref[:, cols])
        s_p = dot(qs, kp_ref[0, cols, :].astype(BF16)) + bias_p[h]
        s_n = nt(qs, kn_ref[:, cols]) + bias_n[h]
        m = jnp.maximum(jnp.max(s_p, axis=-1, keepdims=True), jnp.max(s_n, axis=-1, keepdims=True))
        p_p = jnp.exp2(s_p - m)
        p_n = jnp.exp2(s_n - m)
        l = jnp.sum(p_p, axis=-1, keepdims=True) + jnp.sum(p_n, axis=-1, keepdims=True)
        acc = (dot(p_p.astype(BF16), v_heads[h].astype(BF16))
               + dot(p_n.astype(BF16), vn_ref[:, cols]))
        o_ref[:, cols] = _diff_finish(acc, l, lam, g_ref[...], lam_init).astype(BF16)


def _sample_attn(rel_bias, q, cache_kt, cache_v, k_new, v_new, lams, g_head, *, batch, seq, lam_init):
    past = cache_kt.shape[2]
    hd = 2 * DA_DH
    lam_spec = _const_spec((1, DA_DH))
    new_spec = pl.BlockSpec((seq, DA_Q), lambda b: (b, 0))
    return pl.pallas_call(
        functools.partial(_sample_attn_kernel, past=past, seq=seq, lam_init=lam_init),
        grid=(batch,),
        in_specs=[pl.BlockSpec(memory_space=pltpu.SMEM), new_spec,
                  pl.BlockSpec((1, DA_Q, past), lambda b: (b, 0, 0)),
                  pl.BlockSpec((1, past, DA_HEADS, hd), lambda b: (b, 0, 0, 0)),
                  new_spec, new_spec, lam_spec, lam_spec, lam_spec, lam_spec, _const_spec((1, hd))],
        out_specs=new_spec,
        out_shape=jax.ShapeDtypeStruct((batch * seq, DA_Q), BF16),
        scratch_shapes=[pltpu.VMEM((DA_HEADS, 2 * seq, past), F32), pltpu.VMEM((DA_HEADS, 2 * seq, seq), F32)],
        compiler_params=_params(1), name="diff_attn_sample",
    )(rel_bias, q, cache_kt, cache_v, k_new, v_new, *lams, g_head)


def _prep_weights(norm_g, mlstm_w_in, mlstm_b_gate, mlstm_g_head, mlstm_w_out, kv_g, kv_w, diff_w_q,
                  diff_lam_q1, diff_lam_k1, diff_lam_q2, diff_lam_k2, diff_g_head, diff_w_o,
                  ffn_w_in, ffn_w_out):
    n_main = 2 * M_QK + M_V + D_MODEL
    w_in = mlstm_w_in[0]
    wg = jnp.pad(w_in[:, n_main:], ((0, 0), (0, LANES - 2 * M_HEADS)))
    wg_hi = wg.astype(BF16)
    wg_lo = (wg - wg_hi.astype(F32)).astype(BF16)
    row = lambda a: a.reshape(1, -1)
    return dict(
        norm=[[row(norm_g[l, j]) for j in range(4)] for l in range(2)],
        w_main=w_in.astype(BF16),
        w_gate=jnp.concatenate([wg_hi, wg_lo], axis=1),
        b_gate=jnp.pad(mlstm_b_gate[0], (0, LANES - 2 * M_HEADS)).reshape(1, LANES),
        g_mhead=row(mlstm_g_head[0]),
        w_mout=mlstm_w_out[0].astype(BF16),
        kv_g=row(kv_g), kv_w=kv_w.astype(BF16), w_q=diff_w_q[0].astype(BF16),
        lams=[row(diff_lam_q1[0]), row(diff_lam_k1[0]), row(diff_lam_q2[0]), row(diff_lam_k2[0])],
        g_dhead=row(diff_g_head[0]), w_o=diff_w_o[0].astype(BF16),
        ffn_in=ffn_w_in.astype(BF16), ffn_out=ffn_w_out.astype(BF16),
    )


def _trunk(x, c0, n0, m0, past_k, past_v, rel_bias, w, *, row_tile, ffn_tile, chunk, attn_tile):
    batch, seq, _ = x.shape
    rows = batch * seq
    x2d = x.reshape(rows, D_MODEL)
    lam_init = 0.8 - 0.6 * math.exp(-0.3 * 1)

    m0b = jnp.broadcast_to(m0[..., None], (batch, M_HEADS, LANES))
    x1, c_new, n_new, m_new = _mlstm(x2d, w["norm"][0][0], w["w_main"], w["w_gate"], w["b_gate"], c0, n0, m0b,
                                     w["w_mout"], w["g_mhead"], w["norm"][0][1], batch=batch, chunk=chunk)
    x2 = _ffn(x1, w["norm"][0][2], w["ffn_in"], w["ffn_out"], w["norm"][0][3], layer=0, tile=ffn_tile)

    aq, k_f32, v_f32, k_bf, v_bf = _qkv(x2, w["norm"][1][0], w["kv_g"], w["w_q"], w["kv_w"], tile=row_tile,
                                        batch=batch, k_transposed=past_k is None)
    if past_k is None:
        attn = _prompt_attn(rel_bias, aq, k_bf, v_bf, w["lams"], w["g_dhead"],
                            batch=batch, seq=seq, tile=attn_tile, lam_init=lam_init)
        k_f32 = k_f32.reshape(batch, DA_HEADS, 2, DA_DH, seq).transpose(0, 4, 1, 2, 3)
    else:
        past = past_k.shape[1]
        past_kt = past_k.transpose(0, 2, 3, 4, 1).reshape(batch, DA_Q, past)
        attn = _sample_attn(rel_bias, aq, past_kt, past_v, k_bf, v_bf, w["lams"], w["g_dhead"],
                            batch=batch, seq=seq, lam_init=lam_init)
    y = _ffn(x2, w["norm"][1][2], w["ffn_in"], w["ffn_out"], w["norm"][1][3], layer=1, tile=ffn_tile,
             attn=attn, w_o=w["w_o"], g_pre=w["norm"][1][1])

    return (y.reshape(batch, seq, D_MODEL), c_new[None], n_new[None], m_new[None, :, :, 0],
            k_f32.reshape(batch, seq, DA_HEADS, 2, DA_DH), v_f32.reshape(batch, seq, DA_HEADS, 2 * DA_DH))


def kernel(x_prompt, x_sample, state_C, state_n, state_m, cache_k, cache_v, norm_g, mlstm_w_in, mlstm_b_gate, mlstm_g_head, mlstm_w_out, kv_g, kv_w, rel_bias, diff_w_q, diff_lam_q1, diff_lam_k1, diff_lam_q2, diff_lam_k2, diff_g_head, diff_w_o, ffn_w_in, ffn_w_out):
    w = _prep_weights(norm_g, mlstm_w_in, mlstm_b_gate, mlstm_g_head, mlstm_w_out, kv_g, kv_w, diff_w_q,
                      diff_lam_q1, diff_lam_k1, diff_lam_q2, diff_lam_k2, diff_g_head, diff_w_o,
                      ffn_w_in, ffn_w_out)
    pb = x_prompt.shape[0]
    zeros = lambda *s: jnp.zeros(s, F32)
    y_p, p_c, p_n, p_m, p_k, p_v = _trunk(
        x_prompt, zeros(pb, M_HEADS, M_DV, M_DK), zeros(pb, M_HEADS, M_DK), zeros(pb, M_HEADS),
        None, None, rel_bias, w, row_tile=512, ffn_tile=512, chunk=256, attn_tile=256)
    sb, ss, _ = x_sample.shape
    y_s, s_c, s_n, s_m, s_k, s_v = _trunk(
        x_sample, state_C[0], state_n[0], state_m[0], cache_k, cache_v, rel_bias, w,
        row_tile=sb * ss, ffn_tile=sb * ss, chunk=ss, attn_tile=None)
    return (y_p, y_s, p_c, p_n, p_m, p_k, p_v, s_c, s_n, s_m, s_k, s_v)
```

```python
import functools
import math

import jax
import jax.numpy as jnp
from jax import lax
from jax.experimental import pallas as pl
from jax.experimental.pallas import tpu as pltpu

F32 = jnp.float32
BF16 = jnp.bfloat16

D_MODEL = 1024
ATTN_CHUNK = 64
M_HEADS = 4
M_DK = D_MODEL // (2 * M_HEADS)
M_DV = D_MODEL // M_HEADS
M_QK = M_HEADS * M_DK
M_V = M_HEADS * M_DV
DA_HEADS = 8
DA_DH = D_MODEL // (2 * DA_HEADS)
DA_Q = DA_HEADS * 2 * DA_DH
FFN_HIDDEN = -(-8 * D_MODEL // (3 * 256)) * 256
N_BUCKETS = 32
MAX_DISTANCE = 128
EPS = 1e-6
LOG2E = math.log2(math.e)

LANES = 128
GATE_ROWS = 16
VMEM_LIMIT = 56 * 1024 * 1024
FAR_GROUP = 4
FFN_GROUP_ROWS = 512
FFN_COL_CHUNKS = (1024, 1024, 768)
assert sum(FFN_COL_CHUNKS) == FFN_HIDDEN


def _bucket_upper_bounds():
    half = N_BUCKETS // 2
    max_exact = half // 2
    ratio = MAX_DISTANCE // max_exact
    steps = half - max_exact
    bounds = [n + 1 for n in range(max_exact)]
    for k in range(1, steps):
        n = max_exact
        while n ** steps < (max_exact ** steps) * (ratio ** k):
            n += 1
        bounds.append(n)
    return bounds


_BUCKET_BOUNDS = _bucket_upper_bounds()


def _rms(x, g):
    return x * lax.rsqrt(jnp.mean(x * x, axis=-1, keepdims=True) + EPS) * g


def _log_sigmoid(x):
    return jnp.minimum(x, 0.0) - jnp.log1p(jnp.exp(-jnp.abs(x)))


def _split3(a):
    a1 = a.astype(BF16)
    r1 = a - a1.astype(F32)
    a2 = r1.astype(BF16)
    a3 = (r1 - a2.astype(F32)).astype(BF16)
    return a1, a2, a3


def _const_spec(shape):
    return pl.BlockSpec(shape, lambda *_: (0,) * len(shape), pipeline_mode=pl.Buffered(1))


def _params(n_axes):
    return pltpu.CompilerParams(dimension_semantics=("arbitrary",) * n_axes,
                                vmem_limit_bytes=VMEM_LIMIT)


def _mlstm_kernel(x_ref, xnext_ref, g_ref, w_ref, wg_ref, bg_ref, c0_ref, n0_ref, m0_ref,
                  wout_ref, ghead_ref, gpost_ref,
                  x1_ref, cout_ref, nout_ref, mout_ref,
                  q_s, k_s, v_s, o_s, gcol_s, grow_s, ct_s, m_s, hcat_s, *, chunk, chunks_per_step, n_steps):
    L = chunk
    c = pl.program_id(1)
    dot = functools.partial(jnp.dot, preferred_element_type=F32)

    def wide(rep, n):
        return rep[:, :n] if n < LANES else jnp.concatenate([rep] * (n // LANES), axis=1)

    def project(x, slot):
        xn = _rms(x, g_ref[...])
        xh = xn.astype(BF16)
        xl = (xn - xh.astype(F32)).astype(BF16)
        q_s[slot] = dot(xh, w_ref[:, :M_QK]).astype(BF16)
        k_s[slot] = (dot(xh, w_ref[:, M_QK:2 * M_QK]) * (M_DK ** -0.5)).astype(BF16)
        v_s[slot] = dot(xh, w_ref[:, 2 * M_QK:2 * M_QK + M_V]).astype(BF16)
        o_s[slot] = dot(xh, w_ref[:, 2 * M_QK + M_V:2 * M_QK + M_V + D_MODEL])
        gh = dot(xh, wg_ref[...])
        gl = dot(xl, wg_ref[:, :LANES])
        gates = gh[:, :LANES] + gh[:, LANES:] + gl + bg_ref[...]
        lane = lax.broadcasted_iota(jnp.int32, gates.shape, 1)
        gv = jnp.where(lane < M_HEADS, gates, _log_sigmoid(gates))
        gcol_s[slot] = gv
        if L < LANES:
            gv = jnp.concatenate([gv, jnp.zeros((LANES - L, LANES), F32)], axis=0)
        grow_s[slot] = gv.T[:GATE_ROWS, :L]

    row = lax.broadcasted_iota(jnp.int32, (L, L), 0)
    col = lax.broadcasted_iota(jnp.int32, (L, L), 1)
    causal = col <= row
    tri = jnp.where(causal, 1.0, 0.0).astype(BF16)
    tri_t = jnp.where(row <= col, 1.0, 0.0).astype(BF16)
    ones = jnp.ones((L, LANES), BF16)

    def scan(slot, rows):
        gcol = gcol_s[slot]
        grow = grow_s[slot]
        bcol = sum(dot(tri, part) for part in _split3(gcol))
        brow = sum(dot(part, tri_t) for part in _split3(grow))

        for h in range(M_HEADS):
            q = q_s[slot, :, h * M_DK:(h + 1) * M_DK]
            k = k_s[slot, :, h * M_DK:(h + 1) * M_DK]
            v = v_s[slot, :, h * M_DV:(h + 1) * M_DV]
            ig_row = grow[h:h + 1, :]
            b_row = brow[M_HEADS + h:M_HEADS + h + 1, :]
            ig = jnp.broadcast_to(gcol[:, h:h + 1], (L, LANES))
            b = jnp.broadcast_to(bcol[:, M_HEADS + h:M_HEADS + h + 1], (L, LANES))
            m0 = m_s[h:h + 1, :]
            ct = ct_s[h]

            d = jnp.where(causal, wide(b, L) - b_row + ig_row, -jnp.inf)
            g = b + m0
            m = jnp.maximum(g, jnp.max(d, axis=-1, keepdims=True))
            w_inter = jnp.exp(g - m)
            qk = lax.dot_general(q, k, (((1,), (1,)), ((), ())), preferred_element_type=F32)
            s = (jnp.exp(d - wide(m, L)) * qk).astype(BF16)
            inter = dot(q, ct.astype(BF16))
            num = wide(w_inter, M_DV) * inter[:, :M_DV] + dot(s, v)
            den = w_inter * inter[:, M_DV:] + dot(s, ones)
            inv = 1.0 / jnp.maximum(jnp.abs(den), jnp.exp(-m))
            hh = num * wide(inv, M_DV)

            m_last = m[L - 1:L, :]
            w_state = jnp.exp(g[L - 1:L, :] - m_last)
            w_rows = jnp.exp(b[L - 1:L, :] - b + ig - m_last)
            vw = jnp.concatenate([(v.astype(F32) * wide(w_rows, M_DV)).astype(BF16), w_rows.astype(BF16)], axis=1)
            upd = lax.dot_general(k, vw, (((0,), (0,)), ((), ())), preferred_element_type=F32)
            ct_s[h] = wide(w_state, M_DV + LANES) * ct + upd
            m_s[h:h + 1, :] = m_last

            hn = _rms(hh, ghead_ref[:, h * M_DV:(h + 1) * M_DV])
            gate = jax.nn.sigmoid(o_s[slot, :, h * M_DV:(h + 1) * M_DV])
            hcat_s[slot, :, h * M_DV:(h + 1) * M_DV] = (hn * gate).astype(BF16)

        y = dot(hcat_s[slot], wout_ref[...])
        x1_ref[0, rows, :] = x_ref[0, rows, :] + _rms(y, gpost_ref[...])

    @pl.when(c == 0)
    def _():
        for h in range(M_HEADS):
            ct_s[h, :, :M_DV] = c0_ref[0, h].T
            ct_s[h, :, M_DV:] = jnp.broadcast_to(n0_ref[0, h:h + 1, :], (LANES, M_DK)).T
        m_s[...] = m0_ref[0]
        project(x_ref[0, :L, :], 0)

    for j in range(chunks_per_step):
        if j + 1 < chunks_per_step:
            project(x_ref[0, (j + 1) * L:(j + 2) * L, :], (j + 1) % 2)
        scan(j % 2, slice(j * L, (j + 1) * L))
    if n_steps > 1:
        project(xnext_ref[0], chunks_per_step % 2)

    @pl.when(c == n_steps - 1)
    def _():
        for h in range(M_HEADS):
            t = ct_s[h].T
            cout_ref[0, h] = t[:M_DV]
            nout_ref[0, h:h + 1, :] = t[M_DV:M_DV + 1]
        mout_ref[0] = m_s[...]


def _mlstm(x2d, g_in, w_main, w_gate, b_gate, c0, n0, m0b, w_out, g_head, g_post, *, batch, chunk):
    rows = x2d.shape[0]
    seq = rows // batch
    n_chunks = seq // chunk
    cps = 2 if n_chunks % 2 == 0 else 1
    assert cps == 2 or n_chunks == 1
    n_steps = n_chunks // cps
    x3d = x2d.reshape(batch, seq, D_MODEL)
    step_spec = pl.BlockSpec((1, cps * chunk, D_MODEL), lambda b, c: (b, c, 0))
    next_spec = pl.BlockSpec((1, chunk, D_MODEL), lambda b, c: (b, jnp.minimum(cps * (c + 1), n_chunks - 1), 0))
    state4 = pl.BlockSpec((1, M_HEADS, M_DV, M_DK), lambda b, c: (b, 0, 0, 0))
    state3 = pl.BlockSpec((1, M_HEADS, LANES), lambda b, c: (b, 0, 0))
    vec = _const_spec((1, D_MODEL))
    x1, c_new, n_new, m_new = pl.pallas_call(
        functools.partial(_mlstm_kernel, chunk=chunk, chunks_per_step=cps, n_steps=n_steps),
        grid=(batch, n_steps),
        in_specs=[step_spec, next_spec, vec, _const_spec(w_main.shape), _const_spec(w_gate.shape),
                  _const_spec((1, LANES)), state4, state3, state3,
                  _const_spec((M_V, D_MODEL)), _const_spec((1, M_V)), vec],
        out_specs=[step_spec, state4, state3, state3],
        out_shape=[jax.ShapeDtypeStruct((batch, seq, D_MODEL), F32),
                   jax.ShapeDtypeStruct((batch, M_HEADS, M_DV, M_DK), F32),
                   jax.ShapeDtypeStruct((batch, M_HEADS, LANES), F32),
                   jax.ShapeDtypeStruct((batch, M_HEADS, LANES), F32)],
        scratch_shapes=[pltpu.VMEM((2, chunk, M_QK), BF16), pltpu.VMEM((2, chunk, M_QK), BF16),
                        pltpu.VMEM((2, chunk, M_V), BF16), pltpu.VMEM((2, chunk, D_MODEL), F32),
                        pltpu.VMEM((2, chunk, LANES), F32), pltpu.VMEM((2, GATE_ROWS, chunk), F32),
                        pltpu.VMEM((M_HEADS, M_DK, M_DV + LANES), F32), pltpu.VMEM((M_HEADS, LANES), F32),
                        pltpu.VMEM((2, chunk, M_V), BF16)],
        compiler_params=_params(2),
        name="mlstm",
    )(x3d, x3d, g_in, w_main, w_gate, b_gate, c0, n0, m0b, w_out, g_head, g_post)
    return x1.reshape(rows, D_MODEL), c_new, n_new, m_new


def _ffn_tile(load_x, gin_ref, win_ref, wout_ref, gout_ref, y_ref):
    dot = functools.partial(jnp.dot, preferred_element_type=F32)
    rows = y_ref.shape[0]
    n = rows // FFN_GROUP_ROWS if rows % FFN_GROUP_ROWS == 0 else 1
    groups = [slice(r * rows // n, (r + 1) * rows // n) for r in range(n)]

    def prologue(r):
        x = load_x(groups[r])
        return x, _rms(x, gin_ref[...]).astype(BF16)

    def epilogue(r, x, acc):
        y_ref[groups[r], :] = x + _rms(acc, gout_ref[...])

    x, xn = prologue(0)
    pending = None
    for r in range(n):
        acc = None
        start = 0
        nxt = None
        for j, width in enumerate(FFN_COL_CHUNKS):
            gate = dot(xn, win_ref[:, start:start + width])
            up = dot(xn, win_ref[:, FFN_HIDDEN + start:FFN_HIDDEN + start + width])
            act = (jax.nn.silu(gate) * up).astype(BF16)
            part = dot(act, wout_ref[start:start + width, :])
            acc = part if acc is None else acc + part
            start += width
            if j == 0:
                if pending is not None:
                    epilogue(*pending)
                if r + 1 < n:
                    nxt = prologue(r + 1)
        pending = (r, x, acc)
        if nxt is not None:
            x, xn = nxt
    epilogue(*pending)


def _ffn_kernel(x_ref, gin_ref, win_ref, wout_ref, gout_ref, y_ref):
    _ffn_tile(lambda rows: x_ref[rows, :], gin_ref, win_ref, wout_ref, gout_ref, y_ref)


def _attn_out_ffn_kernel(x_ref, a_ref, wo_ref, gpre_ref, gin_ref, win_ref, wout_ref, gout_ref, y_ref):
    def load_x(rows):
        mix = jnp.dot(a_ref[rows, :], wo_ref[...], preferred_element_type=F32)
        return x_ref[rows, :] + _rms(mix, gpre_ref[...])

    _ffn_tile(load_x, gin_ref, win_ref, wout_ref, gout_ref, y_ref)


def _ffn(x2d, g_in, w_in, w_out, g_out, *, layer, tile, attn=None, w_o=None, g_pre=None):
    rows = x2d.shape[0]
    row_spec = lambda cols: pl.BlockSpec((tile, cols), lambda i: (i, 0))
    vec = _const_spec((1, D_MODEL))
    layer_spec = lambda w: pl.BlockSpec((None,) + w.shape[1:], lambda i: (layer, 0, 0),
                                        pipeline_mode=pl.Buffered(1))
    ffn_specs = [vec, layer_spec(w_in), layer_spec(w_out), vec]
    if attn is None:
        kern, in_specs = _ffn_kernel, [row_spec(D_MODEL)] + ffn_specs
        args = (x2d, g_in, w_in, w_out, g_out)
    else:
        kern = _attn_out_ffn_kernel
        in_specs = [row_spec(D_MODEL), row_spec(DA_Q), _const_spec(w_o.shape), vec] + ffn_specs
        args = (x2d, attn, w_o, g_pre, g_in, w_in, w_out, g_out)
    return pl.pallas_call(
        kern, grid=(rows // tile,), in_specs=in_specs, out_specs=row_spec(D_MODEL),
        out_shape=jax.ShapeDtypeStruct((rows, D_MODEL), F32),
        compiler_params=_params(1), name="ffn",
    )(*args)


def _qkv_kernel(x_ref, gq_ref, gkv_ref, wq_ref, wkv_ref, q_ref, kf_ref, vf_ref, kb_ref, vb_ref, *, k_transposed):
    x = x_ref[...]
    xhat = x * lax.rsqrt(jnp.mean(x * x, axis=-1, keepdims=True) + EPS)
    xq = (xhat * gq_ref[...]).astype(BF16)
    xkv = (xhat * gkv_ref[...]).astype(BF16)
    dot = functools.partial(jnp.dot, preferred_element_type=F32)
    q_ref[...] = (dot(xq, wq_ref[...]) * (DA_DH ** -0.5 * LOG2E)).astype(BF16)
    kk = dot(xkv, wkv_ref[:, :DA_Q])
    if k_transposed:
        kt = kk.T
        kf_ref[0] = kt
        kb_ref[0] = kt.astype(BF16)
    else:
        kf_ref[...] = kk
        kb_ref[...] = kk.astype(BF16)
    vv = dot(xkv, wkv_ref[:, DA_Q:])
    vf_ref[...] = vv
    vb_ref[...] = vv.astype(BF16)


def _qkv(x2d, g_q, g_kv, w_q, w_kv, *, tile, batch, k_transposed):
    rows = x2d.shape[0]
    seq = rows // batch
    row_spec = pl.BlockSpec((tile, D_MODEL), lambda i: (i, 0))
    vec = _const_spec((1, D_MODEL))
    shp = lambda dt: jax.ShapeDtypeStruct((rows, DA_Q), dt)
    if k_transposed:
        tiles_per_seq = seq // tile
        k_spec = pl.BlockSpec((1, DA_Q, tile), lambda i: (i // tiles_per_seq, 0, i % tiles_per_seq))
        k_shp = lambda dt: jax.ShapeDtypeStruct((batch, DA_Q, seq), dt)
    else:
        k_spec, k_shp = row_spec, shp
    return pl.pallas_call(
        functools.partial(_qkv_kernel, k_transposed=k_transposed), grid=(rows // tile,),
        in_specs=[row_spec, vec, vec, _const_spec(w_q.shape), _const_spec(w_kv.shape)],
        out_specs=[row_spec, k_spec, row_spec, k_spec, row_spec],
        out_shape=[shp(BF16), k_shp(F32), shp(F32), k_shp(BF16), shp(BF16)],
        compiler_params=_params(1), name="attn_qkv",
    )(x2d, g_q, g_kv, w_q, w_kv)


def _rel_bias_tile(rb_ref, head, q_pos, k_pos):
    rel = k_pos - q_pos
    n = jnp.abs(rel)
    half = N_BUCKETS // 2

    def table(offset):
        val = jnp.full(rel.shape, rb_ref[offset + half - 1, head], F32)
        for bucket in reversed(range(half - 1)):
            val = jnp.where(n < _BUCKET_BOUNDS[bucket], rb_ref[offset + bucket, head], val)
        return val

    bias = jnp.where(rel > 0, table(half), table(0)) * LOG2E
    shift = ATTN_CHUNK.bit_length() - 1
    visible = jnp.right_shift(k_pos, shift) <= jnp.right_shift(q_pos, shift)
    return jnp.where(visible, bias, -jnp.inf)


def _lambda(l1q_ref, l1k_ref, l2q_ref, l2k_ref, lam_init):
    a = jnp.exp(jnp.sum(l1q_ref[...] * l1k_ref[...], axis=-1, keepdims=True))
    b = jnp.exp(jnp.sum(l2q_ref[...] * l2k_ref[...], axis=-1, keepdims=True))
    return a - b + lam_init


def _stack_maps(q):
    lane = lax.broadcasted_iota(jnp.int32, q.shape, 1)
    zero = jnp.zeros_like(q)
    return jnp.concatenate([jnp.where(lane < DA_DH, q, zero), jnp.where(lane >= DA_DH, q, zero)], axis=0)


def _diff_finish(acc, l, lam, g, lam_init):
    t = acc.shape[0] // 2
    o = acc[:t] / l[:t] - lam * (acc[t:] / l[t:])
    return _rms(o, g) * (1.0 - lam_init)


def _largest_divisor(n, candidates):
    return max(u for u in candidates if n % u == 0)


def _prompt_attn_kernel(rb_ref, q_ref, k_ref, v_ref, l1q_ref, l1k_ref, l2q_ref, l2k_ref, g_ref,
                        o_ref, bias_s, m_s, acc_s, snext_s, *, tile, n_tiles, lam_init):
    T = tile
    hd = 2 * DA_DH
    h = pl.program_id(0)
    b = pl.program_id(1)
    dot = functools.partial(jnp.dot, preferred_element_type=F32)

    @pl.when(b == 0)
    def _():
        qp = lax.broadcasted_iota(jnp.int32, (T, 2 * T), 0) + T
        kp = lax.broadcasted_iota(jnp.int32, (T, 2 * T), 1)
        near = _rel_bias_tile(rb_ref, h, qp, kp)
        bias_s[...] = jnp.concatenate([near, near], axis=0)

    far_bias = rb_ref[N_BUCKETS // 2 - 1, h] * LOG2E
    lane = lax.broadcasted_iota(jnp.int32, (T, hd), 1)

    def update(i, key_block, n_blocks, bias):
        apply(i, scores(i, key_block, n_blocks), key_block, n_blocks, bias)

    def scores(i, key_block, n_blocks):
        q = q_ref[pl.ds(pl.multiple_of(i * T, T), T), :]
        kb = k_ref[0, :, pl.ds(pl.multiple_of(key_block * T, T), n_blocks * T)]
        zero = jnp.zeros_like(q)
        return jnp.concatenate([dot(jnp.where(lane < DA_DH, q, zero), kb),
                                dot(jnp.where(lane >= DA_DH, q, zero), kb)], axis=0)

    def apply(i, s, key_block, n_blocks, bias):
        vb = v_ref[pl.ds(pl.multiple_of(key_block * T, T), n_blocks * T), :]
        vext = jnp.concatenate([vb, jnp.ones_like(vb)], axis=1)
        first = bias is not None
        if first:
            s = s + bias
            m_new = jnp.broadcast_to(jnp.max(s, axis=-1, keepdims=True), (2 * T, LANES))
        else:
            m_prev = m_s[i]
            m_new = jnp.maximum(m_prev, jnp.max(s, axis=-1, keepdims=True))
        p = jnp.exp2(s - jnp.concatenate([m_new] * (n_blocks * T // LANES), axis=1))
        pv = jnp.dot(p.astype(BF16), vext, preferred_element_type=F32)
        if first:
            acc_s[i] = pv
            m_s[i] = m_new - far_bias
        else:
            alpha = jnp.exp2(m_prev - m_new)
            acc_s[i] = jnp.concatenate([alpha, alpha], axis=1) * acc_s[i] + pv
            m_s[i] = m_new

    update(0, 0, 1, bias_s[:, T:])
    near_unroll = _largest_divisor(n_tiles - 1, (5, 3, 2, 1))

    last_tile = n_tiles - 1
    if n_tiles > 1:
        snext_s[:, :2 * T] = scores(1, 0, 2)

    def near_body(it, carry):
        s = snext_s[:, :2 * T]
        for u in range(near_unroll):
            i = 1 + it * near_unroll + u
            i_next = jnp.minimum(i + 1, last_tile)
            s_next = scores(i_next, i_next - 1, 2)
            apply(i, s, i - 1, 2, bias_s[...])
            s = s_next
        snext_s[:, :2 * T] = s
        return carry

    lax.fori_loop(0, (n_tiles - 1) // near_unroll, near_body, 0)

    rest = []
    for i in range(2, n_tiles):
        n_far = i - 1
        if n_far % 2:
            rest.append((i, n_far - 1, 1))
        if n_far % FAR_GROUP >= 2:
            rest.append((i, n_far - n_far % FAR_GROUP, 2))
    s = scores(*rest[0]) if rest else None
    for idx, (i, key_block, n_blocks) in enumerate(rest):
        s_next = scores(*rest[idx + 1]) if idx + 1 < len(rest) else None
        apply(i, s, key_block, n_blocks, None)
        s = s_next
    n_group = sum((i - 1) // FAR_GROUP for i in range(1, n_tiles))
    far_unroll = _largest_divisor(n_group, (7, 5, 4, 3, 2, 1))
    shift = FAR_GROUP.bit_length() - 1

    def far_body(_, carry):
        i, t = carry
        s = snext_s[...]
        for _u in range(far_unroll):
            wrap = t + 1 >= jnp.right_shift(i - 1, shift)
            i_next, t_next = jnp.where(wrap, i + 1, i), jnp.where(wrap, 0, t + 1)
            t_next = jnp.where(i_next > last_tile, 0, t_next)
            i_next = jnp.minimum(i_next, last_tile)
            s_next = scores(i_next, FAR_GROUP * t_next, FAR_GROUP)
            apply(i, s, FAR_GROUP * t, FAR_GROUP, None)
            s, i, t = s_next, i_next, t_next
        snext_s[...] = s
        return i, t

    if n_group:
        snext_s[...] = scores(FAR_GROUP + 1, 0, FAR_GROUP)
        lax.fori_loop(0, n_group // far_unroll, far_body, (jnp.int32(FAR_GROUP + 1), jnp.int32(0)))

    lam = _lambda(l1q_ref, l1k_ref, l2q_ref, l2k_ref, lam_init)
    for i in range(n_tiles):
        acc = acc_s[i]
        o_ref[i * T:(i + 1) * T, :] = _diff_finish(acc[:, :hd], acc[:, hd:], lam, g_ref[...], lam_init).astype(BF16)


def _prompt_attn(rel_bias, q, k, v, lams, g_head, *, batch, seq, tile, lam_init):
    assert tile >= MAX_DISTANCE and tile % ATTN_CHUNK == 0 and seq % tile == 0
    nq = seq // tile
    hd = 2 * DA_DH
    lam_spec = _const_spec((1, DA_DH))
    seq_spec = pl.BlockSpec((seq, hd), lambda h, b: (b, h))
    feat_spec = pl.BlockSpec((1, hd, seq), lambda h, b: (b, h, 0))
    return pl.pallas_call(
        functools.partial(_prompt_attn_kernel, tile=tile, n_tiles=nq, lam_init=lam_init),
        grid=(DA_HEADS, batch),
        in_specs=[pl.BlockSpec(memory_space=pltpu.SMEM), seq_spec,
                  feat_spec, seq_spec,
                  lam_spec, lam_spec, lam_spec, lam_spec, _const_spec((1, hd))],
        out_specs=seq_spec,
        out_shape=jax.ShapeDtypeStruct((batch * seq, DA_Q), BF16),
        scratch_shapes=[pltpu.VMEM((2 * tile, 2 * tile), F32),
                        pltpu.VMEM((nq, 2 * tile, LANES), F32), pltpu.VMEM((nq, 2 * tile, 2 * hd), F32),
                        pltpu.VMEM((2 * tile, FAR_GROUP * tile), F32)],
        compiler_params=_params(2), name="diff_attn_prompt",
    )(rel_bias, q, k, v, *lams, g_head)


def _sample_attn_kernel(rb_ref, q_ref, kp_ref, vp_ref, kn_ref, vn_ref, l1q_ref, l1k_ref, l2q_ref, l2k_ref,
                        g_ref, o_ref, bias_p, bias_n, *, past, seq, lam_init):
    b = pl.program_id(0)
    hd = 2 * DA_DH
    dot = functools.partial(jnp.dot, preferred_element_type=F32)
    nt = lambda a, bb: lax.dot_general(a, bb, (((1,), (1,)), ((), ())), preferred_element_type=F32)

    @pl.when(b == 0)
    def _():
        qp = lax.broadcasted_iota(jnp.int32, (seq, past), 0) + past
        kp = lax.broadcasted_iota(jnp.int32, (seq, past), 1)
        qn = lax.broadcasted_iota(jnp.int32, (seq, seq), 0) + past
        kn = lax.broadcasted_iota(jnp.int32, (seq, seq), 1) + past
        for h in range(DA_HEADS):
            tile = _rel_bias_tile(rb_ref, h, qp, kp)
            bias_p[h] = jnp.concatenate([tile, tile], axis=0)
            tile = _rel_bias_tile(rb_ref, h, qn, kn)
            bias_n[h] = jnp.concatenate([tile, tile], axis=0)

    lam = _lambda(l1q_ref, l1k_ref, l2q_ref, l2k_ref, lam_init)
    v_heads = pltpu.einshape("khd->hkd", vp_ref[0])
    for h in range(DA_HEADS):
        cols = slice(h * hd, (h + 1) * hd)
        qs = _stack_maps(q_ref[:, cols])
        s_p = dot(qs, kp_ref[0, cols, :].astype(BF16)) + bias_p[h]
        s_n = nt(qs, kn_ref[:, cols]) + bias_n[h]
        m = jnp.maximum(jnp.max(s_p, axis=-1, keepdims=True), jnp.max(s_n, axis=-1, keepdims=True))
        p_p = jnp.exp2(s_p - m)
        p_n = jnp.exp2(s_n - m)
        l = jnp.sum(p_p, axis=-1, keepdims=True) + jnp.sum(p_n, axis=-1, keepdims=True)
        acc = (dot(p_p.astype(BF16), v_heads[h].astype(BF16))
               + dot(p_n.astype(BF16), vn_ref[:, cols]))
        o_ref[:, cols] = _diff_finish(acc, l, lam, g_ref[...], lam_init).astype(BF16)


def _sample_attn(rel_bias, q, cache_kt, cache_v, k_new, v_new, lams, g_head, *, batch, seq, lam_init):
    past = cache_kt.shape[2]
    hd = 2 * DA_DH
    lam_spec = _const_spec((1, DA_DH))
    new_spec = pl.BlockSpec((seq, DA_Q), lambda b: (b, 0))
    return pl.pallas_call(
        functools.partial(_sample_attn_kernel, past=past, seq=seq, lam_init=lam_init),
        grid=(batch,),
        in_specs=[pl.BlockSpec(memory_space=pltpu.SMEM), new_spec,
                  pl.BlockSpec((1, DA_Q, past), lambda b: (b, 0, 0)),
                  pl.BlockSpec((1, past, DA_HEADS, hd), lambda b: (b, 0, 0, 0)),
                  new_spec, new_spec, lam_spec, lam_spec, lam_spec, lam_spec, _const_spec((1, hd))],
        out_specs=new_spec,
        out_shape=jax.ShapeDtypeStruct((batch * seq, DA_Q), BF16),
        scratch_shapes=[pltpu.VMEM((DA_HEADS, 2 * seq, past), F32), pltpu.VMEM((DA_HEADS, 2 * seq, seq), F32)],
        compiler_params=_params(1), name="diff_attn_sample",
    )(rel_bias, q, cache_kt, cache_v, k_new, v_new, *lams, g_head)


def _prep_weights(norm_g, mlstm_w_in, mlstm_b_gate, mlstm_g_head, mlstm_w_out, kv_g, kv_w, diff_w_q,
                  diff_lam_q1, diff_lam_k1, diff_lam_q2, diff_lam_k2, diff_g_head, diff_w_o,
                  ffn_w_in, ffn_w_out):
    n_main = 2 * M_QK + M_V + D_MODEL
    w_in = mlstm_w_in[0]
    wg = jnp.pad(w_in[:, n_main:], ((0, 0), (0, LANES - 2 * M_HEADS)))
    wg_hi = wg.astype(BF16)
    wg_lo = (wg - wg_hi.astype(F32)).astype(BF16)
    row = lambda a: a.reshape(1, -1)
    return dict(
        norm=[[row(norm_g[l, j]) for j in range(4)] for l in range(2)],
        w_main=w_in.astype(BF16),
        w_gate=jnp.concatenate([wg_hi, wg_lo], axis=1),
        b_gate=jnp.pad(mlstm_b_gate[0], (0, LANES - 2 * M_HEADS)).reshape(1, LANES),
        g_mhead=row(mlstm_g_head[0]),
        w_mout=mlstm_w_out[0].astype(BF16),
        kv_g=row(kv_g), kv_w=kv_w.astype(BF16), w_q=diff_w_q[0].astype(BF16),
        lams=[row(diff_lam_q1[0]), row(diff_lam_k1[0]), row(diff_lam_q2[0]), row(diff_lam_k2[0])],
        g_dhead=row(diff_g_head[0]), w_o=diff_w_o[0].astype(BF16),
        ffn_in=ffn_w_in.astype(BF16), ffn_out=ffn_w_out.astype(BF16),
    )


def _trunk(x, c0, n0, m0, past_k, past_v, rel_bias, w, *, row_tile, ffn_tile, chunk, attn_tile):
    batch, seq, _ = x.shape
    rows = batch * seq
    x2d = x.reshape(rows, D_MODEL)
    lam_init = 0.8 - 0.6 * math.exp(-0.3 * 1)

    m0b = jnp.broadcast_to(m0[..., None], (batch, M_HEADS, LANES))
    x1, c_new, n_new, m_new = _mlstm(x2d, w["norm"][0][0], w["w_main"], w["w_gate"], w["b_gate"], c0, n0, m0b,
                                     w["w_mout"], w["g_mhead"], w["norm"][0][1], batch=batch, chunk=chunk)
    x2 = _ffn(x1, w["norm"][0][2], w["ffn_in"], w["ffn_out"], w["norm"][0][3], layer=0, tile=ffn_tile)

    aq, k_f32, v_f32, k_bf, v_bf = _qkv(x2, w["norm"][1][0], w["kv_g"], w["w_q"], w["kv_w"], tile=row_tile,
                                        batch=batch, k_transposed=past_k is None)
    if past_k is None:
        attn = _prompt_attn(rel_bias, aq, k_bf, v_bf, w["lams"], w["g_dhead"],
                            batch=batch, seq=seq, tile=attn_tile, lam_init=lam_init)
        k_f32 = k_f32.reshape(batch, DA_HEADS, 2, DA_DH, seq).transpose(0, 4, 1, 2, 3)
    else:
        past = past_k.shape[1]
        past_kt = past_k.transpose(0, 2, 3, 4, 1).reshape(batch, DA_Q, past)
        attn = _sample_attn(rel_bias, aq, past_kt, past_v, k_bf, v_bf, w["lams"], w["g_dhead"],
                            batch=batch, seq=seq, lam_init=lam_init)
    y = _ffn(x2, w["norm"][1][2], w["ffn_in"], w["ffn_out"], w["norm"][1][3], layer=1, tile=ffn_tile,
             attn=attn, w_o=w["w_o"], g_pre=w["norm"][1][1])

    return (y.reshape(batch, seq, D_MODEL), c_new[None], n_new[None], m_new[None, :, :, 0],
            k_f32.reshape(batch, seq, DA_HEADS, 2, DA_DH), v_f32.reshape(batch, seq, DA_HEADS, 2 * DA_DH))


def kernel(x_prompt, x_sample, state_C, state_n, state_m, cache_k, cache_v, norm_g, mlstm_w_in, mlstm_b_gate, mlstm_g_head, mlstm_w_out, kv_g, kv_w, rel_bias, diff_w_q, diff_lam_q1, diff_lam_k1, diff_lam_q2, diff_lam_k2, diff_g_head, diff_w_o, ffn_w_in, ffn_w_out):
    w = _prep_weights(norm_g, mlstm_w_in, mlstm_b_gate, mlstm_g_head, mlstm_w_out, kv_g, kv_w, diff_w_q,
                      diff_lam_q1, diff_lam_k1, diff_lam_q2, diff_lam_k2, diff_g_head, diff_w_o,
                      ffn_w_in, ffn_w_out)
    pb = x_prompt.shape[0]
    zeros = lambda *s: jnp.zeros(s, F32)
    y_p, p_c, p_n, p_m, p_k, p_v = _trunk(
        x_prompt, zeros(pb, M_HEADS, M_DV, M_DK), zeros(pb, M_HEADS, M_DK), zeros(pb, M_HEADS),
        None, None, rel_bias, w, row_tile=512, ffn_tile=1024, chunk=256, attn_tile=256)
    sb, ss, _ = x_sample.shape
    y_s, s_c, s_n, s_m, s_k, s_v = _trunk(
        x_sample, state_C[0], state_n[0], state_m[0], cache_k, cache_v, rel_bias, w,
        row_tile=sb * ss, ffn_tile=sb * ss, chunk=ss, attn_tile=None)
    return (y_p, y_s, p_c, p_n, p_m, p_k, p_v, s_c, s_n, s_m, s_k, s_v)
```

```python
import functools
import math

import jax
import jax.numpy as jnp
from jax import lax
from jax.experimental import pallas as pl
from jax.experimental.pallas import tpu as pltpu

F32 = jnp.float32
BF16 = jnp.bfloat16

D_MODEL = 1024
ATTN_CHUNK = 64
M_HEADS = 4
M_DK = D_MODEL // (2 * M_HEADS)
M_DV = D_MODEL // M_HEADS
M_QK = M_HEADS * M_DK
M_V = M_HEADS * M_DV
DA_HEADS = 8
DA_DH = D_MODEL // (2 * DA_HEADS)
DA_Q = DA_HEADS * 2 * DA_DH
FFN_HIDDEN = -(-8 * D_MODEL // (3 * 256)) * 256
N_BUCKETS = 32
MAX_DISTANCE = 128
EPS = 1e-6
LOG2E = math.log2(math.e)

LANES = 128
GATE_ROWS = 16
VMEM_LIMIT = 56 * 1024 * 1024
FAR_GROUP = 4
FFN_GROUP_ROWS = 512
FFN_COL_CHUNKS = (1024, 1024, 768)
assert sum(FFN_COL_CHUNKS) == FFN_HIDDEN


def _bucket_upper_bounds():
    half = N_BUCKETS // 2
    max_exact = half // 2
    ratio = MAX_DISTANCE // max_exact
    steps = half - max_exact
    bounds = [n + 1 for n in range(max_exact)]
    for k in range(1, steps):
        n = max_exact
        while n ** steps < (max_exact ** steps) * (ratio ** k):
            n += 1
        bounds.append(n)
    return bounds


_BUCKET_BOUNDS = _bucket_upper_bounds()


def _rms(x, g):
    return x * lax.rsqrt(jnp.mean(x * x, axis=-1, keepdims=True) + EPS) * g


def _log_sigmoid(x):
    return jnp.minimum(x, 0.0) - jnp.log1p(jnp.exp(-jnp.abs(x)))


def _split3(a):
    a1 = a.astype(BF16)
    r1 = a - a1.astype(F32)
    a2 = r1.astype(BF16)
    a3 = (r1 - a2.astype(F32)).astype(BF16)
    return a1, a2, a3


def _const_spec(shape):
    return pl.BlockSpec(shape, lambda *_: (0,) * len(shape), pipeline_mode=pl.Buffered(1))


def _params(n_axes):
    return pltpu.CompilerParams(dimension_semantics=("arbitrary",) * n_axes,
                                vmem_limit_bytes=VMEM_LIMIT)


def _mlstm_kernel(x_ref, xnext_ref, g_ref, w_ref, wg_ref, bg_ref, c0_ref, n0_ref, m0_ref,
                  wout_ref, ghead_ref, gpost_ref,
                  x1_ref, cout_ref, nout_ref, mout_ref,
                  q_s, k_s, v_s, o_s, gcol_s, grow_s, ct_s, m_s, hcat_s, *, chunk, chunks_per_step, n_steps):
    L = chunk
    c = pl.program_id(1)
    dot = functools.partial(jnp.dot, preferred_element_type=F32)

    def wide(rep, n):
        return rep[:, :n] if n < LANES else jnp.concatenate([rep] * (n // LANES), axis=1)

    def project(x, slot):
        xn = _rms(x, g_ref[...])
        xh = xn.astype(BF16)
        xl = (xn - xh.astype(F32)).astype(BF16)
        q_s[slot] = dot(xh, w_ref[:, :M_QK]).astype(BF16)
        k_s[slot] = (dot(xh, w_ref[:, M_QK:2 * M_QK]) * (M_DK ** -0.5)).astype(BF16)
        v_s[slot] = dot(xh, w_ref[:, 2 * M_QK:2 * M_QK + M_V]).astype(BF16)
        o_s[slot] = dot(xh, w_ref[:, 2 * M_QK + M_V:2 * M_QK + M_V + D_MODEL])
        gh = dot(xh, wg_ref[...])
        gl = dot(xl, wg_ref[:, :LANES])
        gates = gh[:, :LANES] + gh[:, LANES:] + gl + bg_ref[...]
        lane = lax.broadcasted_iota(jnp.int32, gates.shape, 1)
        gv = jnp.where(lane < M_HEADS, gates, _log_sigmoid(gates))
        gcol_s[slot] = gv
        if L < LANES:
            gv = jnp.concatenate([gv, jnp.zeros((LANES - L, LANES), F32)], axis=0)
        grow_s[slot] = gv.T[:GATE_ROWS, :L]

    row = lax.broadcasted_iota(jnp.int32, (L, L), 0)
    col = lax.broadcasted_iota(jnp.int32, (L, L), 1)
    causal = col <= row
    tri = jnp.where(causal, 1.0, 0.0).astype(BF16)
    tri_t = jnp.where(row <= col, 1.0, 0.0).astype(BF16)
    ones = jnp.ones((L, LANES), BF16)

    def scan(slot, rows):
        gcol = gcol_s[slot]
        grow = grow_s[slot]
        bcol = sum(dot(tri, part) for part in _split3(gcol))
        brow = sum(dot(part, tri_t) for part in _split3(grow))

        for h in range(M_HEADS):
            q = q_s[slot, :, h * M_DK:(h + 1) * M_DK]
            k = k_s[slot, :, h * M_DK:(h + 1) * M_DK]
            v = v_s[slot, :, h * M_DV:(h + 1) * M_DV]
            ig_row = grow[h:h + 1, :]
            b_row = brow[M_HEADS + h:M_HEADS + h + 1, :]
            ig = jnp.broadcast_to(gcol[:, h:h + 1], (L, LANES))
            b = jnp.broadcast_to(bcol[:, M_HEADS + h:M_HEADS + h + 1], (L, LANES))
            m0 = m_s[h:h + 1, :]
            ct = ct_s[h]

            d = jnp.where(causal, wide(b, L) - b_row + ig_row, -jnp.inf)
            g = b + m0
            m = jnp.maximum(g, jnp.max(d, axis=-1, keepdims=True))
            w_inter = jnp.exp(g - m)
            qk = lax.dot_general(q, k, (((1,), (1,)), ((), ())), preferred_element_type=F32)
            s = (jnp.exp(d - wide(m, L)) * qk).astype(BF16)
            inter = dot(q, ct.astype(BF16))
            num = wide(w_inter, M_DV) * inter[:, :M_DV] + dot(s, v)
            den = w_inter * inter[:, M_DV:] + dot(s, ones)
            inv = 1.0 / jnp.maximum(jnp.abs(den), jnp.exp(-m))
            hh = num * wide(inv, M_DV)

            m_last = m[L - 1:L, :]
            w_state = jnp.exp(g[L - 1:L, :] - m_last)
            w_rows = jnp.exp(b[L - 1:L, :] - b + ig - m_last)
            vw = jnp.concatenate([(v.astype(F32) * wide(w_rows, M_DV)).astype(BF16), w_rows.astype(BF16)], axis=1)
            upd = lax.dot_general(k, vw, (((0,), (0,)), ((), ())), preferred_element_type=F32)
            ct_s[h] = wide(w_state, M_DV + LANES) * ct + upd
            m_s[h:h + 1, :] = m_last

            hn = _rms(hh, ghead_ref[:, h * M_DV:(h + 1) * M_DV])
            gate = jax.nn.sigmoid(o_s[slot, :, h * M_DV:(h + 1) * M_DV])
            hcat_s[slot, :, h * M_DV:(h + 1) * M_DV] = (hn * gate).astype(BF16)

        y = dot(hcat_s[slot], wout_ref[...])
        x1_ref[0, rows, :] = x_ref[0, rows, :] + _rms(y, gpost_ref[...])

    @pl.when(c == 0)
    def _():
        for h in range(M_HEADS):
            ct_s[h, :, :M_DV] = c0_ref[0, h].T
            ct_s[h, :, M_DV:] = jnp.broadcast_to(n0_ref[0, h:h + 1, :], (LANES, M_DK)).T
        m_s[...] = m0_ref[0]
        project(x_ref[0, :L, :], 0)

    for j in range(chunks_per_step):
        if j + 1 < chunks_per_step:
            project(x_ref[0, (j + 1) * L:(j + 2) * L, :], (j + 1) % 2)
        scan(j % 2, slice(j * L, (j + 1) * L))
    if n_steps > 1:
        project(xnext_ref[0], chunks_per_step % 2)

    @pl.when(c == n_steps - 1)
    def _():
        for h in range(M_HEADS):
            t = ct_s[h].T
            cout_ref[0, h] = t[:M_DV]
            nout_ref[0, h:h + 1, :] = t[M_DV:M_DV + 1]
        mout_ref[0] = m_s[...]


def _mlstm(x2d, g_in, w_main, w_gate, b_gate, c0, n0, m0b, w_out, g_head, g_post, *, batch, chunk):
    rows = x2d.shape[0]
    seq = rows // batch
    n_chunks = seq // chunk
    cps = 2 if n_chunks % 2 == 0 else 1
    assert cps == 2 or n_chunks == 1
    n_steps = n_chunks // cps
    x3d = x2d.reshape(batch, seq, D_MODEL)
    step_spec = pl.BlockSpec((1, cps * chunk, D_MODEL), lambda b, c: (b, c, 0))
    next_spec = pl.BlockSpec((1, chunk, D_MODEL), lambda b, c: (b, jnp.minimum(cps * (c + 1), n_chunks - 1), 0))
    state4 = pl.BlockSpec((1, M_HEADS, M_DV, M_DK), lambda b, c: (b, 0, 0, 0))
    state3 = pl.BlockSpec((1, M_HEADS, LANES), lambda b, c: (b, 0, 0))
    vec = _const_spec((1, D_MODEL))
    x1, c_new, n_new, m_new = pl.pallas_call(
        functools.partial(_mlstm_kernel, chunk=chunk, chunks_per_step=cps, n_steps=n_steps),
        grid=(batch, n_steps),
        in_specs=[step_spec, next_spec, vec, _const_spec(w_main.shape), _const_spec(w_gate.shape),
                  _const_spec((1, LANES)), state4, state3, state3,
                  _const_spec((M_V, D_MODEL)), _const_spec((1, M_V)), vec],
        out_specs=[step_spec, state4, state3, state3],
        out_shape=[jax.ShapeDtypeStruct((batch, seq, D_MODEL), F32),
                   jax.ShapeDtypeStruct((batch, M_HEADS, M_DV, M_DK), F32),
                   jax.ShapeDtypeStruct((batch, M_HEADS, LANES), F32),
                   jax.ShapeDtypeStruct((batch, M_HEADS, LANES), F32)],
        scratch_shapes=[pltpu.VMEM((2, chunk, M_QK), BF16), pltpu.VMEM((2, chunk, M_QK), BF16),
                        pltpu.VMEM((2, chunk, M_V), BF16), pltpu.VMEM((2, chunk, D_MODEL), F32),
                        pltpu.VMEM((2, chunk, LANES), F32), pltpu.VMEM((2, GATE_ROWS, chunk), F32),
                        pltpu.VMEM((M_HEADS, M_DK, M_DV + LANES), F32), pltpu.VMEM((M_HEADS, LANES), F32),
                        pltpu.VMEM((2, chunk, M_V), BF16)],
        compiler_params=_params(2),
        name="mlstm",
    )(x3d, x3d, g_in, w_main, w_gate, b_gate, c0, n0, m0b, w_out, g_head, g_post)
    return x1.reshape(rows, D_MODEL), c_new, n_new, m_new


def _ffn_tile(load_x, gin_ref, win_ref, wout_ref, gout_ref, y_ref):
    dot = functools.partial(jnp.dot, preferred_element_type=F32)
    rows = y_ref.shape[0]
    n = rows // FFN_GROUP_ROWS if rows % FFN_GROUP_ROWS == 0 else 1
    groups = [slice(r * rows // n, (r + 1) * rows // n) for r in range(n)]

    def prologue(r):
        x = load_x(groups[r])
        return x, _rms(x, gin_ref[...]).astype(BF16)

    def epilogue(r, x, acc):
        y_ref[groups[r], :] = x + _rms(acc, gout_ref[...])

    x, xn = prologue(0)
    pending = None
    for r in range(n):
        acc = None
        start = 0
        nxt = None
        for j, width in enumerate(FFN_COL_CHUNKS):
            gate = dot(xn, win_ref[:, start:start + width])
            up = dot(xn, win_ref[:, FFN_HIDDEN + start:FFN_HIDDEN + start + width])
            act = (jax.nn.silu(gate) * up).astype(BF16)
            part = dot(act, wout_ref[start:start + width, :])
            acc = part if acc is None else acc + part
            start += width
            if j == 0:
                if pending is not None:
                    epilogue(*pending)
                if r + 1 < n:
                    nxt = prologue(r + 1)
        pending = (r, x, acc)
        if nxt is not None:
            x, xn = nxt
    epilogue(*pending)


def _ffn_kernel(x_ref, gin_ref, win_ref, wout_ref, gout_ref, y_ref):
    _ffn_tile(lambda rows: x_ref[rows, :], gin_ref, win_ref, wout_ref, gout_ref, y_ref)


def _attn_out_ffn_kernel(x_ref, a_ref, wo_ref, gpre_ref, gin_ref, win_ref, wout_ref, gout_ref, y_ref):
    def load_x(rows):
        mix = jnp.dot(a_ref[rows, :], wo_ref[...], preferred_element_type=F32)
        return x_ref[rows, :] + _rms(mix, gpre_ref[...])

    _ffn_tile(load_x, gin_ref, win_ref, wout_ref, gout_ref, y_ref)


def _ffn(x2d, g_in, w_in, w_out, g_out, *, layer, tile, attn=None, w_o=None, g_pre=None):
    rows = x2d.shape[0]
    row_spec = lambda cols: pl.BlockSpec((tile, cols), lambda i: (i, 0))
    vec = _const_spec((1, D_MODEL))
    layer_spec = lambda w: pl.BlockSpec((None,) + w.shape[1:], lambda i: (layer, 0, 0),
                                        pipeline_mode=pl.Buffered(1))
    ffn_specs = [vec, layer_spec(w_in), layer_spec(w_out), vec]
    if attn is None:
        kern, in_specs = _ffn_kernel, [row_spec(D_MODEL)] + ffn_specs
        args = (x2d, g_in, w_in, w_out, g_out)
    else:
        kern = _attn_out_ffn_kernel
        in_specs = [row_spec(D_MODEL), row_spec(DA_Q), _const_spec(w_o.shape), vec] + ffn_specs
        args = (x2d, attn, w_o, g_pre, g_in, w_in, w_out, g_out)
    return pl.pallas_call(
        kern, grid=(rows // tile,), in_specs=in_specs, out_specs=row_spec(D_MODEL),
        out_shape=jax.ShapeDtypeStruct((rows, D_MODEL), F32),
        compiler_params=_params(1), name="ffn",
    )(*args)


def _qkv_kernel(x_ref, gq_ref, gkv_ref, wq_ref, wkv_ref, q_ref, kf_ref, vf_ref, kb_ref, vb_ref, *, k_transposed):
    x = x_ref[...]
    xhat = x * lax.rsqrt(jnp.mean(x * x, axis=-1, keepdims=True) + EPS)
    xq = (xhat * gq_ref[...]).astype(BF16)
    xkv = (xhat * gkv_ref[...]).astype(BF16)
    dot = functools.partial(jnp.dot, preferred_element_type=F32)
    q_ref[...] = (dot(xq, wq_ref[...]) * (DA_DH ** -0.5 * LOG2E)).astype(BF16)
    kk = dot(xkv, wkv_ref[:, :DA_Q])
    if k_transposed:
        kt = kk.T
        kf_ref[0] = kt
        kb_ref[0] = kt.astype(BF16)
    else:
        kf_ref[...] = kk
        kb_ref[...] = kk.astype(BF16)
    vv = dot(xkv, wkv_ref[:, DA_Q:])
    vf_ref[...] = vv
    vb_ref[...] = vv.astype(BF16)


def _qkv(x2d, g_q, g_kv, w_q, w_kv, *, tile, batch, k_transposed):
    rows = x2d.shape[0]
    seq = rows // batch
    row_spec = pl.BlockSpec((tile, D_MODEL), lambda i: (i, 0))
    vec = _const_spec((1, D_MODEL))
    shp = lambda dt: jax.ShapeDtypeStruct((rows, DA_Q), dt)
    if k_transposed:
        tiles_per_seq = seq // tile
        k_spec = pl.BlockSpec((1, DA_Q, tile), lambda i: (i // tiles_per_seq, 0, i % tiles_per_seq))
        k_shp = lambda dt: jax.ShapeDtypeStruct((batch, DA_Q, seq), dt)
    else:
        k_spec, k_shp = row_spec, shp
    return pl.pallas_call(
        functools.partial(_qkv_kernel, k_transposed=k_transposed), grid=(rows // tile,),
        in_specs=[row_spec, vec, vec, _const_spec(w_q.shape), _const_spec(w_kv.shape)],
        out_specs=[row_spec, k_spec, row_spec, k_spec, row_spec],
        out_shape=[shp(BF16), k_shp(F32), shp(F32), k_shp(BF16), shp(BF16)],
        compiler_params=_params(1), name="attn_qkv",
    )(x2d, g_q, g_kv, w_q, w_kv)


def _rel_bias_tile(rb_ref, head, q_pos, k_pos):
    rel = k_pos - q_pos
    n = jnp.abs(rel)
    half = N_BUCKETS // 2

    def table(offset):
        val = jnp.full(rel.shape, rb_ref[offset + half - 1, head], F32)
        for bucket in reversed(range(half - 1)):
            val = jnp.where(n < _BUCKET_BOUNDS[bucket], rb_ref[offset + bucket, head], val)
        return val

    bias = jnp.where(rel > 0, table(half), table(0)) * LOG2E
    shift = ATTN_CHUNK.bit_length() - 1
    visible = jnp.right_shift(k_pos, shift) <= jnp.right_shift(q_pos, shift)
    return jnp.where(visible, bias, -jnp.inf)


def _lambda(l1q_ref, l1k_ref, l2q_ref, l2k_ref, lam_init):
    a = jnp.exp(jnp.sum(l1q_ref[...] * l1k_ref[...], axis=-1, keepdims=True))
    b = jnp.exp(jnp.sum(l2q_ref[...] * l2k_ref[...], axis=-1, keepdims=True))
    return a - b + lam_init


def _stack_maps(q):
    lane = lax.broadcasted_iota(jnp.int32, q.shape, 1)
    zero = jnp.zeros_like(q)
    return jnp.concatenate([jnp.where(lane < DA_DH, q, zero), jnp.where(lane >= DA_DH, q, zero)], axis=0)


def _diff_finish(acc, l, lam, g, lam_init):
    t = acc.shape[0] // 2
    o = acc[:t] / l[:t] - lam * (acc[t:] / l[t:])
    return _rms(o, g) * (1.0 - lam_init)


def _largest_divisor(n, candidates):
    return max(u for u in candidates if n % u == 0)


def _prompt_attn_kernel(rb_ref, q_ref, k_ref, v_ref, l1q_ref, l1k_ref, l2q_ref, l2k_ref, g_ref,
                        o_ref, bias_s, m_s, acc_s, snext_s, *, tile, n_tiles, lam_init):
    T = tile
    hd = 2 * DA_DH
    h = pl.program_id(0)
    b = pl.program_id(1)
    dot = functools.partial(jnp.dot, preferred_element_type=F32)

    @pl.when(b == 0)
    def _():
        qp = lax.broadcasted_iota(jnp.int32, (T, 2 * T), 0) + T
        kp = lax.broadcasted_iota(jnp.int32, (T, 2 * T), 1)
        near = _rel_bias_tile(rb_ref, h, qp, kp)
        bias_s[...] = jnp.concatenate([near, near], axis=0)

    far_bias = rb_ref[N_BUCKETS // 2 - 1, h] * LOG2E
    lane = lax.broadcasted_iota(jnp.int32, (T, hd), 1)

    def update(i, key_block, n_blocks, bias):
        apply(i, scores(i, key_block, n_blocks), key_block, n_blocks, bias)

    def scores(i, key_block, n_blocks):
        q = q_ref[pl.ds(pl.multiple_of(i * T, T), T), :]
        kb = k_ref[0, :, pl.ds(pl.multiple_of(key_block * T, T), n_blocks * T)]
        zero = jnp.zeros_like(q)
        return jnp.concatenate([dot(jnp.where(lane < DA_DH, q, zero), kb),
                                dot(jnp.where(lane >= DA_DH, q, zero), kb)], axis=0)

    def apply(i, s, key_block, n_blocks, bias):
        vb = v_ref[pl.ds(pl.multiple_of(key_block * T, T), n_blocks * T), :]
        vext = jnp.concatenate([vb, jnp.ones_like(vb)], axis=1)
        first = bias is not None
        if first:
            s = s + bias
            m_new = jnp.broadcast_to(jnp.max(s, axis=-1, keepdims=True), (2 * T, LANES))
        else:
            m_prev = m_s[i]
            m_new = jnp.maximum(m_prev, jnp.max(s, axis=-1, keepdims=True))
        p = jnp.exp2(s - jnp.concatenate([m_new] * (n_blocks * T // LANES), axis=1))
        pv = jnp.dot(p.astype(BF16), vext, preferred_element_type=F32)
        if first:
            acc_s[i] = pv
            m_s[i] = m_new - far_bias
        else:
            alpha = jnp.exp2(m_prev - m_new)
            acc_s[i] = jnp.concatenate([alpha, alpha], axis=1) * acc_s[i] + pv
            m_s[i] = m_new

    update(0, 0, 1, bias_s[:, T:])
    near_unroll = _largest_divisor(n_tiles - 1, (5, 3, 2, 1))

    last_tile = n_tiles - 1
    if n_tiles > 1:
        snext_s[:, :2 * T] = scores(1, 0, 2)

    def near_body(it, carry):
        s = snext_s[:, :2 * T]
        for u in range(near_unroll):
            i = 1 + it * near_unroll + u
            i_next = jnp.minimum(i + 1, last_tile)
            s_next = scores(i_next, i_next - 1, 2)
            apply(i, s, i - 1, 2, bias_s[...])
            s = s_next
        snext_s[:, :2 * T] = s
        return carry

    lax.fori_loop(0, (n_tiles - 1) // near_unroll, near_body, 0)

    rest = []
    for i in range(2, n_tiles):
        n_far = i - 1
        if n_far % 2:
            rest.append((i, n_far - 1, 1))
        if n_far % FAR_GROUP >= 2:
            rest.append((i, n_far - n_far % FAR_GROUP, 2))
    s = scores(*rest[0]) if rest else None
    for idx, (i, key_block, n_blocks) in enumerate(rest):
        s_next = scores(*rest[idx + 1]) if idx + 1 < len(rest) else None
        apply(i, s, key_block, n_blocks, None)
        s = s_next
    n_group = sum((i - 1) // FAR_GROUP for i in range(1, n_tiles))
    far_unroll = _largest_divisor(n_group, (7, 5, 4, 3, 2, 1))
    shift = FAR_GROUP.bit_length() - 1

    def far_body(_, carry):
        i, t = carry
        s = snext_s[...]
        for _u in range(far_unroll):
            wrap = t + 1 >= jnp.right_shift(i - 1, shift)
            i_next, t_next = jnp.where(wrap, i + 1, i), jnp.where(wrap, 0, t + 1)
            t_next = jnp.where(i_next > last_tile, 0, t_next)
            i_next = jnp.minimum(i_next, last_tile)
            s_next = scores(i_next, FAR_GROUP * t_next, FAR_GROUP)
            apply(i, s, FAR_GROUP * t, FAR_GROUP, None)
            s, i, t = s_next, i_next, t_next
        snext_s[...] = s
        return i, t

    if n_group:
        snext_s[...] = scores(FAR_GROUP + 1, 0, FAR_GROUP)
        lax.fori_loop(0, n_group // far_unroll, far_body, (jnp.int32(FAR_GROUP + 1), jnp.int32(0)))

    lam = _lambda(l1q_ref, l1k_ref, l2q_ref, l2k_ref, lam_init)
    for i in range(n_tiles):
        acc = acc_s[i]
        o_ref[i * T:(i + 1) * T, :] = _diff_finish(acc[:, :hd], acc[:, hd:], lam, g_ref[...], lam_init).astype(BF16)


def _prompt_attn(rel_bias, q, k, v, lams, g_head, *, batch, seq, tile, lam_init):
    assert tile >= MAX_DISTANCE and tile % ATTN_CHUNK == 0 and seq % tile == 0
    nq = seq // tile
    hd = 2 * DA_DH
    lam_spec = _const_spec((1, DA_DH))
    seq_spec = pl.BlockSpec((seq, hd), lambda h, b: (b, h))
    feat_spec = pl.BlockSpec((1, hd, seq), lambda h, b: (b, h, 0))
    return pl.pallas_call(
        functools.partial(_prompt_attn_kernel, tile=tile, n_tiles=nq, lam_init=lam_init),
        grid=(DA_HEADS, batch),
        in_specs=[pl.BlockSpec(memory_space=pltpu.SMEM), seq_spec,
                  feat_spec, seq_spec,
                  lam_spec, lam_spec, lam_spec, lam_spec, _const_spec((1, hd))],
        out_specs=seq_spec,
        out_shape=jax.ShapeDtypeStruct((batch * seq, DA_Q), BF16),
        scratch_shapes=[pltpu.VMEM((2 * tile, 2 * tile), F32),
                        pltpu.VMEM((nq, 2 * tile, LANES), F32), pltpu.VMEM((nq, 2 * tile, 2 * hd), F32),
                        pltpu.VMEM((2 * tile, FAR_GROUP * tile), F32)],
        compiler_params=_params(2), name="diff_attn_prompt",
    )(rel_bias, q, k, v, *lams, g_head)


def _sample_attn_kernel(rb_ref, q_ref, kp_ref, vp_ref, kn_ref, vn_ref, l1q_ref, l1k_ref, l2q_ref, l2k_ref,
                        g_ref, o_ref, bias_p, bias_n, *, past, seq, lam_init):
    b = pl.program_id(0)
    hd = 2 * DA_DH
    dot = functools.partial(jnp.dot, preferred_element_type=F32)
    nt = lambda a, bb: lax.dot_general(a, bb, (((1,), (1,)), ((), ())), preferred_element_type=F32)

    @pl.when(b == 0)
    def _():
        qp = lax.broadcasted_iota(jnp.int32, (seq, past), 0) + past
        kp = lax.broadcasted_iota(jnp.int32, (seq, past), 1)
        qn = lax.broadcasted_iota(jnp.int32, (seq, seq), 0) + past
        kn = lax.broadcasted_iota(jnp.int32, (seq, seq), 1) + past
        for h in range(DA_HEADS):
            tile = _rel_bias_tile(rb_ref, h, qp, kp)
            bias_p[h] = jnp.concatenate([tile, tile], axis=0)
            tile = _rel_bias_tile(rb_ref, h, qn, kn)
            bias_n[h] = jnp.concatenate([tile, tile], axis=0)

    lam = _lambda(l1q_ref, l1k_ref, l2q_ref, l2k_ref, lam_init)
    v_heads = jnp.transpose(vp_ref[0], (1, 0, 2))
    for h in range(DA_HEADS):
        cols = slice(h * hd, (h + 1) * hd)
        qs = _stack_maps(q_ref[:, cols])
        s_p = dot(qs, kp_ref[0, cols, :].astype(BF16)) + bias_p[h]
        s_n = nt(qs, kn_ref[:, cols]) + bias_n[h]
        m = jnp.maximum(jnp.max(s_p, axis=-1, keepdims=True), jnp.max(s_n, axis=-1, keepdims=True))
        p_p = jnp.exp2(s_p - m)
        p_n = jnp.exp2(s_n - m)
        l = jnp.sum(p_p, axis=-1, keepdims=True) + jnp.sum(p_n, axis=-1, keepdims=True)
        acc = (dot(p_p.astype(BF16), v_heads[h].astype(BF16))
               + dot(p_n.astype(BF16), vn_ref[:, cols]))
        o_ref[:, cols] = _diff_finish(acc, l, lam, g_ref[...], lam_init).astype(BF16)


def _sample_attn(rel_bias, q, cache_kt, cache_v, k_new, v_new, lams, g_head, *, batch, seq, lam_init):
    past = cache_kt.shape[2]
    hd = 2 * DA_DH
    lam_spec = _const_spec((1, DA_DH))
    new_spec = pl.BlockSpec((seq, DA_Q), lambda b: (b, 0))
    return pl.pallas_call(
        functools.partial(_sample_attn_kernel, past=past, seq=seq, lam_init=lam_init),
        grid=(batch,),
        in_specs=[pl.BlockSpec(memory_space=pltpu.SMEM), new_spec,
                  pl.BlockSpec((1, DA_Q, past), lambda b: (b, 0, 0)),
                  pl.BlockSpec((1, past, DA_HEADS, hd), lambda b: (b, 0, 0, 0)),
                  new_spec, new_spec, lam_spec, lam_spec, lam_spec, lam_spec, _const_spec((1, hd))],
        out_specs=new_spec,
        out_shape=jax.ShapeDtypeStruct((batch * seq, DA_Q), BF16),
        scratch_shapes=[pltpu.VMEM((DA_HEADS, 2 * seq, past), F32), pltpu.VMEM((DA_HEADS, 2 * seq, seq), F32)],
        compiler_params=_params(1), name="diff_attn_sample",
    )(rel_bias, q, cache_kt, cache_v, k_new, v_new, *lams, g_head)


def _prep_weights(norm_g, mlstm_w_in, mlstm_b_gate, mlstm_g_head, mlstm_w_out, kv_g, kv_w, diff_w_q,
                  diff_lam_q1, diff_lam_k1, diff_lam_q2, diff_lam_k2, diff_g_head, diff_w_o,
                  ffn_w_in, ffn_w_out):
    n_main = 2 * M_QK + M_V + D_MODEL
    w_in = mlstm_w_in[0]
    wg = jnp.pad(w_in[:, n_main:], ((0, 0), (0, LANES - 2 * M_HEADS)))
    wg_hi = wg.astype(BF16)
    wg_lo = (wg - wg_hi.astype(F32)).astype(BF16)
    row = lambda a: a.reshape(1, -1)
    return dict(
        norm=[[row(norm_g[l, j]) for j in range(4)] for l in range(2)],
        w_main=w_in.astype(BF16),
        w_gate=jnp.concatenate([wg_hi, wg_lo], axis=1),
        b_gate=jnp.pad(mlstm_b_gate[0], (0, LANES - 2 * M_HEADS)).reshape(1, LANES),
        g_mhead=row(mlstm_g_head[0]),
        w_mout=mlstm_w_out[0].astype(BF16),
        kv_g=row(kv_g), kv_w=kv_w.astype(BF16), w_q=diff_w_q[0].astype(BF16),
        lams=[row(diff_lam_q1[0]), row(diff_lam_k1[0]), row(diff_lam_q2[0]), row(diff_lam_k2[0])],
        g_dhead=row(diff_g_head[0]), w_o=diff_w_o[0].astype(BF16),
        ffn_in=ffn_w_in.astype(BF16), ffn_out=ffn_w_out.astype(BF16),
    )


def _trunk(x, c0, n0, m0, past_k, past_v, rel_bias, w, *, row_tile, ffn_tile, chunk, attn_tile):
    batch, seq, _ = x.shape
    rows = batch * seq
    x2d = x.reshape(rows, D_MODEL)
    lam_init = 0.8 - 0.6 * math.exp(-0.3 * 1)

    m0b = jnp.broadcast_to(m0[..., None], (batch, M_HEADS, LANES))
    x1, c_new, n_new, m_new = _mlstm(x2d, w["norm"][0][0], w["w_main"], w["w_gate"], w["b_gate"], c0, n0, m0b,
                                     w["w_mout"], w["g_mhead"], w["norm"][0][1], batch=batch, chunk=chunk)
    x2 = _ffn(x1, w["norm"][0][2], w["ffn_in"], w["ffn_out"], w["norm"][0][3], layer=0, tile=ffn_tile)

    aq, k_f32, v_f32, k_bf, v_bf = _qkv(x2, w["norm"][1][0], w["kv_g"], w["w_q"], w["kv_w"], tile=row_tile,
                                        batch=batch, k_transposed=past_k is None)
    if past_k is None:
        attn = _prompt_attn(rel_bias, aq, k_bf, v_bf, w["lams"], w["g_dhead"],
                            batch=batch, seq=seq, tile=attn_tile, lam_init=lam_init)
        k_f32 = k_f32.reshape(batch, DA_HEADS, 2, DA_DH, seq).transpose(0, 4, 1, 2, 3)
    else:
        past = past_k.shape[1]
        past_kt = past_k.transpose(0, 2, 3, 4, 1).reshape(batch, DA_Q, past)
        attn = _sample_attn(rel_bias, aq, past_kt, past_v, k_bf, v_bf, w["lams"], w["g_dhead"],
                            batch=batch, seq=seq, lam_init=lam_init)
    y = _ffn(x2, w["norm"][1][2], w["ffn_in"], w["ffn_out"], w["norm"][1][3], layer=1, tile=ffn_tile,
             attn=attn, w_o=w["w_o"], g_pre=w["norm"][1][1])

    return (y.reshape(batch, seq, D_MODEL), c_new[None], n_new[None], m_new[None, :, :, 0],
            k_f32.reshape(batch, seq, DA_HEADS, 2, DA_DH), v_f32.reshape(batch, seq, DA_HEADS, 2 * DA_DH))


def kernel(x_prompt, x_sample, state_C, state_n, state_m, cache_k, cache_v, norm_g, mlstm_w_in, mlstm_b_gate, mlstm_g_head, mlstm_w_out, kv_g, kv_w, rel_bias, diff_w_q, diff_lam_q1, diff_lam_k1, diff_lam_q2, diff_lam_k2, diff_g_head, diff_w_o, ffn_w_in, ffn_w_out):
    w = _prep_weights(norm_g, mlstm_w_in, mlstm_b_gate, mlstm_g_head, mlstm_w_out, kv_g, kv_w, diff_w_q,
                      diff_lam_q1, diff_lam_k1, diff_lam_q2, diff_lam_k2, diff_g_head, diff_w_o,
                      ffn_w_in, ffn_w_out)
    pb = x_prompt.shape[0]
    zeros = lambda *s: jnp.zeros(s, F32)
    y_p, p_c, p_n, p_m, p_k, p_v = _trunk(
        x_prompt, zeros(pb, M_HEADS, M_DV, M_DK), zeros(pb, M_HEADS, M_DK), zeros(pb, M_HEADS),
        None, None, rel_bias, w, row_tile=512, ffn_tile=1024, chunk=256, attn_tile=256)
    sb, ss, _ = x_sample.shape
    y_s, s_c, s_n, s_m, s_k, s_v = _trunk(
        x_sample, state_C[0], state_n[0], state_m[0], cache_k, cache_v, rel_bias, w,
        row_tile=sb * ss, ffn_tile=sb * ss, chunk=ss, attn_tile=None)
    return (y_p, y_s, p_c, p_n, p_m, p_k, p_v, s_c, s_n, s_m, s_k, s_v)
```

```python
import functools
import math

import jax
import jax.numpy as jnp
from jax import lax
from jax.experimental import pallas as pl
from jax.experimental.pallas import tpu as pltpu

F32 = jnp.float32
BF16 = jnp.bfloat16

D_MODEL = 1024
ATTN_CHUNK = 64
M_HEADS = 4
M_DK = D_MODEL // (2 * M_HEADS)
M_DV = D_MODEL // M_HEADS
M_QK = M_HEADS * M_DK
M_V = M_HEADS * M_DV
DA_HEADS = 8
DA_DH = D_MODEL // (2 * DA_HEADS)
DA_Q = DA_HEADS * 2 * DA_DH
FFN_HIDDEN = -(-8 * D_MODEL // (3 * 256)) * 256
N_BUCKETS = 32
MAX_DISTANCE = 128
EPS = 1e-6
LOG2E = math.log2(math.e)

LANES = 128
GATE_ROWS = 16
VMEM_LIMIT = 56 * 1024 * 1024
FAR_GROUP = 4
FFN_GROUP_ROWS = 512
FFN_COL_CHUNKS = (1024, 1024, 768)
assert sum(FFN_COL_CHUNKS) == FFN_HIDDEN


def _bucket_upper_bounds():
    half = N_BUCKETS // 2
    max_exact = half // 2
    ratio = MAX_DISTANCE // max_exact
    steps = half - max_exact
    bounds = [n + 1 for n in range(max_exact)]
    for k in range(1, steps):
        n = max_exact
        while n ** steps < (max_exact ** steps) * (ratio ** k):
            n += 1
        bounds.append(n)
    return bounds


_BUCKET_BOUNDS = _bucket_upper_bounds()


def _rms(x, g):
    return x * lax.rsqrt(jnp.mean(x * x, axis=-1, keepdims=True) + EPS) * g


def _log_sigmoid(x):
    return jnp.minimum(x, 0.0) - jnp.log1p(jnp.exp(-jnp.abs(x)))


def _split3(a):
    a1 = a.astype(BF16)
    r1 = a - a1.astype(F32)
    a2 = r1.astype(BF16)
    a3 = (r1 - a2.astype(F32)).astype(BF16)
    return a1, a2, a3


def _const_spec(shape):
    return pl.BlockSpec(shape, lambda *_: (0,) * len(shape), pipeline_mode=pl.Buffered(1))


def _params(n_axes):
    return pltpu.CompilerParams(dimension_semantics=("arbitrary",) * n_axes,
                                vmem_limit_bytes=VMEM_LIMIT)


def _mlstm_kernel(x_ref, xnext_ref, g_ref, w_ref, wg_ref, bg_ref, c0_ref, n0_ref, m0_ref,
                  wout_ref, ghead_ref, gpost_ref,
                  x1_ref, cout_ref, nout_ref, mout_ref,
                  q_s, k_s, v_s, o_s, gcol_s, grow_s, ct_s, m_s, hcat_s, *, chunk, chunks_per_step, n_steps):
    L = chunk
    c = pl.program_id(1)
    dot = functools.partial(jnp.dot, preferred_element_type=F32)

    def wide(rep, n):
        return rep[:, :n] if n < LANES else jnp.concatenate([rep] * (n // LANES), axis=1)

    def projection(load_x, slot):
        xs = {}

        def queries():
            xn = _rms(load_x(), g_ref[...])
            xs["hi"] = xn.astype(BF16)
            xs["lo"] = (xn - xs["hi"].astype(F32)).astype(BF16)
            q_s[slot] = dot(xs["hi"], w_ref[:, :M_QK]).astype(BF16)
            k_s[slot] = (dot(xs["hi"], w_ref[:, M_QK:2 * M_QK]) * (M_DK ** -0.5)).astype(BF16)

        def values():
            v_s[slot] = dot(xs["hi"], w_ref[:, 2 * M_QK:2 * M_QK + M_V]).astype(BF16)

        def out_gate():
            o_s[slot] = dot(xs["hi"], w_ref[:, 2 * M_QK + M_V:2 * M_QK + M_V + D_MODEL])

        def gates():
            gh = dot(xs["hi"], wg_ref[...])
            gl = dot(xs["lo"], wg_ref[:, :LANES])
            pre = gh[:, :LANES] + gh[:, LANES:] + gl + bg_ref[...]
            lane = lax.broadcasted_iota(jnp.int32, pre.shape, 1)
            gv = jnp.where(lane < M_HEADS, pre, _log_sigmoid(pre))
            gcol_s[slot] = gv
            if L < LANES:
                gv = jnp.concatenate([gv, jnp.zeros((LANES - L, LANES), F32)], axis=0)
            grow_s[slot] = gv.T[:GATE_ROWS, :L]

        return [queries, values, out_gate, gates]

    row = lax.broadcasted_iota(jnp.int32, (L, L), 0)
    col = lax.broadcasted_iota(jnp.int32, (L, L), 1)
    causal = col <= row
    tri = jnp.where(causal, 1.0, 0.0).astype(BF16)
    tri_t = jnp.where(row <= col, 1.0, 0.0).astype(BF16)
    ones = jnp.ones((L, LANES), BF16)

    def scan(slot, rows, between_heads):
        gcol = gcol_s[slot]
        grow = grow_s[slot]
        bcol = sum(dot(tri, part) for part in _split3(gcol))
        brow = sum(dot(part, tri_t) for part in _split3(grow))

        for h in range(M_HEADS):
            q = q_s[slot, :, h * M_DK:(h + 1) * M_DK]
            k = k_s[slot, :, h * M_DK:(h + 1) * M_DK]
            v = v_s[slot, :, h * M_DV:(h + 1) * M_DV]
            ig_row = grow[h:h + 1, :]
            b_row = brow[M_HEADS + h:M_HEADS + h + 1, :]
            ig = jnp.broadcast_to(gcol[:, h:h + 1], (L, LANES))
            b = jnp.broadcast_to(bcol[:, M_HEADS + h:M_HEADS + h + 1], (L, LANES))
            m0 = m_s[h:h + 1, :]
            ct = ct_s[h]

            d = jnp.where(causal, wide(b, L) - b_row + ig_row, -jnp.inf)
            g = b + m0
            m = jnp.maximum(g, jnp.max(d, axis=-1, keepdims=True))
            w_inter = jnp.exp(g - m)
            qk = lax.dot_general(q, k, (((1,), (1,)), ((), ())), preferred_element_type=F32)
            s = (jnp.exp(d - wide(m, L)) * qk).astype(BF16)
            inter = dot(q, ct.astype(BF16))
            num = wide(w_inter, M_DV) * inter[:, :M_DV] + dot(s, v)
            den = w_inter * inter[:, M_DV:] + dot(s, ones)
            inv = 1.0 / jnp.maximum(jnp.abs(den), jnp.exp(-m))
            hh = num * wide(inv, M_DV)

            m_last = m[L - 1:L, :]
            w_state = jnp.exp(g[L - 1:L, :] - m_last)
            w_rows = jnp.exp(b[L - 1:L, :] - b + ig - m_last)
            vw = jnp.concatenate([(v.astype(F32) * wide(w_rows, M_DV)).astype(BF16), w_rows.astype(BF16)], axis=1)
            upd = lax.dot_general(k, vw, (((0,), (0,)), ((), ())), preferred_element_type=F32)
            ct_s[h] = wide(w_state, M_DV + LANES) * ct + upd
            m_s[h:h + 1, :] = m_last

            hn = _rms(hh, ghead_ref[:, h * M_DV:(h + 1) * M_DV])
            gate = jax.nn.sigmoid(o_s[slot, :, h * M_DV:(h + 1) * M_DV])
            hcat_s[slot, :, h * M_DV:(h + 1) * M_DV] = (hn * gate).astype(BF16)
            if h < len(between_heads):
                between_heads[h]()

        y = dot(hcat_s[slot], wout_ref[...])
        x1_ref[0, rows, :] = x_ref[0, rows, :] + _rms(y, gpost_ref[...])

    @pl.when(c == 0)
    def _():
        for h in range(M_HEADS):
            ct_s[h, :, :M_DV] = c0_ref[0, h].T
            ct_s[h, :, M_DV:] = jnp.broadcast_to(n0_ref[0, h:h + 1, :], (LANES, M_DK)).T
        m_s[...] = m0_ref[0]
        for step in projection(lambda: x_ref[0, :L, :], 0):
            step()

    for j in range(chunks_per_step):
        rows = slice(j * L, (j + 1) * L)
        if j + 1 < chunks_per_step:
            following = projection(lambda j=j: x_ref[0, (j + 1) * L:(j + 2) * L, :], (j + 1) % 2)
        elif n_steps > 1:
            following = projection(lambda: xnext_ref[0], (j + 1) % 2)
        else:
            following = []
        scan(j % 2, rows, following)

    @pl.when(c == n_steps - 1)
    def _():
        for h in range(M_HEADS):
            t = ct_s[h].T
            cout_ref[0, h] = t[:M_DV]
            nout_ref[0, h:h + 1, :] = t[M_DV:M_DV + 1]
        mout_ref[0] = m_s[...]


def _mlstm(x2d, g_in, w_main, w_gate, b_gate, c0, n0, m0b, w_out, g_head, g_post, *, batch, chunk):
    rows = x2d.shape[0]
    seq = rows // batch
    n_chunks = seq // chunk
    cps = 2 if n_chunks % 2 == 0 else 1
    assert cps == 2 or n_chunks == 1
    n_steps = n_chunks // cps
    x3d = x2d.reshape(batch, seq, D_MODEL)
    step_spec = pl.BlockSpec((1, cps * chunk, D_MODEL), lambda b, c: (b, c, 0))
    next_spec = pl.BlockSpec((1, chunk, D_MODEL), lambda b, c: (b, jnp.minimum(cps * (c + 1), n_chunks - 1), 0))
    state4 = pl.BlockSpec((1, M_HEADS, M_DV, M_DK), lambda b, c: (b, 0, 0, 0))
    state3 = pl.BlockSpec((1, M_HEADS, LANES), lambda b, c: (b, 0, 0))
    vec = _const_spec((1, D_MODEL))
    x1, c_new, n_new, m_new = pl.pallas_call(
        functools.partial(_mlstm_kernel, chunk=chunk, chunks_per_step=cps, n_steps=n_steps),
        grid=(batch, n_steps),
        in_specs=[step_spec, next_spec, vec, _const_spec(w_main.shape), _const_spec(w_gate.shape),
                  _const_spec((1, LANES)), state4, state3, state3,
                  _const_spec((M_V, D_MODEL)), _const_spec((1, M_V)), vec],
        out_specs=[step_spec, state4, state3, state3],
        out_shape=[jax.ShapeDtypeStruct((batch, seq, D_MODEL), F32),
                   jax.ShapeDtypeStruct((batch, M_HEADS, M_DV, M_DK), F32),
                   jax.ShapeDtypeStruct((batch, M_HEADS, LANES), F32),
                   jax.ShapeDtypeStruct((batch, M_HEADS, LANES), F32)],
        scratch_shapes=[pltpu.VMEM((2, chunk, M_QK), BF16), pltpu.VMEM((2, chunk, M_QK), BF16),
                        pltpu.VMEM((2, chunk, M_V), BF16), pltpu.VMEM((2, chunk, D_MODEL), F32),
                        pltpu.VMEM((2, chunk, LANES), F32), pltpu.VMEM((2, GATE_ROWS, chunk), F32),
                        pltpu.VMEM((M_HEADS, M_DK, M_DV + LANES), F32), pltpu.VMEM((M_HEADS, LANES), F32),
                        pltpu.VMEM((2, chunk, M_V), BF16)],
        compiler_params=_params(2),
        name="mlstm",
    )(x3d, x3d, g_in, w_main, w_gate, b_gate, c0, n0, m0b, w_out, g_head, g_post)
    return x1.reshape(rows, D_MODEL), c_new, n_new, m_new


def _ffn_tile(load_x, gin_ref, win_ref, wout_ref, gout_ref, y_ref):
    dot = functools.partial(jnp.dot, preferred_element_type=F32)
    rows = y_ref.shape[0]
    n = rows // FFN_GROUP_ROWS if rows % FFN_GROUP_ROWS == 0 else 1
    groups = [slice(r * rows // n, (r + 1) * rows // n) for r in range(n)]

    def prologue(r):
        x = load_x(groups[r])
        return x, _rms(x, gin_ref[...]).astype(BF16)

    def epilogue(r, x, acc):
        y_ref[groups[r], :] = x + _rms(acc, gout_ref[...])

    x, xn = prologue(0)
    pending = None
    for r in range(n):
        acc = None
        start = 0
        nxt = None
        for j, width in enumerate(FFN_COL_CHUNKS):
            gate = dot(xn, win_ref[:, start:start + width])
            up = dot(xn, win_ref[:, FFN_HIDDEN + start:FFN_HIDDEN + start + width])
            act = (jax.nn.silu(gate) * up).astype(BF16)
            part = dot(act, wout_ref[start:start + width, :])
            acc = part if acc is None else acc + part
            start += width
            if j == 0:
                if pending is not None:
                    epilogue(*pending)
                if r + 1 < n:
                    nxt = prologue(r + 1)
        pending = (r, x, acc)
        if nxt is not None:
            x, xn = nxt
    epilogue(*pending)


def _ffn_kernel(x_ref, gin_ref, win_ref, wout_ref, gout_ref, y_ref):
    _ffn_tile(lambda rows: x_ref[rows, :], gin_ref, win_ref, wout_ref, gout_ref, y_ref)


def _attn_out_ffn_kernel(x_ref, a_ref, wo_ref, gpre_ref, gin_ref, win_ref, wout_ref, gout_ref, y_ref):
    def load_x(rows):
        mix = jnp.dot(a_ref[rows, :], wo_ref[...], preferred_element_type=F32)
        return x_ref[rows, :] + _rms(mix, gpre_ref[...])

    _ffn_tile(load_x, gin_ref, win_ref, wout_ref, gout_ref, y_ref)


def _ffn(x2d, g_in, w_in, w_out, g_out, *, layer, tile, attn=None, w_o=None, g_pre=None):
    rows = x2d.shape[0]
    row_spec = lambda cols: pl.BlockSpec((tile, cols), lambda i: (i, 0))
    vec = _const_spec((1, D_MODEL))
    layer_spec = lambda w: pl.BlockSpec((None,) + w.shape[1:], lambda i: (layer, 0, 0),
                                        pipeline_mode=pl.Buffered(1))
    ffn_specs = [vec, layer_spec(w_in), layer_spec(w_out), vec]
    if attn is None:
        kern, in_specs = _ffn_kernel, [row_spec(D_MODEL)] + ffn_specs
        args = (x2d, g_in, w_in, w_out, g_out)
    else:
        kern = _attn_out_ffn_kernel
        in_specs = [row_spec(D_MODEL), row_spec(DA_Q), _const_spec(w_o.shape), vec] + ffn_specs
        args = (x2d, attn, w_o, g_pre, g_in, w_in, w_out, g_out)
    return pl.pallas_call(
        kern, grid=(rows // tile,), in_specs=in_specs, out_specs=row_spec(D_MODEL),
        out_shape=jax.ShapeDtypeStruct((rows, D_MODEL), F32),
        compiler_params=_params(1), name="ffn",
    )(*args)


def _qkv_kernel(x_ref, gq_ref, gkv_ref, wq_ref, wkv_ref, q_ref, kf_ref, vf_ref, kb_ref, vb_ref, *, k_transposed):
    x = x_ref[...]
    xhat = x * lax.rsqrt(jnp.mean(x * x, axis=-1, keepdims=True) + EPS)
    xq = (xhat * gq_ref[...]).astype(BF16)
    xkv = (xhat * gkv_ref[...]).astype(BF16)
    dot = functools.partial(jnp.dot, preferred_element_type=F32)
    q_ref[...] = (dot(xq, wq_ref[...]) * (DA_DH ** -0.5 * LOG2E)).astype(BF16)
    kk = dot(xkv, wkv_ref[:, :DA_Q])
    if k_transposed:
        kt = kk.T
        kf_ref[0] = kt
        kb_ref[0] = kt.astype(BF16)
    else:
        kf_ref[...] = kk
        kb_ref[...] = kk.astype(BF16)
    vv = dot(xkv, wkv_ref[:, DA_Q:])
    vf_ref[...] = vv
    vb_ref[...] = vv.astype(BF16)


def _qkv(x2d, g_q, g_kv, w_q, w_kv, *, tile, batch, k_transposed):
    rows = x2d.shape[0]
    seq = rows // batch
    row_spec = pl.BlockSpec((tile, D_MODEL), lambda i: (i, 0))
    vec = _const_spec((1, D_MODEL))
    shp = lambda dt: jax.ShapeDtypeStruct((rows, DA_Q), dt)
    if k_transposed:
        tiles_per_seq = seq // tile
        k_spec = pl.BlockSpec((1, DA_Q, tile), lambda i: (i // tiles_per_seq, 0, i % tiles_per_seq))
        k_shp = lambda dt: jax.ShapeDtypeStruct((batch, DA_Q, seq), dt)
    else:
        k_spec, k_shp = row_spec, shp
    return pl.pallas_call(
        functools.partial(_qkv_kernel, k_transposed=k_transposed), grid=(rows // tile,),
        in_specs=[row_spec, vec, vec, _const_spec(w_q.shape), _const_spec(w_kv.shape)],
        out_specs=[row_spec, k_spec, row_spec, k_spec, row_spec],
        out_shape=[shp(BF16), k_shp(F32), shp(F32), k_shp(BF16), shp(BF16)],
        compiler_params=_params(1), name="attn_qkv",
    )(x2d, g_q, g_kv, w_q, w_kv)


def _rel_bias_tile(rb_ref, head, q_pos, k_pos):
    rel = k_pos - q_pos
    n = jnp.abs(rel)
    half = N_BUCKETS // 2

    def table(offset):
        val = jnp.full(rel.shape, rb_ref[offset + half - 1, head], F32)
        for bucket in reversed(range(half - 1)):
            val = jnp.where(n < _BUCKET_BOUNDS[bucket], rb_ref[offset + bucket, head], val)
        return val

    bias = jnp.where(rel > 0, table(half), table(0)) * LOG2E
    shift = ATTN_CHUNK.bit_length() - 1
    visible = jnp.right_shift(k_pos, shift) <= jnp.right_shift(q_pos, shift)
    return jnp.where(visible, bias, -jnp.inf)


def _lambda(l1q_ref, l1k_ref, l2q_ref, l2k_ref, lam_init):
    a = jnp.exp(jnp.sum(l1q_ref[...] * l1k_ref[...], axis=-1, keepdims=True))
    b = jnp.exp(jnp.sum(l2q_ref[...] * l2k_ref[...], axis=-1, keepdims=True))
    return a - b + lam_init


def _stack_maps(q):
    lane = lax.broadcasted_iota(jnp.int32, q.shape, 1)
    zero = jnp.zeros_like(q)
    return jnp.concatenate([jnp.where(lane < DA_DH, q, zero), jnp.where(lane >= DA_DH, q, zero)], axis=0)


def _diff_finish(acc, l, lam, g, lam_init):
    t = acc.shape[0] // 2
    o = acc[:t] / l[:t] - lam * (acc[t:] / l[t:])
    return _rms(o, g) * (1.0 - lam_init)


def _largest_divisor(n, candidates):
    return max(u for u in candidates if n % u == 0)


def _prompt_attn_kernel(rb_ref, q_ref, k_ref, v_ref, l1q_ref, l1k_ref, l2q_ref, l2k_ref, g_ref,
                        o_ref, bias_s, m_s, acc_s, snext_s, *, tile, n_tiles, lam_init):
    T = tile
    hd = 2 * DA_DH
    h = pl.program_id(0)
    b = pl.program_id(1)
    dot = functools.partial(jnp.dot, preferred_element_type=F32)

    @pl.when(b == 0)
    def _():
        qp = lax.broadcasted_iota(jnp.int32, (T, 2 * T), 0) + T
        kp = lax.broadcasted_iota(jnp.int32, (T, 2 * T), 1)
        near = _rel_bias_tile(rb_ref, h, qp, kp)
        bias_s[...] = jnp.concatenate([near, near], axis=0)

    far_bias = rb_ref[N_BUCKETS // 2 - 1, h] * LOG2E
    lane = lax.broadcasted_iota(jnp.int32, (T, hd), 1)

    def update(i, key_block, n_blocks, bias):
        apply(i, scores(i, key_block, n_blocks), key_block, n_blocks, bias)

    def scores(i, key_block, n_blocks):
        q = q_ref[pl.ds(pl.multiple_of(i * T, T), T), :]
        kb = k_ref[0, :, pl.ds(pl.multiple_of(key_block * T, T), n_blocks * T)]
        zero = jnp.zeros_like(q)
        return jnp.concatenate([dot(jnp.where(lane < DA_DH, q, zero), kb),
                                dot(jnp.where(lane >= DA_DH, q, zero), kb)], axis=0)

    def apply(i, s, key_block, n_blocks, bias):
        vb = v_ref[pl.ds(pl.multiple_of(key_block * T, T), n_blocks * T), :]
        vext = jnp.concatenate([vb, jnp.ones_like(vb)], axis=1)
        first = bias is not None
        if first:
            s = s + bias
            m_new = jnp.broadcast_to(jnp.max(s, axis=-1, keepdims=True), (2 * T, LANES))
        else:
            m_prev = m_s[i]
            m_new = jnp.maximum(m_prev, jnp.max(s, axis=-1, keepdims=True))
        p = jnp.exp2(s - jnp.concatenate([m_new] * (n_blocks * T // LANES), axis=1))
        pv = jnp.dot(p.astype(BF16), vext, preferred_element_type=F32)
        if first:
            acc_s[i] = pv
            m_s[i] = m_new - far_bias
        else:
            alpha = jnp.exp2(m_prev - m_new)
            acc_s[i] = jnp.concatenate([alpha, alpha], axis=1) * acc_s[i] + pv
            m_s[i] = m_new

    update(0, 0, 1, bias_s[:, T:])
    near_unroll = _largest_divisor(n_tiles - 1, (5, 3, 2, 1))

    last_tile = n_tiles - 1
    if n_tiles > 1:
        snext_s[:, :2 * T] = scores(1, 0, 2)

    def near_body(it, carry):
        s = snext_s[:, :2 * T]
        for u in range(near_unroll):
            i = 1 + it * near_unroll + u
            i_next = jnp.minimum(i + 1, last_tile)
            s_next = scores(i_next, i_next - 1, 2)
            apply(i, s, i - 1, 2, bias_s[...])
            s = s_next
        snext_s[:, :2 * T] = s
        return carry

    lax.fori_loop(0, (n_tiles - 1) // near_unroll, near_body, 0)

    rest = []
    for i in range(2, n_tiles):
        n_far = i - 1
        if n_far % 2:
            rest.append((i, n_far - 1, 1))
        if n_far % FAR_GROUP >= 2:
            rest.append((i, n_far - n_far % FAR_GROUP, 2))
    s = scores(*rest[0]) if rest else None
    for idx, (i, key_block, n_blocks) in enumerate(rest):
        s_next = scores(*rest[idx + 1]) if idx + 1 < len(rest) else None
        apply(i, s, key_block, n_blocks, None)
        s = s_next
    n_group = sum((i - 1) // FAR_GROUP for i in range(1, n_tiles))
    far_unroll = _largest_divisor(n_group, (7, 5, 4, 3, 2, 1))
    shift = FAR_GROUP.bit_length() - 1

    def far_body(_, carry):
        i, t = carry
        s = snext_s[...]
        for _u in range(far_unroll):
            wrap = t + 1 >= jnp.right_shift(i - 1, shift)
            i_next, t_next = jnp.where(wrap, i + 1, i), jnp.where(wrap, 0, t + 1)
            t_next = jnp.where(i_next > last_tile, 0, t_next)
            i_next = jnp.minimum(i_next, last_tile)
            s_next = scores(i_next, FAR_GROUP * t_next, FAR_GROUP)
            apply(i, s, FAR_GROUP * t, FAR_GROUP, None)
            s, i, t = s_next, i_next, t_next
        snext_s[...] = s
        return i, t

    if n_group:
        snext_s[...] = scores(FAR_GROUP + 1, 0, FAR_GROUP)
        lax.fori_loop(0, n_group // far_unroll, far_body, (jnp.int32(FAR_GROUP + 1), jnp.int32(0)))

    lam = _lambda(l1q_ref, l1k_ref, l2q_ref, l2k_ref, lam_init)
    for i in range(n_tiles):
        acc = acc_s[i]
        o_ref[i * T:(i + 1) * T, :] = _diff_finish(acc[:, :hd], acc[:, hd:], lam, g_ref[...], lam_init).astype(BF16)


def _prompt_attn(rel_bias, q, k, v, lams, g_head, *, batch, seq, tile, lam_init):
    assert tile >= MAX_DISTANCE and tile % ATTN_CHUNK == 0 and seq % tile == 0
    nq = seq // tile
    hd = 2 * DA_DH
    lam_spec = _const_spec((1, DA_DH))
    seq_spec = pl.BlockSpec((seq, hd), lambda h, b: (b, h))
    feat_spec = pl.BlockSpec((1, hd, seq), lambda h, b: (b, h, 0))
    return pl.pallas_call(
        functools.partial(_prompt_attn_kernel, tile=tile, n_tiles=nq, lam_init=lam_init),
        grid=(DA_HEADS, batch),
        in_specs=[pl.BlockSpec(memory_space=pltpu.SMEM), seq_spec,
                  feat_spec, seq_spec,
                  lam_spec, lam_spec, lam_spec, lam_spec, _const_spec((1, hd))],
        out_specs=seq_spec,
        out_shape=jax.ShapeDtypeStruct((batch * seq, DA_Q), BF16),
        scratch_shapes=[pltpu.VMEM((2 * tile, 2 * tile), F32),
                        pltpu.VMEM((nq, 2 * tile, LANES), F32), pltpu.VMEM((nq, 2 * tile, 2 * hd), F32),
                        pltpu.VMEM((2 * tile, FAR_GROUP * tile), F32)],
        compiler_params=_params(2), name="diff_attn_prompt",
    )(rel_bias, q, k, v, *lams, g_head)


def _sample_attn_kernel(rb_ref, q_ref, kp_ref, vp_ref, kn_ref, vn_ref, l1q_ref, l1k_ref, l2q_ref, l2k_ref,
                        g_ref, o_ref, bias_p, bias_n, *, past, seq, lam_init):
    b = pl.program_id(0)
    hd = 2 * DA_DH
    dot = functools.partial(jnp.dot, preferred_element_type=F32)
    nt = lambda a, bb: lax.dot_general(a, bb, (((1,), (1,)), ((), ())), preferred_element_type=F32)

    @pl.when(b == 0)
    def _():
        qp = lax.broadcasted_iota(jnp.int32, (seq, past), 0) + past
        kp = lax.broadcasted_iota(jnp.int32, (seq, past), 1)
        qn = lax.broadcasted_iota(jnp.int32, (seq, seq), 0) + past
        kn = lax.broadcasted_iota(jnp.int32, (seq, seq), 1) + past
        for h in range(DA_HEADS):
            tile = _rel_bias_tile(rb_ref, h, qp, kp)
            bias_p[h] = jnp.concatenate([tile, tile], axis=0)
            tile = _rel_bias_tile(rb_ref, h, qn, kn)
            bias_n[h] = jnp.concatenate([tile, tile], axis=0)

    lam = _lambda(l1q_ref, l1k_ref, l2q_ref, l2k_ref, lam_init)
    v_heads = jnp.transpose(vp_ref[0], (1, 0, 2))
    for h in range(DA_HEADS):
        cols = slice(h * hd, (h + 1) * hd)
        qs = _stack_maps(q_ref[:, cols])
        s_p = dot(qs, kp_ref[0, cols, :].astype(BF16)) + bias_p[h]
        s_n = nt(qs, kn_ref[:, cols]) + bias_n[h]
        m = jnp.maximum(jnp.max(s_p, axis=-1, keepdims=True), jnp.max(s_n, axis=-1, keepdims=True))
        p_p = jnp.exp2(s_p - m)
        p_n = jnp.exp2(s_n - m)
        l = jnp.sum(p_p, axis=-1, keepdims=True) + jnp.sum(p_n, axis=-1, keepdims=True)
        acc = (dot(p_p.astype(BF16), v_heads[h].astype(BF16))
               + dot(p_n.astype(BF16), vn_ref[:, cols]))
        o_ref[:, cols] = _diff_finish(acc, l, lam, g_ref[...], lam_init).astype(BF16)


def _sample_attn(rel_bias, q, cache_kt, cache_v, k_new, v_new, lams, g_head, *, batch, seq, lam_init):
    past = cache_kt.shape[2]
    hd = 2 * DA_DH
    lam_spec = _const_spec((1, DA_DH))
    new_spec = pl.BlockSpec((seq, DA_Q), lambda b: (b, 0))
    return pl.pallas_call(
        functools.partial(_sample_attn_kernel, past=past, seq=seq, lam_init=lam_init),
        grid=(batch,),
        in_specs=[pl.BlockSpec(memory_space=pltpu.SMEM), new_spec,
                  pl.BlockSpec((1, DA_Q, past), lambda b: (b, 0, 0)),
                  pl.BlockSpec((1, past, DA_HEADS, hd), lambda b: (b, 0, 0, 0)),
                  new_spec, new_spec, lam_spec, lam_spec, lam_spec, lam_spec, _const_spec((1, hd))],
        out_specs=new_spec,
        out_shape=jax.ShapeDtypeStruct((batch * seq, DA_Q), BF16),
        scratch_shapes=[pltpu.VMEM((DA_HEADS, 2 * seq, past), F32), pltpu.VMEM((DA_HEADS, 2 * seq, seq), F32)],
        compiler_params=_params(1), name="diff_attn_sample",
    )(rel_bias, q, cache_kt, cache_v, k_new, v_new, *lams, g_head)


def _prep_weights(norm_g, mlstm_w_in, mlstm_b_gate, mlstm_g_head, mlstm_w_out, kv_g, kv_w, diff_w_q,
                  diff_lam_q1, diff_lam_k1, diff_lam_q2, diff_lam_k2, diff_g_head, diff_w_o,
                  ffn_w_in, ffn_w_out):
    n_main = 2 * M_QK + M_V + D_MODEL
    w_in = mlstm_w_in[0]
    wg = jnp.pad(w_in[:, n_main:], ((0, 0), (0, LANES - 2 * M_HEADS)))
    wg_hi = wg.astype(BF16)
    wg_lo = (wg - wg_hi.astype(F32)).astype(BF16)
    row = lambda a: a.reshape(1, -1)
    return dict(
        norm=[[row(norm_g[l, j]) for j in range(4)] for l in range(2)],
        w_main=w_in.astype(BF16),
        w_gate=jnp.concatenate([wg_hi, wg_lo], axis=1),
        b_gate=jnp.pad(mlstm_b_gate[0], (0, LANES - 2 * M_HEADS)).reshape(1, LANES),
        g_mhead=row(mlstm_g_head[0]),
        w_mout=mlstm_w_out[0].astype(BF16),
        kv_g=row(kv_g), kv_w=kv_w.astype(BF16), w_q=diff_w_q[0].astype(BF16),
        lams=[row(diff_lam_q1[0]), row(diff_lam_k1[0]), row(diff_lam_q2[0]), row(diff_lam_k2[0])],
        g_dhead=row(diff_g_head[0]), w_o=diff_w_o[0].astype(BF16),
        ffn_in=ffn_w_in.astype(BF16), ffn_out=ffn_w_out.astype(BF16),
    )


def _trunk(x, c0, n0, m0, past_k, past_v, rel_bias, w, *, row_tile, ffn_tile, chunk, attn_tile):
    batch, seq, _ = x.shape
    rows = batch * seq
    x2d = x.reshape(rows, D_MODEL)
    lam_init = 0.8 - 0.6 * math.exp(-0.3 * 1)

    m0b = jnp.broadcast_to(m0[..., None], (batch, M_HEADS, LANES))
    x1, c_new, n_new, m_new = _mlstm(x2d, w["norm"][0][0], w["w_main"], w["w_gate"], w["b_gate"], c0, n0, m0b,
                                     w["w_mout"], w["g_mhead"], w["norm"][0][1], batch=batch, chunk=chunk)
    x2 = _ffn(x1, w["norm"][0][2], w["ffn_in"], w["ffn_out"], w["norm"][0][3], layer=0, tile=ffn_tile)

    aq, k_f32, v_f32, k_bf, v_bf = _qkv(x2, w["norm"][1][0], w["kv_g"], w["w_q"], w["kv_w"], tile=row_tile,
                                        batch=batch, k_transposed=past_k is None)
    if past_k is None:
        attn = _prompt_attn(rel_bias, aq, k_bf, v_bf, w["lams"], w["g_dhead"],
                            batch=batch, seq=seq, tile=attn_tile, lam_init=lam_init)
        k_f32 = k_f32.reshape(batch, DA_HEADS, 2, DA_DH, seq).transpose(0, 4, 1, 2, 3)
    else:
        past = past_k.shape[1]
        past_kt = past_k.transpose(0, 2, 3, 4, 1).reshape(batch, DA_Q, past)
        attn = _sample_attn(rel_bias, aq, past_kt, past_v, k_bf, v_bf, w["lams"], w["g_dhead"],
                            batch=batch, seq=seq, lam_init=lam_init)
    y = _ffn(x2, w["norm"][1][2], w["ffn_in"], w["ffn_out"], w["norm"][1][3], layer=1, tile=ffn_tile,
             attn=attn, w_o=w["w_o"], g_pre=w["norm"][1][1])

    return (y.reshape(batch, seq, D_MODEL), c_new[None], n_new[None], m_new[None, :, :, 0],
            k_f32.reshape(batch, seq, DA_HEADS, 2, DA_DH), v_f32.reshape(batch, seq, DA_HEADS, 2 * DA_DH))


def kernel(x_prompt, x_sample, state_C, state_n, state_m, cache_k, cache_v, norm_g, mlstm_w_in, mlstm_b_gate, mlstm_g_head, mlstm_w_out, kv_g, kv_w, rel_bias, diff_w_q, diff_lam_q1, diff_lam_k1, diff_lam_q2, diff_lam_k2, diff_g_head, diff_w_o, ffn_w_in, ffn_w_out):
    w = _prep_weights(norm_g, mlstm_w_in, mlstm_b_gate, mlstm_g_head, mlstm_w_out, kv_g, kv_w, diff_w_q,
                      diff_lam_q1, diff_lam_k1, diff_lam_q2, diff_lam_k2, diff_g_head, diff_w_o,
                      ffn_w_in, ffn_w_out)
    pb = x_prompt.shape[0]
    zeros = lambda *s: jnp.zeros(s, F32)
    y_p, p_c, p_n, p_m, p_k, p_v = _trunk(
        x_prompt, zeros(pb, M_HEADS, M_DV, M_DK), zeros(pb, M_HEADS, M_DK), zeros(pb, M_HEADS),
        None, None, rel_bias, w, row_tile=512, ffn_tile=1024, chunk=256, attn_tile=256)
    sb, ss, _ = x_sample.shape
    y_s, s_c, s_n, s_m, s_k, s_v = _trunk(
        x_sample, state_C[0], state_n[0], state_m[0], cache_k, cache_v, rel_bias, w,
        row_tile=sb * ss, ffn_tile=sb * ss, chunk=ss, attn_tile=None)
    return (y_p, y_s, p_c, p_n, p_m, p_k, p_v, s_c, s_n, s_m, s_k, s_v)
```

```python
import functools
import math

import jax
import jax.numpy as jnp
from jax import lax
from jax.experimental import pallas as pl
from jax.experimental.pallas import tpu as pltpu

F32 = jnp.float32
BF16 = jnp.bfloat16

D_MODEL = 1024
ATTN_CHUNK = 64
M_HEADS = 4
M_DK = D_MODEL // (2 * M_HEADS)
M_DV = D_MODEL // M_HEADS
M_QK = M_HEADS * M_DK
M_V = M_HEADS * M_DV
DA_HEADS = 8
DA_DH = D_MODEL // (2 * DA_HEADS)
DA_Q = DA_HEADS * 2 * DA_DH
FFN_HIDDEN = -(-8 * D_MODEL // (3 * 256)) * 256
N_BUCKETS = 32
MAX_DISTANCE = 128
EPS = 1e-6
LOG2E = math.log2(math.e)

LANES = 128
GATE_ROWS = 16
VMEM_LIMIT = 56 * 1024 * 1024
FAR_GROUP = 4
FFN_GROUP_ROWS = 512
FFN_COL_CHUNKS = (1024, 1024, 768)
assert sum(FFN_COL_CHUNKS) == FFN_HIDDEN


def _bucket_upper_bounds():
    half = N_BUCKETS // 2
    max_exact = half // 2
    ratio = MAX_DISTANCE // max_exact
    steps = half - max_exact
    bounds = [n + 1 for n in range(max_exact)]
    for k in range(1, steps):
        n = max_exact
        while n ** steps < (max_exact ** steps) * (ratio ** k):
            n += 1
        bounds.append(n)
    return bounds


_BUCKET_BOUNDS = _bucket_upper_bounds()


def _rms(x, g):
    return x * lax.rsqrt(jnp.mean(x * x, axis=-1, keepdims=True) + EPS) * g


def _log_sigmoid(x):
    return jnp.minimum(x, 0.0) - jnp.log1p(jnp.exp(-jnp.abs(x)))


def _split3(a):
    a1 = a.astype(BF16)
    r1 = a - a1.astype(F32)
    a2 = r1.astype(BF16)
    a3 = (r1 - a2.astype(F32)).astype(BF16)
    return a1, a2, a3


def _const_spec(shape):
    return pl.BlockSpec(shape, lambda *_: (0,) * len(shape), pipeline_mode=pl.Buffered(1))


def _params(n_axes):
    return pltpu.CompilerParams(dimension_semantics=("arbitrary",) * n_axes,
                                vmem_limit_bytes=VMEM_LIMIT)


def _mlstm_kernel(x_ref, xnext_ref, g_ref, w_ref, wg_ref, bg_ref, c0_ref, n0_ref, m0_ref,
                  wout_ref, ghead_ref, gpost_ref,
                  x1_ref, cout_ref, nout_ref, mout_ref,
                  q_s, k_s, v_s, o_s, gcol_s, grow_s, bcol_s, brow_s, ct_s, m_s, hcat_s,
                  *, chunk, chunks_per_step, n_steps):
    L = chunk
    c = pl.program_id(1)
    dot = functools.partial(jnp.dot, preferred_element_type=F32)

    def wide(rep, n):
        return rep[:, :n] if n < LANES else jnp.concatenate([rep] * (n // LANES), axis=1)

    def projection(load_x, slot):
        xs = {}
        half = M_V // 2
        v0, o0 = 2 * M_QK, 2 * M_QK + M_V

        def queries():
            xn = _rms(load_x(), g_ref[...])
            xs["hi"] = xn.astype(BF16)
            xs["lo"] = (xn - xs["hi"].astype(F32)).astype(BF16)
            q_s[slot] = dot(xs["hi"], w_ref[:, :M_QK]).astype(BF16)

        def keys():
            k_s[slot] = (dot(xs["hi"], w_ref[:, M_QK:2 * M_QK]) * (M_DK ** -0.5)).astype(BF16)

        def values(part):
            def step():
                cols = slice(part * half, (part + 1) * half)
                v_s[slot, :, cols] = dot(xs["hi"], w_ref[:, v0 + part * half:v0 + (part + 1) * half]).astype(BF16)
            return step

        def out_gate(part):
            def step():
                cols = slice(part * half, (part + 1) * half)
                o_s[slot, :, cols] = dot(xs["hi"], w_ref[:, o0 + part * half:o0 + (part + 1) * half])
            return step

        def gates():
            gh = dot(xs["hi"], wg_ref[...])
            gl = dot(xs["lo"], wg_ref[:, :LANES])
            pre = gh[:, :LANES] + gh[:, LANES:] + gl + bg_ref[...]
            lane = lax.broadcasted_iota(jnp.int32, pre.shape, 1)
            gv = jnp.where(lane < M_HEADS, pre, _log_sigmoid(pre))
            gcol_s[slot] = gv
            if L < LANES:
                gv = jnp.concatenate([gv, jnp.zeros((LANES - L, LANES), F32)], axis=0)
            grow_s[slot] = gv.T[:GATE_ROWS, :L]

        def cumulate():
            bcol_s[slot] = sum(dot(tri, part) for part in _split3(gcol_s[slot]))
            brow_s[slot] = sum(dot(part, tri_t) for part in _split3(grow_s[slot]))

        return [queries, gates, cumulate, keys, values(0), values(1), out_gate(0), out_gate(1)]

    row = lax.broadcasted_iota(jnp.int32, (L, L), 0)
    col = lax.broadcasted_iota(jnp.int32, (L, L), 1)
    causal = col <= row
    tri = jnp.where(causal, 1.0, 0.0).astype(BF16)
    tri_t = jnp.where(row <= col, 1.0, 0.0).astype(BF16)
    ones = jnp.ones((L, LANES), BF16)

    def scan(slot, rows, fillers):
        fillers = list(fillers)

        def fill():
            if fillers:
                fillers.pop(0)()

        gcol = gcol_s[slot]
        grow = grow_s[slot]
        bcol = bcol_s[slot]
        brow = brow_s[slot]

        for h in range(M_HEADS):
            q = q_s[slot, :, h * M_DK:(h + 1) * M_DK]
            k = k_s[slot, :, h * M_DK:(h + 1) * M_DK]
            v = v_s[slot, :, h * M_DV:(h + 1) * M_DV]
            ig_row = grow[h:h + 1, :]
            b_row = brow[M_HEADS + h:M_HEADS + h + 1, :]
            ig = jnp.broadcast_to(gcol[:, h:h + 1], (L, LANES))
            b = jnp.broadcast_to(bcol[:, M_HEADS + h:M_HEADS + h + 1], (L, LANES))
            m0 = m_s[h:h + 1, :]
            ct = ct_s[h]

            d = jnp.where(causal, wide(b, L) - b_row + ig_row, -jnp.inf)
            g = b + m0
            m = jnp.maximum(g, jnp.max(d, axis=-1, keepdims=True))
            w_inter = jnp.exp(g - m)
            qk = lax.dot_general(q, k, (((1,), (1,)), ((), ())), preferred_element_type=F32)
            s = (jnp.exp(d - wide(m, L)) * qk).astype(BF16)
            inter = dot(q, ct.astype(BF16))
            fill()
            num = wide(w_inter, M_DV) * inter[:, :M_DV] + dot(s, v)
            den = w_inter * inter[:, M_DV:] + dot(s, ones)
            inv = 1.0 / jnp.maximum(jnp.abs(den), jnp.exp(-m))
            hh = num * wide(inv, M_DV)

            m_last = m[L - 1:L, :]
            w_state = jnp.exp(g[L - 1:L, :] - m_last)
            w_rows = jnp.exp(b[L - 1:L, :] - b + ig - m_last)
            vw = jnp.concatenate([(v.astype(F32) * wide(w_rows, M_DV)).astype(BF16), w_rows.astype(BF16)], axis=1)
            upd = lax.dot_general(k, vw, (((0,), (0,)), ((), ())), preferred_element_type=F32)
            ct_s[h] = wide(w_state, M_DV + LANES) * ct + upd
            m_s[h:h + 1, :] = m_last

            hn = _rms(hh, ghead_ref[:, h * M_DV:(h + 1) * M_DV])
            gate = jax.nn.sigmoid(o_s[slot, :, h * M_DV:(h + 1) * M_DV])
            hcat_s[slot, :, h * M_DV:(h + 1) * M_DV] = (hn * gate).astype(BF16)
            fill()
        while fillers:
            fill()

        y = dot(hcat_s[slot], wout_ref[...])
        x1_ref[0, rows, :] = x_ref[0, rows, :] + _rms(y, gpost_ref[...])

    @pl.when(c == 0)
    def _():
        for h in range(M_HEADS):
            ct_s[h, :, :M_DV] = c0_ref[0, h].T
            ct_s[h, :, M_DV:] = jnp.broadcast_to(n0_ref[0, h:h + 1, :], (LANES, M_DK)).T
        m_s[...] = m0_ref[0]
        for step in projection(lambda: x_ref[0, :L, :], 0):
            step()

    for j in range(chunks_per_step):
        rows = slice(j * L, (j + 1) * L)
        if j + 1 < chunks_per_step:
            following = projection(lambda j=j: x_ref[0, (j + 1) * L:(j + 2) * L, :], (j + 1) % 2)
        elif n_steps > 1:
            following = projection(lambda: xnext_ref[0], (j + 1) % 2)
        else:
            following = []
        scan(j % 2, rows, following)

    @pl.when(c == n_steps - 1)
    def _():
        for h in range(M_HEADS):
            t = ct_s[h].T
            cout_ref[0, h] = t[:M_DV]
            nout_ref[0, h:h + 1, :] = t[M_DV:M_DV + 1]
        mout_ref[0] = m_s[...]


def _mlstm(x2d, g_in, w_main, w_gate, b_gate, c0, n0, m0b, w_out, g_head, g_post, *, batch, chunk):
    rows = x2d.shape[0]
    seq = rows // batch
    n_chunks = seq // chunk
    cps = 2 if n_chunks % 2 == 0 else 1
    assert cps == 2 or n_chunks == 1
    n_steps = n_chunks // cps
    x3d = x2d.reshape(batch, seq, D_MODEL)
    step_spec = pl.BlockSpec((1, cps * chunk, D_MODEL), lambda b, c: (b, c, 0))
    next_spec = pl.BlockSpec((1, chunk, D_MODEL), lambda b, c: (b, jnp.minimum(cps * (c + 1), n_chunks - 1), 0))
    state4 = pl.BlockSpec((1, M_HEADS, M_DV, M_DK), lambda b, c: (b, 0, 0, 0))
    state3 = pl.BlockSpec((1, M_HEADS, LANES), lambda b, c: (b, 0, 0))
    vec = _const_spec((1, D_MODEL))
    x1, c_new, n_new, m_new = pl.pallas_call(
        functools.partial(_mlstm_kernel, chunk=chunk, chunks_per_step=cps, n_steps=n_steps),
        grid=(batch, n_steps),
        in_specs=[step_spec, next_spec, vec, _const_spec(w_main.shape), _const_spec(w_gate.shape),
                  _const_spec((1, LANES)), state4, state3, state3,
                  _const_spec((M_V, D_MODEL)), _const_spec((1, M_V)), vec],
        out_specs=[step_spec, state4, state3, state3],
        out_shape=[jax.ShapeDtypeStruct((batch, seq, D_MODEL), F32),
                   jax.ShapeDtypeStruct((batch, M_HEADS, M_DV, M_DK), F32),
                   jax.ShapeDtypeStruct((batch, M_HEADS, LANES), F32),
                   jax.ShapeDtypeStruct((batch, M_HEADS, LANES), F32)],
        scratch_shapes=[pltpu.VMEM((2, chunk, M_QK), BF16), pltpu.VMEM((2, chunk, M_QK), BF16),
                        pltpu.VMEM((2, chunk, M_V), BF16), pltpu.VMEM((2, chunk, D_MODEL), F32),
                        pltpu.VMEM((2, chunk, LANES), F32), pltpu.VMEM((2, GATE_ROWS, chunk), F32),
                        pltpu.VMEM((2, chunk, LANES), F32), pltpu.VMEM((2, GATE_ROWS, chunk), F32),
                        pltpu.VMEM((M_HEADS, M_DK, M_DV + LANES), F32), pltpu.VMEM((M_HEADS, LANES), F32),
                        pltpu.VMEM((2, chunk, M_V), BF16)],
        compiler_params=_params(2),
        name="mlstm",
    )(x3d, x3d, g_in, w_main, w_gate, b_gate, c0, n0, m0b, w_out, g_head, g_post)
    return x1.reshape(rows, D_MODEL), c_new, n_new, m_new


def _ffn_tile(load_x, gin_ref, win_ref, wout_ref, gout_ref, y_ref):
    dot = functools.partial(jnp.dot, preferred_element_type=F32)
    rows = y_ref.shape[0]
    n = rows // FFN_GROUP_ROWS if rows % FFN_GROUP_ROWS == 0 else 1
    groups = [slice(r * rows // n, (r + 1) * rows // n) for r in range(n)]

    def prologue(r):
        x = load_x(groups[r])
        return x, _rms(x, gin_ref[...]).astype(BF16)

    def epilogue(r, x, acc):
        y_ref[groups[r], :] = x + _rms(acc, gout_ref[...])

    x, xn = prologue(0)
    pending = None
    for r in range(n):
        acc = None
        start = 0
        nxt = None
        for j, width in enumerate(FFN_COL_CHUNKS):
            gate = dot(xn, win_ref[:, start:start + width])
            up = dot(xn, win_ref[:, FFN_HIDDEN + start:FFN_HIDDEN + start + width])
            act = (jax.nn.silu(gate) * up).astype(BF16)
            part = dot(act, wout_ref[start:start + width, :])
            acc = part if acc is None else acc + part
            start += width
            if j == 0:
                if pending is not None:
                    epilogue(*pending)
                if r + 1 < n:
                    nxt = prologue(r + 1)
        pending = (r, x, acc)
        if nxt is not None:
            x, xn = nxt
    epilogue(*pending)


def _ffn_kernel(x_ref, gin_ref, win_ref, wout_ref, gout_ref, y_ref):
    _ffn_tile(lambda rows: x_ref[rows, :], gin_ref, win_ref, wout_ref, gout_ref, y_ref)


def _attn_out_ffn_kernel(x_ref, a_ref, wo_ref, gpre_ref, gin_ref, win_ref, wout_ref, gout_ref, y_ref):
    def load_x(rows):
        mix = jnp.dot(a_ref[rows, :], wo_ref[...], preferred_element_type=F32)
        return x_ref[rows, :] + _rms(mix, gpre_ref[...])

    _ffn_tile(load_x, gin_ref, win_ref, wout_ref, gout_ref, y_ref)


def _ffn(x2d, g_in, w_in, w_out, g_out, *, layer, tile, attn=None, w_o=None, g_pre=None):
    rows = x2d.shape[0]
    row_spec = lambda cols: pl.BlockSpec((tile, cols), lambda i: (i, 0))
    vec = _const_spec((1, D_MODEL))
    layer_spec = lambda w: pl.BlockSpec((None,) + w.shape[1:], lambda i: (layer, 0, 0),
                                        pipeline_mode=pl.Buffered(1))
    ffn_specs = [vec, layer_spec(w_in), layer_spec(w_out), vec]
    if attn is None:
        kern, in_specs = _ffn_kernel, [row_spec(D_MODEL)] + ffn_specs
        args = (x2d, g_in, w_in, w_out, g_out)
    else:
        kern = _attn_out_ffn_kernel
        in_specs = [row_spec(D_MODEL), row_spec(DA_Q), _const_spec(w_o.shape), vec] + ffn_specs
        args = (x2d, attn, w_o, g_pre, g_in, w_in, w_out, g_out)
    return pl.pallas_call(
        kern, grid=(rows // tile,), in_specs=in_specs, out_specs=row_spec(D_MODEL),
        out_shape=jax.ShapeDtypeStruct((rows, D_MODEL), F32),
        compiler_params=_params(1), name="ffn",
    )(*args)


def _qkv_kernel(x_ref, gq_ref, gkv_ref, wq_ref, wkv_ref, q_ref, kf_ref, vf_ref, kb_ref, vb_ref, *, k_transposed):
    x = x_ref[...]
    xhat = x * lax.rsqrt(jnp.mean(x * x, axis=-1, keepdims=True) + EPS)
    xq = (xhat * gq_ref[...]).astype(BF16)
    xkv = (xhat * gkv_ref[...]).astype(BF16)
    dot = functools.partial(jnp.dot, preferred_element_type=F32)
    q_ref[...] = (dot(xq, wq_ref[...]) * (DA_DH ** -0.5 * LOG2E)).astype(BF16)
    kk = dot(xkv, wkv_ref[:, :DA_Q])
    if k_transposed:
        kt = kk.T
        kf_ref[0] = kt
        kb_ref[0] = kt.astype(BF16)
    else:
        kf_ref[...] = kk
        kb_ref[...] = kk.astype(BF16)
    vv = dot(xkv, wkv_ref[:, DA_Q:])
    vf_ref[...] = vv
    vb_ref[...] = vv.astype(BF16)


def _qkv(x2d, g_q, g_kv, w_q, w_kv, *, tile, batch, k_transposed):
    rows = x2d.shape[0]
    seq = rows // batch
    row_spec = pl.BlockSpec((tile, D_MODEL), lambda i: (i, 0))
    vec = _const_spec((1, D_MODEL))
    shp = lambda dt: jax.ShapeDtypeStruct((rows, DA_Q), dt)
    if k_transposed:
        tiles_per_seq = seq // tile
        k_spec = pl.BlockSpec((1, DA_Q, tile), lambda i: (i // tiles_per_seq, 0, i % tiles_per_seq))
        k_shp = lambda dt: jax.ShapeDtypeStruct((batch, DA_Q, seq), dt)
    else:
        k_spec, k_shp = row_spec, shp
    return pl.pallas_call(
        functools.partial(_qkv_kernel, k_transposed=k_transposed), grid=(rows // tile,),
        in_specs=[row_spec, vec, vec, _const_spec(w_q.shape), _const_spec(w_kv.shape)],
        out_specs=[row_spec, k_spec, row_spec, k_spec, row_spec],
        out_shape=[shp(BF16), k_shp(F32), shp(F32), k_shp(BF16), shp(BF16)],
        compiler_params=_params(1), name="attn_qkv",
    )(x2d, g_q, g_kv, w_q, w_kv)


def _rel_bias_tile(rb_ref, head, q_pos, k_pos):
    rel = k_pos - q_pos
    n = jnp.abs(rel)
    half = N_BUCKETS // 2

    def table(offset):
        val = jnp.full(rel.shape, rb_ref[offset + half - 1, head], F32)
        for bucket in reversed(range(half - 1)):
            val = jnp.where(n < _BUCKET_BOUNDS[bucket], rb_ref[offset + bucket, head], val)
        return val

    bias = jnp.where(rel > 0, table(half), table(0)) * LOG2E
    shift = ATTN_CHUNK.bit_length() - 1
    visible = jnp.right_shift(k_pos, shift) <= jnp.right_shift(q_pos, shift)
    return jnp.where(visible, bias, -jnp.inf)


def _lambda(l1q_ref, l1k_ref, l2q_ref, l2k_ref, lam_init):
    a = jnp.exp(jnp.sum(l1q_ref[...] * l1k_ref[...], axis=-1, keepdims=True))
    b = jnp.exp(jnp.sum(l2q_ref[...] * l2k_ref[...], axis=-1, keepdims=True))
    return a - b + lam_init


def _stack_maps(q):
    lane = lax.broadcasted_iota(jnp.int32, q.shape, 1)
    zero = jnp.zeros_like(q)
    return jnp.concatenate([jnp.where(lane < DA_DH, q, zero), jnp.where(lane >= DA_DH, q, zero)], axis=0)


def _diff_finish(acc, l, lam, g, lam_init):
    t = acc.shape[0] // 2
    o = acc[:t] / l[:t] - lam * (acc[t:] / l[t:])
    return _rms(o, g) * (1.0 - lam_init)


def _largest_divisor(n, candidates):
    return max(u for u in candidates if n % u == 0)


def _prompt_attn_kernel(rb_ref, q_ref, k_ref, v_ref, l1q_ref, l1k_ref, l2q_ref, l2k_ref, g_ref,
                        o_ref, bias_s, m_s, acc_s, snext_s, *, tile, n_tiles, lam_init):
    T = tile
    hd = 2 * DA_DH
    h = pl.program_id(0)
    b = pl.program_id(1)
    dot = functools.partial(jnp.dot, preferred_element_type=F32)

    @pl.when(b == 0)
    def _():
        qp = lax.broadcasted_iota(jnp.int32, (T, 2 * T), 0) + T
        kp = lax.broadcasted_iota(jnp.int32, (T, 2 * T), 1)
        near = _rel_bias_tile(rb_ref, h, qp, kp)
        bias_s[...] = jnp.concatenate([near, near], axis=0)

    far_bias = rb_ref[N_BUCKETS // 2 - 1, h] * LOG2E
    lane = lax.broadcasted_iota(jnp.int32, (T, hd), 1)

    def update(i, key_block, n_blocks, bias):
        apply(i, scores(i, key_block, n_blocks), key_block, n_blocks, bias)

    def scores(i, key_block, n_blocks):
        q = q_ref[pl.ds(pl.multiple_of(i * T, T), T), :]
        kb = k_ref[0, :, pl.ds(pl.multiple_of(key_block * T, T), n_blocks * T)]
        zero = jnp.zeros_like(q)
        return jnp.concatenate([dot(jnp.where(lane < DA_DH, q, zero), kb),
                                dot(jnp.where(lane >= DA_DH, q, zero), kb)], axis=0)

    def apply(i, s, key_block, n_blocks, bias):
        vb = v_ref[pl.ds(pl.multiple_of(key_block * T, T), n_blocks * T), :]
        vext = jnp.concatenate([vb, jnp.ones_like(vb)], axis=1)
        first = bias is not None
        if first:
            s = s + bias
            m_new = jnp.broadcast_to(jnp.max(s, axis=-1, keepdims=True), (2 * T, LANES))
        else:
            m_prev = m_s[i]
            m_new = jnp.maximum(m_prev, jnp.max(s, axis=-1, keepdims=True))
        p = jnp.exp2(s - jnp.concatenate([m_new] * (n_blocks * T // LANES), axis=1))
        pv = jnp.dot(p.astype(BF16), vext, preferred_element_type=F32)
        if first:
            acc_s[i] = pv
            m_s[i] = m_new - far_bias
        else:
            alpha = jnp.exp2(m_prev - m_new)
            acc_s[i] = jnp.concatenate([alpha, alpha], axis=1) * acc_s[i] + pv
            m_s[i] = m_new

    update(0, 0, 1, bias_s[:, T:])
    near_unroll = _largest_divisor(n_tiles - 1, (5, 3, 2, 1))

    last_tile = n_tiles - 1
    if n_tiles > 1:
        snext_s[:, :2 * T] = scores(1, 0, 2)

    def near_body(it, carry):
        s = snext_s[:, :2 * T]
        for u in range(near_unroll):
            i = 1 + it * near_unroll + u
            i_next = jnp.minimum(i + 1, last_tile)
            s_next = scores(i_next, i_next - 1, 2)
            apply(i, s, i - 1, 2, bias_s[...])
            s = s_next
        snext_s[:, :2 * T] = s
        return carry

    lax.fori_loop(0, (n_tiles - 1) // near_unroll, near_body, 0)

    rest = []
    for i in range(2, n_tiles):
        n_far = i - 1
        if n_far % 2:
            rest.append((i, n_far - 1, 1))
        if n_far % FAR_GROUP >= 2:
            rest.append((i, n_far - n_far % FAR_GROUP, 2))
    s = scores(*rest[0]) if rest else None
    for idx, (i, key_block, n_blocks) in enumerate(rest):
        s_next = scores(*rest[idx + 1]) if idx + 1 < len(rest) else None
        apply(i, s, key_block, n_blocks, None)
        s = s_next
    n_group = sum((i - 1) // FAR_GROUP for i in range(1, n_tiles))
    far_unroll = _largest_divisor(n_group, (7, 5, 4, 3, 2, 1))
    shift = FAR_GROUP.bit_length() - 1

    def far_body(_, carry):
        i, t = carry
        s = snext_s[...]
        for _u in range(far_unroll):
            wrap = t + 1 >= jnp.right_shift(i - 1, shift)
            i_next, t_next = jnp.where(wrap, i + 1, i), jnp.where(wrap, 0, t + 1)
            t_next = jnp.where(i_next > last_tile, 0, t_next)
            i_next = jnp.minimum(i_next, last_tile)
            s_next = scores(i_next, FAR_GROUP * t_next, FAR_GROUP)
            apply(i, s, FAR_GROUP * t, FAR_GROUP, None)
            s, i, t = s_next, i_next, t_next
        snext_s[...] = s
        return i, t

    if n_group:
        snext_s[...] = scores(FAR_GROUP + 1, 0, FAR_GROUP)
        lax.fori_loop(0, n_group // far_unroll, far_body, (jnp.int32(FAR_GROUP + 1), jnp.int32(0)))

    lam = _lambda(l1q_ref, l1k_ref, l2q_ref, l2k_ref, lam_init)
    for i in range(n_tiles):
        acc = acc_s[i]
        o_ref[i * T:(i + 1) * T, :] = _diff_finish(acc[:, :hd], acc[:, hd:], lam, g_ref[...], lam_init).astype(BF16)


def _prompt_attn(rel_bias, q, k, v, lams, g_head, *, batch, seq, tile, lam_init):
    assert tile >= MAX_DISTANCE and tile % ATTN_CHUNK == 0 and seq % tile == 0
    nq = seq // tile
    hd = 2 * DA_DH
    lam_spec = _const_spec((1, DA_DH))
    seq_spec = pl.BlockSpec((seq, hd), lambda h, b: (b, h))
    feat_spec = pl.BlockSpec((1, hd, seq), lambda h, b: (b, h, 0))
    return pl.pallas_call(
        functools.partial(_prompt_attn_kernel, tile=tile, n_tiles=nq, lam_init=lam_init),
        grid=(DA_HEADS, batch),
        in_specs=[pl.BlockSpec(memory_space=pltpu.SMEM), seq_spec,
                  feat_spec, seq_spec,
                  lam_spec, lam_spec, lam_spec, lam_spec, _const_spec((1, hd))],
        out_specs=seq_spec,
        out_shape=jax.ShapeDtypeStruct((batch * seq, DA_Q), BF16),
        scratch_shapes=[pltpu.VMEM((2 * tile, 2 * tile), F32),
                        pltpu.VMEM((nq, 2 * tile, LANES), F32), pltpu.VMEM((nq, 2 * tile, 2 * hd), F32),
                        pltpu.VMEM((2 * tile, FAR_GROUP * tile), F32)],
        compiler_params=_params(2), name="diff_attn_prompt",
    )(rel_bias, q, k, v, *lams, g_head)


def _sample_attn_kernel(rb_ref, q_ref, kp_ref, vp_ref, kn_ref, vn_ref, l1q_ref, l1k_ref, l2q_ref, l2k_ref,
                        g_ref, o_ref, bias_p, bias_n, *, past, seq, lam_init):
    b = pl.program_id(0)
    hd = 2 * DA_DH
    dot = functools.partial(jnp.dot, preferred_element_type=F32)
    nt = lambda a, bb: lax.dot_general(a, bb, (((1,), (1,)), ((), ())), preferred_element_type=F32)

    @pl.when(b == 0)
    def _():
        qp = lax.broadcasted_iota(jnp.int32, (seq, past), 0) + past
        kp = lax.broadcasted_iota(jnp.int32, (seq, past), 1)
        qn = lax.broadcasted_iota(jnp.int32, (seq, seq), 0) + past
        kn = lax.broadcasted_iota(jnp.int32, (seq, seq), 1) + past
        for h in range(DA_HEADS):
            tile = _rel_bias_tile(rb_ref, h, qp, kp)
            bias_p[h] = jnp.concatenate([tile, tile], axis=0)
            tile = _rel_bias_tile(rb_ref, h, qn, kn)
            bias_n[h] = jnp.concatenate([tile, tile], axis=0)

    lam = _lambda(l1q_ref, l1k_ref, l2q_ref, l2k_ref, lam_init)
    v_heads = jnp.transpose(vp_ref[0], (1, 0, 2))
    for h in range(DA_HEADS):
        cols = slice(h * hd, (h + 1) * hd)
        qs = _stack_maps(q_ref[:, cols])
        s_p = dot(qs, kp_ref[0, cols, :].astype(BF16)) + bias_p[h]
        s_n = nt(qs, kn_ref[:, cols]) + bias_n[h]
        m = jnp.maximum(jnp.max(s_p, axis=-1, keepdims=True), jnp.max(s_n, axis=-1, keepdims=True))
        p_p = jnp.exp2(s_p - m)
        p_n = jnp.exp2(s_n - m)
        l = jnp.sum(p_p, axis=-1, keepdims=True) + jnp.sum(p_n, axis=-1, keepdims=True)
        acc = (dot(p_p.astype(BF16), v_heads[h].astype(BF16))
               + dot(p_n.astype(BF16), vn_ref[:, cols]))
        o_ref[:, cols] = _diff_finish(acc, l, lam, g_ref[...], lam_init).astype(BF16)


def _sample_attn(rel_bias, q, cache_kt, cache_v, k_new, v_new, lams, g_head, *, batch, seq, lam_init):
    past = cache_kt.shape[2]
    hd = 2 * DA_DH
    lam_spec = _const_spec((1, DA_DH))
    new_spec = pl.BlockSpec((seq, DA_Q), lambda b: (b, 0))
    return pl.pallas_call(
        functools.partial(_sample_attn_kernel, past=past, seq=seq, lam_init=lam_init),
        grid=(batch,),
        in_specs=[pl.BlockSpec(memory_space=pltpu.SMEM), new_spec,
                  pl.BlockSpec((1, DA_Q, past), lambda b: (b, 0, 0)),
                  pl.BlockSpec((1, past, DA_HEADS, hd), lambda b: (b, 0, 0, 0)),
                  new_spec, new_spec, lam_spec, lam_spec, lam_spec, lam_spec, _const_spec((1, hd))],
        out_specs=new_spec,
        out_shape=jax.ShapeDtypeStruct((batch * seq, DA_Q), BF16),
        scratch_shapes=[pltpu.VMEM((DA_HEADS, 2 * seq, past), F32), pltpu.VMEM((DA_HEADS, 2 * seq, seq), F32)],
        compiler_params=_params(1), name="diff_attn_sample",
    )(rel_bias, q, cache_kt, cache_v, k_new, v_new, *lams, g_head)


def _prep_weights(norm_g, mlstm_w_in, mlstm_b_gate, mlstm_g_head, mlstm_w_out, kv_g, kv_w, diff_w_q,
                  diff_lam_q1, diff_lam_k1, diff_lam_q2, diff_lam_k2, diff_g_head, diff_w_o,
                  ffn_w_in, ffn_w_out):
    n_main = 2 * M_QK + M_V + D_MODEL
    w_in = mlstm_w_in[0]
    wg = jnp.pad(w_in[:, n_main:], ((0, 0), (0, LANES - 2 * M_HEADS)))
    wg_hi = wg.astype(BF16)
    wg_lo = (wg - wg_hi.astype(F32)).astype(BF16)
    row = lambda a: a.reshape(1, -1)
    return dict(
        norm=[[row(norm_g[l, j]) for j in range(4)] for l in range(2)],
        w_main=w_in.astype(BF16),
        w_gate=jnp.concatenate([wg_hi, wg_lo], axis=1),
        b_gate=jnp.pad(mlstm_b_gate[0], (0, LANES - 2 * M_HEADS)).reshape(1, LANES),
        g_mhead=row(mlstm_g_head[0]),
        w_mout=mlstm_w_out[0].astype(BF16),
        kv_g=row(kv_g), kv_w=kv_w.astype(BF16), w_q=diff_w_q[0].astype(BF16),
        lams=[row(diff_lam_q1[0]), row(diff_lam_k1[0]), row(diff_lam_q2[0]), row(diff_lam_k2[0])],
        g_dhead=row(diff_g_head[0]), w_o=diff_w_o[0].astype(BF16),
        ffn_in=ffn_w_in.astype(BF16), ffn_out=ffn_w_out.astype(BF16),
    )


def _trunk(x, c0, n0, m0, past_k, past_v, rel_bias, w, *, row_tile, ffn_tile, chunk, attn_tile):
    batch, seq, _ = x.shape
    rows = batch * seq
    x2d = x.reshape(rows, D_MODEL)
    lam_init = 0.8 - 0.6 * math.exp(-0.3 * 1)

    m0b = jnp.broadcast_to(m0[..., None], (batch, M_HEADS, LANES))
    x1, c_new, n_new, m_new = _mlstm(x2d, w["norm"][0][0], w["w_main"], w["w_gate"], w["b_gate"], c0, n0, m0b,
                                     w["w_mout"], w["g_mhead"], w["norm"][0][1], batch=batch, chunk=chunk)
    x2 = _ffn(x1, w["norm"][0][2], w["ffn_in"], w["ffn_out"], w["norm"][0][3], layer=0, tile=ffn_tile)

    aq, k_f32, v_f32, k_bf, v_bf = _qkv(x2, w["norm"][1][0], w["kv_g"], w["w_q"], w["kv_w"], tile=row_tile,
                                        batch=batch, k_transposed=past_k is None)
    if past_k is None:
        attn = _prompt_attn(rel_bias, aq, k_bf, v_bf, w["lams"], w["g_dhead"],
                            batch=batch, seq=seq, tile=attn_tile, lam_init=lam_init)
        k_f32 = k_f32.reshape(batch, DA_HEADS, 2, DA_DH, seq).transpose(0, 4, 1, 2, 3)
    else:
        past = past_k.shape[1]
        past_kt = past_k.transpose(0, 2, 3, 4, 1).reshape(batch, DA_Q, past)
        attn = _sample_attn(rel_bias, aq, past_kt, past_v, k_bf, v_bf, w["lams"], w["g_dhead"],
                            batch=batch, seq=seq, lam_init=lam_init)
    y = _ffn(x2, w["norm"][1][2], w["ffn_in"], w["ffn_out"], w["norm"][1][3], layer=1, tile=ffn_tile,
             attn=attn, w_o=w["w_o"], g_pre=w["norm"][1][1])

    return (y.reshape(batch, seq, D_MODEL), c_new[None], n_new[None], m_new[None, :, :, 0],
            k_f32.reshape(batch, seq, DA_HEADS, 2, DA_DH), v_f32.reshape(batch, seq, DA_HEADS, 2 * DA_DH))


def kernel(x_prompt, x_sample, state_C, state_n, state_m, cache_k, cache_v, norm_g, mlstm_w_in, mlstm_b_gate, mlstm_g_head, mlstm_w_out, kv_g, kv_w, rel_bias, diff_w_q, diff_lam_q1, diff_lam_k1, diff_lam_q2, diff_lam_k2, diff_g_head, diff_w_o, ffn_w_in, ffn_w_out):
    w = _prep_weights(norm_g, mlstm_w_in, mlstm_b_gate, mlstm_g_head, mlstm_w_out, kv_g, kv_w, diff_w_q,
                      diff_lam_q1, diff_lam_k1, diff_lam_q2, diff_lam_k2, diff_g_head, diff_w_o,
                      ffn_w_in, ffn_w_out)
    pb = x_prompt.shape[0]
    zeros = lambda *s: jnp.zeros(s, F32)
    y_p, p_c, p_n, p_m, p_k, p_v = _trunk(
        x_prompt, zeros(pb, M_HEADS, M_DV, M_DK), zeros(pb, M_HEADS, M_DK), zeros(pb, M_HEADS),
        None, None, rel_bias, w, row_tile=512, ffn_tile=1024, chunk=256, attn_tile=256)
    sb, ss, _ = x_sample.shape
    y_s, s_c, s_n, s_m, s_k, s_v = _trunk(
        x_sample, state_C[0], state_n[0], state_m[0], cache_k, cache_v, rel_bias, w,
        row_tile=sb * ss, ffn_tile=sb * ss, chunk=ss, attn_tile=None)
    return (y_p, y_s, p_c, p_n, p_m, p_k, p_v, s_c, s_n, s_m, s_k, s_v)
```

```python
import functools
import math

import jax
import jax.numpy as jnp
from jax import lax
from jax.experimental import pallas as pl
from jax.experimental.pallas import tpu as pltpu

F32 = jnp.float32
BF16 = jnp.bfloat16

D_MODEL = 1024
ATTN_CHUNK = 64
M_HEADS = 4
M_DK = D_MODEL // (2 * M_HEADS)
M_DV = D_MODEL // M_HEADS
M_QK = M_HEADS * M_DK
M_V = M_HEADS * M_DV
DA_HEADS = 8
DA_DH = D_MODEL // (2 * DA_HEADS)
DA_Q = DA_HEADS * 2 * DA_DH
FFN_HIDDEN = -(-8 * D_MODEL // (3 * 256)) * 256
N_BUCKETS = 32
MAX_DISTANCE = 128
EPS = 1e-6
LOG2E = math.log2(math.e)

LANES = 128
GATE_ROWS = 16
VMEM_LIMIT = 56 * 1024 * 1024
FAR_GROUP = 4
FFN_GROUP_ROWS = 512
FFN_COL_CHUNKS = (1024, 1024, 768)
assert sum(FFN_COL_CHUNKS) == FFN_HIDDEN


def _bucket_upper_bounds():
    half = N_BUCKETS // 2
    max_exact = half // 2
    ratio = MAX_DISTANCE // max_exact
    steps = half - max_exact
    bounds = [n + 1 for n in range(max_exact)]
    for k in range(1, steps):
        n = max_exact
        while n ** steps < (max_exact ** steps) * (ratio ** k):
            n += 1
        bounds.append(n)
    return bounds


_BUCKET_BOUNDS = _bucket_upper_bounds()


def _rms(x, g):
    return x * lax.rsqrt(jnp.mean(x * x, axis=-1, keepdims=True) + EPS) * g


def _log_sigmoid(x):
    return jnp.minimum(x, 0.0) - jnp.log1p(jnp.exp(-jnp.abs(x)))


def _split3(a):
    a1 = a.astype(BF16)
    r1 = a - a1.astype(F32)
    a2 = r1.astype(BF16)
    a3 = (r1 - a2.astype(F32)).astype(BF16)
    return a1, a2, a3


def _const_spec(shape):
    return pl.BlockSpec(shape, lambda *_: (0,) * len(shape), pipeline_mode=pl.Buffered(1))


def _params(n_axes):
    return pltpu.CompilerParams(dimension_semantics=("arbitrary",) * n_axes,
                                vmem_limit_bytes=VMEM_LIMIT)


def _mlstm_kernel(x_ref, xnext_ref, g_ref, w_ref, wg_ref, bg_ref, c0_ref, n0_ref, m0_ref,
                  wout_ref, ghead_ref, gpost_ref,
                  x1_ref, cout_ref, nout_ref, mout_ref,
                  q_s, k_s, v_s, o_s, gcol_s, grow_s, bcol_s, brow_s, ct_s, m_s, hcat_s,
                  *, chunk, chunks_per_step, n_steps):
    L = chunk
    c = pl.program_id(1)
    dot = functools.partial(jnp.dot, preferred_element_type=F32)

    def wide(rep, n):
        return rep[:, :n] if n < LANES else jnp.concatenate([rep] * (n // LANES), axis=1)

    def projection(load_x, slot):
        xs = {}
        half = M_V // 2
        v0, o0 = 2 * M_QK, 2 * M_QK + M_V

        def queries():
            xn = _rms(load_x(), g_ref[...])
            xs["hi"] = xn.astype(BF16)
            xs["lo"] = (xn - xs["hi"].astype(F32)).astype(BF16)
            q_s[slot] = dot(xs["hi"], w_ref[:, :M_QK]).astype(BF16)

        def keys():
            k_s[slot] = (dot(xs["hi"], w_ref[:, M_QK:2 * M_QK]) * (M_DK ** -0.5)).astype(BF16)

        def values(part):
            def step():
                cols = slice(part * half, (part + 1) * half)
                v_s[slot, :, cols] = dot(xs["hi"], w_ref[:, v0 + part * half:v0 + (part + 1) * half]).astype(BF16)
            return step

        def out_gate(part):
            def step():
                cols = slice(part * half, (part + 1) * half)
                o_s[slot, :, cols] = dot(xs["hi"], w_ref[:, o0 + part * half:o0 + (part + 1) * half])
            return step

        def gates():
            gh = dot(xs["hi"], wg_ref[...])
            gl = dot(xs["lo"], wg_ref[:, :LANES])
            pre = gh[:, :LANES] + gh[:, LANES:] + gl + bg_ref[...]
            lane = lax.broadcasted_iota(jnp.int32, pre.shape, 1)
            gv = jnp.where(lane < M_HEADS, pre, _log_sigmoid(pre))
            gcol_s[slot] = gv
            if L < LANES:
                gv = jnp.concatenate([gv, jnp.zeros((LANES - L, LANES), F32)], axis=0)
            grow_s[slot] = gv.T[:GATE_ROWS, :L]

        def cumulate():
            bcol_s[slot] = sum(dot(tri, part) for part in _split3(gcol_s[slot]))
            brow_s[slot] = sum(dot(part, tri_t) for part in _split3(grow_s[slot]))

        return [queries, gates, cumulate, keys, values(0), values(1), out_gate(0), out_gate(1)]

    row = lax.broadcasted_iota(jnp.int32, (L, L), 0)
    col = lax.broadcasted_iota(jnp.int32, (L, L), 1)
    causal = col <= row
    tri = jnp.where(causal, 1.0, 0.0).astype(BF16)
    tri_t = jnp.where(row <= col, 1.0, 0.0).astype(BF16)
    ones = jnp.ones((L, LANES), BF16)

    def scan(slot, rows, fillers):
        fillers = list(fillers)

        def fill():
            if fillers:
                fillers.pop(0)()

        gcol = gcol_s[slot]
        grow = grow_s[slot]
        bcol = bcol_s[slot]
        brow = brow_s[slot]

        for h in range(M_HEADS):
            q = q_s[slot, :, h * M_DK:(h + 1) * M_DK]
            k = k_s[slot, :, h * M_DK:(h + 1) * M_DK]
            v = v_s[slot, :, h * M_DV:(h + 1) * M_DV]
            ig_row = grow[h:h + 1, :]
            b_row = brow[M_HEADS + h:M_HEADS + h + 1, :]
            ig = jnp.broadcast_to(gcol[:, h:h + 1], (L, LANES))
            b = jnp.broadcast_to(bcol[:, M_HEADS + h:M_HEADS + h + 1], (L, LANES))
            m0 = m_s[h:h + 1, :]
            ct = ct_s[h]

            d = jnp.where(causal, wide(b, L) - b_row + ig_row, -jnp.inf)
            g = b + m0
            m = jnp.maximum(g, jnp.max(d, axis=-1, keepdims=True))
            w_inter = jnp.exp(g - m)
            qk = lax.dot_general(q, k, (((1,), (1,)), ((), ())), preferred_element_type=F32)
            s = (jnp.exp(d - wide(m, L)) * qk).astype(BF16)
            inter = dot(q, ct.astype(BF16))
            fill()
            num = wide(w_inter, M_DV) * inter[:, :M_DV] + dot(s, v)
            den = w_inter * inter[:, M_DV:] + dot(s, ones)
            inv = 1.0 / jnp.maximum(jnp.abs(den), jnp.exp(-m))
            hh = num * wide(inv, M_DV)

            m_last = m[L - 1:L, :]
            w_state = jnp.exp(g[L - 1:L, :] - m_last)
            w_rows = jnp.exp(b[L - 1:L, :] - b + ig - m_last)
            vw = jnp.concatenate([(v.astype(F32) * wide(w_rows, M_DV)).astype(BF16), w_rows.astype(BF16)], axis=1)
            upd = lax.dot_general(k, vw, (((0,), (0,)), ((), ())), preferred_element_type=F32)
            ct_s[h] = wide(w_state, M_DV + LANES) * ct + upd
            m_s[h:h + 1, :] = m_last

            hn = _rms(hh, ghead_ref[:, h * M_DV:(h + 1) * M_DV])
            gate = jax.nn.sigmoid(o_s[slot, :, h * M_DV:(h + 1) * M_DV])
            hcat_s[slot, :, h * M_DV:(h + 1) * M_DV] = (hn * gate).astype(BF16)
            fill()
        while fillers:
            fill()

        y = dot(hcat_s[slot], wout_ref[...])
        x1_ref[0, rows, :] = x_ref[0, rows, :] + _rms(y, gpost_ref[...])

    @pl.when(c == 0)
    def _():
        for h in range(M_HEADS):
            ct_s[h, :, :M_DV] = c0_ref[0, h].T
            ct_s[h, :, M_DV:] = jnp.broadcast_to(n0_ref[0, h:h + 1, :], (LANES, M_DK)).T
        m_s[...] = m0_ref[0]
        for step in projection(lambda: x_ref[0, :L, :], 0):
            step()

    for j in range(chunks_per_step):
        rows = slice(j * L, (j + 1) * L)
        if j + 1 < chunks_per_step:
            following = projection(lambda j=j: x_ref[0, (j + 1) * L:(j + 2) * L, :], (j + 1) % 2)
        elif n_steps > 1:
            following = projection(lambda: xnext_ref[0], (j + 1) % 2)
        else:
            following = []
        scan(j % 2, rows, following)

    @pl.when(c == n_steps - 1)
    def _():
        for h in range(M_HEADS):
            t = ct_s[h].T
            cout_ref[0, h] = t[:M_DV]
            nout_ref[0, h:h + 1, :] = t[M_DV:M_DV + 1]
        mout_ref[0] = m_s[...]


def _mlstm(x2d, g_in, w_main, w_gate, b_gate, c0, n0, m0b, w_out, g_head, g_post, *, batch, chunk):
    rows = x2d.shape[0]
    seq = rows // batch
    n_chunks = seq // chunk
    cps = 2 if n_chunks % 2 == 0 else 1
    assert cps == 2 or n_chunks == 1
    n_steps = n_chunks // cps
    x3d = x2d.reshape(batch, seq, D_MODEL)
    step_spec = pl.BlockSpec((1, cps * chunk, D_MODEL), lambda b, c: (b, c, 0))
    next_spec = pl.BlockSpec((1, chunk, D_MODEL), lambda b, c: (b, jnp.minimum(cps * (c + 1), n_chunks - 1), 0))
    state4 = pl.BlockSpec((1, M_HEADS, M_DV, M_DK), lambda b, c: (b, 0, 0, 0))
    state3 = pl.BlockSpec((1, M_HEADS, LANES), lambda b, c: (b, 0, 0))
    vec = _const_spec((1, D_MODEL))
    x1, c_new, n_new, m_new = pl.pallas_call(
        functools.partial(_mlstm_kernel, chunk=chunk, chunks_per_step=cps, n_steps=n_steps),
        grid=(batch, n_steps),
        in_specs=[step_spec, next_spec, vec, _const_spec(w_main.shape), _const_spec(w_gate.shape),
                  _const_spec((1, LANES)), state4, state3, state3,
                  _const_spec((M_V, D_MODEL)), _const_spec((1, M_V)), vec],
        out_specs=[step_spec, state4, state3, state3],
        out_shape=[jax.ShapeDtypeStruct((batch, seq, D_MODEL), F32),
                   jax.ShapeDtypeStruct((batch, M_HEADS, M_DV, M_DK), F32),
                   jax.ShapeDtypeStruct((batch, M_HEADS, LANES), F32),
                   jax.ShapeDtypeStruct((batch, M_HEADS, LANES), F32)],
        scratch_shapes=[pltpu.VMEM((2, chunk, M_QK), BF16), pltpu.VMEM((2, chunk, M_QK), BF16),
                        pltpu.VMEM((2, chunk, M_V), BF16), pltpu.VMEM((2, chunk, D_MODEL), F32),
                        pltpu.VMEM((2, chunk, LANES), F32), pltpu.VMEM((2, GATE_ROWS, chunk), F32),
                        pltpu.VMEM((2, chunk, LANES), F32), pltpu.VMEM((2, GATE_ROWS, chunk), F32),
                        pltpu.VMEM((M_HEADS, M_DK, M_DV + LANES), F32), pltpu.VMEM((M_HEADS, LANES), F32),
                        pltpu.VMEM((2, chunk, M_V), BF16)],
        compiler_params=_params(2),
        name="mlstm",
    )(x3d, x3d, g_in, w_main, w_gate, b_gate, c0, n0, m0b, w_out, g_head, g_post)
    return x1.reshape(rows, D_MODEL), c_new, n_new, m_new


def _ffn_tile(load_x, gin_ref, win_ref, wout_ref, gout_ref, y_ref):
    dot = functools.partial(jnp.dot, preferred_element_type=F32)
    rows = y_ref.shape[0]
    n = rows // FFN_GROUP_ROWS if rows % FFN_GROUP_ROWS == 0 else 1
    groups = [slice(r * rows // n, (r + 1) * rows // n) for r in range(n)]

    def prologue(r):
        x = load_x(groups[r])
        return x, _rms(x, gin_ref[...]).astype(BF16)

    def epilogue(r, x, acc):
        y_ref[groups[r], :] = x + _rms(acc, gout_ref[...])

    x, xn = prologue(0)
    pending = None
    for r in range(n):
        acc = None
        start = 0
        nxt = None
        for j, width in enumerate(FFN_COL_CHUNKS):
            gate = dot(xn, win_ref[:, start:start + width])
            up = dot(xn, win_ref[:, FFN_HIDDEN + start:FFN_HIDDEN + start + width])
            act = (jax.nn.silu(gate) * up).astype(BF16)
            part = dot(act, wout_ref[start:start + width, :])
            acc = part if acc is None else acc + part
            start += width
            if j == 0:
                if pending is not None:
                    epilogue(*pending)
                if r + 1 < n:
                    nxt = prologue(r + 1)
        pending = (r, x, acc)
        if nxt is not None:
            x, xn = nxt
    epilogue(*pending)


def _ffn_kernel(x_ref, gin_ref, win_ref, wout_ref, gout_ref, y_ref):
    _ffn_tile(lambda rows: x_ref[rows, :], gin_ref, win_ref, wout_ref, gout_ref, y_ref)


def _attn_out_ffn_kernel(x_ref, a_ref, wo_ref, gpre_ref, gin_ref, win_ref, wout_ref, gout_ref, y_ref):
    def load_x(rows):
        mix = jnp.dot(a_ref[rows, :], wo_ref[...], preferred_element_type=F32)
        return x_ref[rows, :] + _rms(mix, gpre_ref[...])

    _ffn_tile(load_x, gin_ref, win_ref, wout_ref, gout_ref, y_ref)


def _ffn(x2d, g_in, w_in, w_out, g_out, *, layer, tile, attn=None, w_o=None, g_pre=None):
    rows = x2d.shape[0]
    row_spec = lambda cols: pl.BlockSpec((tile, cols), lambda i: (i, 0))
    vec = _const_spec((1, D_MODEL))
    layer_spec = lambda w: pl.BlockSpec((None,) + w.shape[1:], lambda i: (layer, 0, 0),
                                        pipeline_mode=pl.Buffered(1))
    ffn_specs = [vec, layer_spec(w_in), layer_spec(w_out), vec]
    if attn is None:
        kern, in_specs = _ffn_kernel, [row_spec(D_MODEL)] + ffn_specs
        args = (x2d, g_in, w_in, w_out, g_out)
    else:
        kern = _attn_out_ffn_kernel
        in_specs = [row_spec(D_MODEL), row_spec(DA_Q), _const_spec(w_o.shape), vec] + ffn_specs
        args = (x2d, attn, w_o, g_pre, g_in, w_in, w_out, g_out)
    return pl.pallas_call(
        kern, grid=(rows // tile,), in_specs=in_specs, out_specs=row_spec(D_MODEL),
        out_shape=jax.ShapeDtypeStruct((rows, D_MODEL), F32),
        compiler_params=_params(1), name="ffn",
    )(*args)


def _qkv_kernel(x_ref, gq_ref, gkv_ref, wq_ref, wkv_ref, q_ref, kf_ref, vf_ref, kb_ref, vb_ref, *, k_transposed):
    x = x_ref[...]
    xhat = x * lax.rsqrt(jnp.mean(x * x, axis=-1, keepdims=True) + EPS)
    xq = (xhat * gq_ref[...]).astype(BF16)
    xkv = (xhat * gkv_ref[...]).astype(BF16)
    dot = functools.partial(jnp.dot, preferred_element_type=F32)
    q_ref[...] = (dot(xq, wq_ref[...]) * (DA_DH ** -0.5 * LOG2E)).astype(BF16)
    kk = dot(xkv, wkv_ref[:, :DA_Q])
    if k_transposed:
        kt = kk.T
        kf_ref[0] = kt
        kb_ref[0] = kt.astype(BF16)
    else:
        kf_ref[...] = kk
        kb_ref[...] = kk.astype(BF16)
    vv = dot(xkv, wkv_ref[:, DA_Q:])
    vf_ref[...] = vv
    vb_ref[...] = vv.astype(BF16)


def _qkv(x2d, g_q, g_kv, w_q, w_kv, *, tile, batch, k_transposed):
    rows = x2d.shape[0]
    seq = rows // batch
    row_spec = pl.BlockSpec((tile, D_MODEL), lambda i: (i, 0))
    vec = _const_spec((1, D_MODEL))
    shp = lambda dt: jax.ShapeDtypeStruct((rows, DA_Q), dt)
    if k_transposed:
        tiles_per_seq = seq // tile
        k_spec = pl.BlockSpec((1, DA_Q, tile), lambda i: (i // tiles_per_seq, 0, i % tiles_per_seq))
        k_shp = lambda dt: jax.ShapeDtypeStruct((batch, DA_Q, seq), dt)
    else:
        k_spec, k_shp = row_spec, shp
    return pl.pallas_call(
        functools.partial(_qkv_kernel, k_transposed=k_transposed), grid=(rows // tile,),
        in_specs=[row_spec, vec, vec, _const_spec(w_q.shape), _const_spec(w_kv.shape)],
        out_specs=[row_spec, k_spec, row_spec, k_spec, row_spec],
        out_shape=[shp(BF16), k_shp(F32), shp(F32), k_shp(BF16), shp(BF16)],
        compiler_params=_params(1), name="attn_qkv",
    )(x2d, g_q, g_kv, w_q, w_kv)


def _rel_bias_tile(rb_ref, head, q_pos, k_pos):
    rel = k_pos - q_pos
    n = jnp.abs(rel)
    half = N_BUCKETS // 2

    def table(offset):
        val = jnp.full(rel.shape, rb_ref[offset + half - 1, head], F32)
        for bucket in reversed(range(half - 1)):
            val = jnp.where(n < _BUCKET_BOUNDS[bucket], rb_ref[offset + bucket, head], val)
        return val

    bias = jnp.where(rel > 0, table(half), table(0)) * LOG2E
    shift = ATTN_CHUNK.bit_length() - 1
    visible = jnp.right_shift(k_pos, shift) <= jnp.right_shift(q_pos, shift)
    return jnp.where(visible, bias, -jnp.inf)


def _lambda(l1q_ref, l1k_ref, l2q_ref, l2k_ref, lam_init):
    a = jnp.exp(jnp.sum(l1q_ref[...] * l1k_ref[...], axis=-1, keepdims=True))
    b = jnp.exp(jnp.sum(l2q_ref[...] * l2k_ref[...], axis=-1, keepdims=True))
    return a - b + lam_init


def _stack_maps(q):
    lane = lax.broadcasted_iota(jnp.int32, q.shape, 1)
    zero = jnp.zeros_like(q)
    return jnp.concatenate([jnp.where(lane < DA_DH, q, zero), jnp.where(lane >= DA_DH, q, zero)], axis=0)


def _diff_finish(acc, l, lam, g, lam_init):
    t = acc.shape[0] // 2
    o = acc[:t] / l[:t] - lam * (acc[t:] / l[t:])
    return _rms(o, g) * (1.0 - lam_init)


def _largest_divisor(n, candidates):
    return max(u for u in candidates if n % u == 0)


def _prompt_attn_kernel(rb_ref, q_ref, k_ref, v_ref, l1q_ref, l1k_ref, l2q_ref, l2k_ref, g_ref,
                        o_ref, bias_s, m_s, acc_s, snext_s, *, tile, n_tiles, lam_init):
    T = tile
    hd = 2 * DA_DH
    h = pl.program_id(0)
    b = pl.program_id(1)
    dot = functools.partial(jnp.dot, preferred_element_type=F32)

    @pl.when(b == 0)
    def _():
        qp = lax.broadcasted_iota(jnp.int32, (T, 2 * T), 0) + T
        kp = lax.broadcasted_iota(jnp.int32, (T, 2 * T), 1)
        near = _rel_bias_tile(rb_ref, h, qp, kp)
        bias_s[...] = jnp.concatenate([near, near], axis=0)

    far_bias = rb_ref[N_BUCKETS // 2 - 1, h] * LOG2E
    lane = lax.broadcasted_iota(jnp.int32, (T, hd), 1)

    def update(i, key_block, n_blocks, bias):
        apply(i, scores(i, key_block, n_blocks), key_block, n_blocks, bias)

    def scores(i, key_block, n_blocks):
        q = q_ref[pl.ds(pl.multiple_of(i * T, T), T), :]
        kb = k_ref[0, :, pl.ds(pl.multiple_of(key_block * T, T), n_blocks * T)]
        zero = jnp.zeros_like(q)
        return jnp.concatenate([dot(jnp.where(lane < DA_DH, q, zero), kb),
                                dot(jnp.where(lane >= DA_DH, q, zero), kb)], axis=0)

    def apply(i, s, key_block, n_blocks, bias):
        vb = v_ref[pl.ds(pl.multiple_of(key_block * T, T), n_blocks * T), :]
        vext = jnp.concatenate([vb, jnp.ones_like(vb)], axis=1)
        first = bias is not None
        if first:
            s = s + bias
            m_new = jnp.broadcast_to(jnp.max(s, axis=-1, keepdims=True), (2 * T, LANES))
        else:
            m_prev = m_s[i]
            m_new = jnp.maximum(m_prev, jnp.max(s, axis=-1, keepdims=True))
        p = jnp.exp2(s - jnp.concatenate([m_new] * (n_blocks * T // LANES), axis=1))
        pv = jnp.dot(p.astype(BF16), vext, preferred_element_type=F32)
        if first:
            acc_s[i] = pv
            m_s[i] = m_new - far_bias
        else:
            alpha = jnp.exp2(m_prev - m_new)
            acc_s[i] = jnp.concatenate([alpha, alpha], axis=1) * acc_s[i] + pv
            m_s[i] = m_new

    update(0, 0, 1, bias_s[:, T:])
    near_unroll = _largest_divisor(n_tiles - 1, (15, 5, 3, 2, 1))

    last_tile = n_tiles - 1
    if n_tiles > 1:
        snext_s[:, :2 * T] = scores(1, 0, 2)

    def near_body(it, carry):
        s = snext_s[:, :2 * T]
        for u in range(near_unroll):
            i = 1 + it * near_unroll + u
            i_next = jnp.minimum(i + 1, last_tile)
            s_next = scores(i_next, i_next - 1, 2)
            apply(i, s, i - 1, 2, bias_s[...])
            s = s_next
        snext_s[:, :2 * T] = s
        return carry

    lax.fori_loop(0, (n_tiles - 1) // near_unroll, near_body, 0)

    rest = []
    for i in range(2, n_tiles):
        n_far = i - 1
        if n_far % 2:
            rest.append((i, n_far - 1, 1))
        if n_far % FAR_GROUP >= 2:
            rest.append((i, n_far - n_far % FAR_GROUP, 2))
    s = scores(*rest[0]) if rest else None
    for idx, (i, key_block, n_blocks) in enumerate(rest):
        s_next = scores(*rest[idx + 1]) if idx + 1 < len(rest) else None
        apply(i, s, key_block, n_blocks, None)
        s = s_next
    n_group = sum((i - 1) // FAR_GROUP for i in range(1, n_tiles))
    far_unroll = _largest_divisor(n_group, (7, 5, 4, 3, 2, 1))
    shift = FAR_GROUP.bit_length() - 1

    def far_body(_, carry):
        i, t = carry
        s = snext_s[...]
        for _u in range(far_unroll):
            wrap = t + 1 >= jnp.right_shift(i - 1, shift)
            i_next, t_next = jnp.where(wrap, i + 1, i), jnp.where(wrap, 0, t + 1)
            t_next = jnp.where(i_next > last_tile, 0, t_next)
            i_next = jnp.minimum(i_next, last_tile)
            s_next = scores(i_next, FAR_GROUP * t_next, FAR_GROUP)
            apply(i, s, FAR_GROUP * t, FAR_GROUP, None)
            s, i, t = s_next, i_next, t_next
        snext_s[...] = s
        return i, t

    if n_group:
        snext_s[...] = scores(FAR_GROUP + 1, 0, FAR_GROUP)
        lax.fori_loop(0, n_group // far_unroll, far_body, (jnp.int32(FAR_GROUP + 1), jnp.int32(0)))

    lam = _lambda(l1q_ref, l1k_ref, l2q_ref, l2k_ref, lam_init)
    for i in range(n_tiles):
        acc = acc_s[i]
        o_ref[i * T:(i + 1) * T, :] = _diff_finish(acc[:, :hd], acc[:, hd:], lam, g_ref[...], lam_init).astype(BF16)


def _prompt_attn(rel_bias, q, k, v, lams, g_head, *, batch, seq, tile, lam_init):
    assert tile >= MAX_DISTANCE and tile % ATTN_CHUNK == 0 and seq % tile == 0
    nq = seq // tile
    hd = 2 * DA_DH
    lam_spec = _const_spec((1, DA_DH))
    seq_spec = pl.BlockSpec((seq, hd), lambda h, b: (b, h))
    feat_spec = pl.BlockSpec((1, hd, seq), lambda h, b: (b, h, 0))
    return pl.pallas_call(
        functools.partial(_prompt_attn_kernel, tile=tile, n_tiles=nq, lam_init=lam_init),
        grid=(DA_HEADS, batch),
        in_specs=[pl.BlockSpec(memory_space=pltpu.SMEM), seq_spec,
                  feat_spec, seq_spec,
                  lam_spec, lam_spec, lam_spec, lam_spec, _const_spec((1, hd))],
        out_specs=seq_spec,
        out_shape=jax.ShapeDtypeStruct((batch * seq, DA_Q), BF16),
        scratch_shapes=[pltpu.VMEM((2 * tile, 2 * tile), F32),
                        pltpu.VMEM((nq, 2 * tile, LANES), F32), pltpu.VMEM((nq, 2 * tile, 2 * hd), F32),
                        pltpu.VMEM((2 * tile, FAR_GROUP * tile), F32)],
        compiler_params=_params(2), name="diff_attn_prompt",
    )(rel_bias, q, k, v, *lams, g_head)


def _sample_attn_kernel(rb_ref, q_ref, kp_ref, vp_ref, kn_ref, vn_ref, l1q_ref, l1k_ref, l2q_ref, l2k_ref,
                        g_ref, o_ref, bias_p, bias_n, *, past, seq, lam_init):
    b = pl.program_id(0)
    hd = 2 * DA_DH
    dot = functools.partial(jnp.dot, preferred_element_type=F32)
    nt = lambda a, bb: lax.dot_general(a, bb, (((1,), (1,)), ((), ())), preferred_element_type=F32)

    @pl.when(b == 0)
    def _():
        qp = lax.broadcasted_iota(jnp.int32, (seq, past), 0) + past
        kp = lax.broadcasted_iota(jnp.int32, (seq, past), 1)
        qn = lax.broadcasted_iota(jnp.int32, (seq, seq), 0) + past
        kn = lax.broadcasted_iota(jnp.int32, (seq, seq), 1) + past
        for h in range(DA_HEADS):
            tile = _rel_bias_tile(rb_ref, h, qp, kp)
            bias_p[h] = jnp.concatenate([tile, tile], axis=0)
            tile = _rel_bias_tile(rb_ref, h, qn, kn)
            bias_n[h] = jnp.concatenate([tile, tile], axis=0)

    lam = _lambda(l1q_ref, l1k_ref, l2q_ref, l2k_ref, lam_init)
    v_heads = jnp.transpose(vp_ref[0], (1, 0, 2))
    for h in range(DA_HEADS):
        cols = slice(h * hd, (h + 1) * hd)
        qs = _stack_maps(q_ref[:, cols])
        s_p = dot(qs, kp_ref[0, cols, :].astype(BF16)) + bias_p[h]
        s_n = nt(qs, kn_ref[:, cols]) + bias_n[h]
        m = jnp.maximum(jnp.max(s_p, axis=-1, keepdims=True), jnp.max(s_n, axis=-1, keepdims=True))
        p_p = jnp.exp2(s_p - m)
        p_n = jnp.exp2(s_n - m)
        l = jnp.sum(p_p, axis=-1, keepdims=True) + jnp.sum(p_n, axis=-1, keepdims=True)
        acc = (dot(p_p.astype(BF16), v_heads[h].astype(BF16))
               + dot(p_n.astype(BF16), vn_ref[:, cols]))
        o_ref[:, cols] = _diff_finish(acc, l, lam, g_ref[...], lam_init).astype(BF16)


def _sample_attn(rel_bias, q, cache_kt, cache_v, k_new, v_new, lams, g_head, *, batch, seq, lam_init):
    past = cache_kt.shape[2]
    hd = 2 * DA_DH
    lam_spec = _const_spec((1, DA_DH))
    new_spec = pl.BlockSpec((seq, DA_Q), lambda b: (b, 0))
    return pl.pallas_call(
        functools.partial(_sample_attn_kernel, past=past, seq=seq, lam_init=lam_init),
        grid=(batch,),
        in_specs=[pl.BlockSpec(memory_space=pltpu.SMEM), new_spec,
                  pl.BlockSpec((1, DA_Q, past), lambda b: (b, 0, 0)),
                  pl.BlockSpec((1, past, DA_HEADS, hd), lambda b: (b, 0, 0, 0)),
                  new_spec, new_spec, lam_spec, lam_spec, lam_spec, lam_spec, _const_spec((1, hd))],
        out_specs=new_spec,
        out_shape=jax.ShapeDtypeStruct((batch * seq, DA_Q), BF16),
        scratch_shapes=[pltpu.VMEM((DA_HEADS, 2 * seq, past), F32), pltpu.VMEM((DA_HEADS, 2 * seq, seq), F32)],
        compiler_params=_params(1), name="diff_attn_sample",
    )(rel_bias, q, cache_kt, cache_v, k_new, v_new, *lams, g_head)


def _prep_weights(norm_g, mlstm_w_in, mlstm_b_gate, mlstm_g_head, mlstm_w_out, kv_g, kv_w, diff_w_q,
                  diff_lam_q1, diff_lam_k1, diff_lam_q2, diff_lam_k2, diff_g_head, diff_w_o,
                  ffn_w_in, ffn_w_out):
    n_main = 2 * M_QK + M_V + D_MODEL
    w_in = mlstm_w_in[0]
    wg = jnp.pad(w_in[:, n_main:], ((0, 0), (0, LANES - 2 * M_HEADS)))
    wg_hi = wg.astype(BF16)
    wg_lo = (wg - wg_hi.astype(F32)).astype(BF16)
    row = lambda a: a.reshape(1, -1)
    return dict(
        norm=[[row(norm_g[l, j]) for j in range(4)] for l in range(2)],
        w_main=w_in.astype(BF16),
        w_gate=jnp.concatenate([wg_hi, wg_lo], axis=1),
        b_gate=jnp.pad(mlstm_b_gate[0], (0, LANES - 2 * M_HEADS)).reshape(1, LANES),
        g_mhead=row(mlstm_g_head[0]),
        w_mout=mlstm_w_out[0].astype(BF16),
        kv_g=row(kv_g), kv_w=kv_w.astype(BF16), w_q=diff_w_q[0].astype(BF16),
        lams=[row(diff_lam_q1[0]), row(diff_lam_k1[0]), row(diff_lam_q2[0]), row(diff_lam_k2[0])],
        g_dhead=row(diff_g_head[0]), w_o=diff_w_o[0].astype(BF16),
        ffn_in=ffn_w_in.astype(BF16), ffn_out=ffn_w_out.astype(BF16),
    )


def _trunk(x, c0, n0, m0, past_k, past_v, rel_bias, w, *, row_tile, ffn_tile, chunk, attn_tile):
    batch, seq, _ = x.shape
    rows = batch * seq
    x2d = x.reshape(rows, D_MODEL)
    lam_init = 0.8 - 0.6 * math.exp(-0.3 * 1)

    m0b = jnp.broadcast_to(m0[..., None], (batch, M_HEADS, LANES))
    x1, c_new, n_new, m_new = _mlstm(x2d, w["norm"][0][0], w["w_main"], w["w_gate"], w["b_gate"], c0, n0, m0b,
                                     w["w_mout"], w["g_mhead"], w["norm"][0][1], batch=batch, chunk=chunk)
    x2 = _ffn(x1, w["norm"][0][2], w["ffn_in"], w["ffn_out"], w["norm"][0][3], layer=0, tile=ffn_tile)

    aq, k_f32, v_f32, k_bf, v_bf = _qkv(x2, w["norm"][1][0], w["kv_g"], w["w_q"], w["kv_w"], tile=row_tile,
                                        batch=batch, k_transposed=past_k is None)
    if past_k is None:
        attn = _prompt_attn(rel_bias, aq, k_bf, v_bf, w["lams"], w["g_dhead"],
                            batch=batch, seq=seq, tile=attn_tile, lam_init=lam_init)
        k_f32 = k_f32.reshape(batch, DA_HEADS, 2, DA_DH, seq).transpose(0, 4, 1, 2, 3)
    else:
        past = past_k.shape[1]
        past_kt = past_k.transpose(0, 2, 3, 4, 1).reshape(batch, DA_Q, past)
        attn = _sample_attn(rel_bias, aq, past_kt, past_v, k_bf, v_bf, w["lams"], w["g_dhead"],
                            batch=batch, seq=seq, lam_init=lam_init)
    y = _ffn(x2, w["norm"][1][2], w["ffn_in"], w["ffn_out"], w["norm"][1][3], layer=1, tile=ffn_tile,
             attn=attn, w_o=w["w_o"], g_pre=w["norm"][1][1])

    return (y.reshape(batch, seq, D_MODEL), c_new[None], n_new[None], m_new[None, :, :, 0],
            k_f32.reshape(batch, seq, DA_HEADS, 2, DA_DH), v_f32.reshape(batch, seq, DA_HEADS, 2 * DA_DH))


def kernel(x_prompt, x_sample, state_C, state_n, state_m, cache_k, cache_v, norm_g, mlstm_w_in, mlstm_b_gate, mlstm_g_head, mlstm_w_out, kv_g, kv_w, rel_bias, diff_w_q, diff_lam_q1, diff_lam_k1, diff_lam_q2, diff_lam_k2, diff_g_head, diff_w_o, ffn_w_in, ffn_w_out):
    w = _prep_weights(norm_g, mlstm_w_in, mlstm_b_gate, mlstm_g_head, mlstm_w_out, kv_g, kv_w, diff_w_q,
                      diff_lam_q1, diff_lam_k1, diff_lam_q2, diff_lam_k2, diff_g_head, diff_w_o,
                      ffn_w_in, ffn_w_out)
    pb = x_prompt.shape[0]
    zeros = lambda *s: jnp.zeros(s, F32)
    y_p, p_c, p_n, p_m, p_k, p_v = _trunk(
        x_prompt, zeros(pb, M_HEADS, M_DV, M_DK), zeros(pb, M_HEADS, M_DK), zeros(pb, M_HEADS),
        None, None, rel_bias, w, row_tile=512, ffn_tile=1024, chunk=256, attn_tile=256)
    sb, ss, _ = x_sample.shape
    y_s, s_c, s_n, s_m, s_k, s_v = _trunk(
        x_sample, state_C[0], state_n[0], state_m[0], cache_k, cache_v, rel_bias, w,
        row_tile=sb * ss, ffn_tile=sb * ss, chunk=ss, attn_tile=None)
    return (y_p, y_s, p_c, p_n, p_m, p_k, p_v, s_c, s_n, s_m, s_k, s_v)
```

```python
import functools
import math

import jax
import jax.numpy as jnp
from jax import lax
from jax.experimental import pallas as pl
from jax.experimental.pallas import tpu as pltpu

F32 = jnp.float32
BF16 = jnp.bfloat16

D_MODEL = 1024
ATTN_CHUNK = 64
M_HEADS = 4
M_DK = D_MODEL // (2 * M_HEADS)
M_DV = D_MODEL // M_HEADS
M_QK = M_HEADS * M_DK
M_V = M_HEADS * M_DV
DA_HEADS = 8
DA_DH = D_MODEL // (2 * DA_HEADS)
DA_Q = DA_HEADS * 2 * DA_DH
FFN_HIDDEN = -(-8 * D_MODEL // (3 * 256)) * 256
N_BUCKETS = 32
MAX_DISTANCE = 128
EPS = 1e-6
LOG2E = math.log2(math.e)

LANES = 128
GATE_ROWS = 16
VMEM_LIMIT = 56 * 1024 * 1024
FAR_GROUP = 4
FFN_GROUP_ROWS = 512
FFN_COL_CHUNKS = (1024, 1024, 768)
assert sum(FFN_COL_CHUNKS) == FFN_HIDDEN


def _bucket_upper_bounds():
    half = N_BUCKETS // 2
    max_exact = half // 2
    ratio = MAX_DISTANCE // max_exact
    steps = half - max_exact
    bounds = [n + 1 for n in range(max_exact)]
    for k in range(1, steps):
        n = max_exact
        while n ** steps < (max_exact ** steps) * (ratio ** k):
            n += 1
        bounds.append(n)
    return bounds


_BUCKET_BOUNDS = _bucket_upper_bounds()


def _rms(x, g):
    return x * lax.rsqrt(jnp.mean(x * x, axis=-1, keepdims=True) + EPS) * g


def _log_sigmoid(x):
    return jnp.minimum(x, 0.0) - jnp.log1p(jnp.exp(-jnp.abs(x)))


def _split3(a):
    a1 = a.astype(BF16)
    r1 = a - a1.astype(F32)
    a2 = r1.astype(BF16)
    a3 = (r1 - a2.astype(F32)).astype(BF16)
    return a1, a2, a3


def _const_spec(shape):
    return pl.BlockSpec(shape, lambda *_: (0,) * len(shape), pipeline_mode=pl.Buffered(1))


def _params(n_axes):
    return pltpu.CompilerParams(dimension_semantics=("arbitrary",) * n_axes,
                                vmem_limit_bytes=VMEM_LIMIT)


def _mlstm_kernel(x_ref, xnext_ref, g_ref, w_ref, wg_ref, bg_ref, c0_ref, n0_ref, m0_ref,
                  wout_ref, ghead_ref, gpost_ref,
                  x1_ref, cout_ref, nout_ref, mout_ref,
                  q_s, k_s, v_s, o_s, gcol_s, grow_s, bcol_s, brow_s, ct_s, m_s, hcat_s,
                  *, chunk, chunks_per_step, n_steps):
    L = chunk
    c = pl.program_id(1)
    dot = functools.partial(jnp.dot, preferred_element_type=F32)

    def wide(rep, n):
        return rep[:, :n] if n < LANES else jnp.concatenate([rep] * (n // LANES), axis=1)

    def projection(load_x, slot):
        xs = {}
        half = M_V // 2
        v0, o0 = 2 * M_QK, 2 * M_QK + M_V

        def queries():
            xn = _rms(load_x(), g_ref[...])
            xs["hi"] = xn.astype(BF16)
            xs["lo"] = (xn - xs["hi"].astype(F32)).astype(BF16)
            q_s[slot] = dot(xs["hi"], w_ref[:, :M_QK]).astype(BF16)

        def keys():
            k_s[slot] = (dot(xs["hi"], w_ref[:, M_QK:2 * M_QK]) * (M_DK ** -0.5)).astype(BF16)

        def values(part):
            def step():
                cols = slice(part * half, (part + 1) * half)
                v_s[slot, :, cols] = dot(xs["hi"], w_ref[:, v0 + part * half:v0 + (part + 1) * half]).astype(BF16)
            return step

        def out_gate(part):
            def step():
                cols = slice(part * half, (part + 1) * half)
                o_s[slot, :, cols] = dot(xs["hi"], w_ref[:, o0 + part * half:o0 + (part + 1) * half])
            return step

        def gates():
            gh = dot(xs["hi"], wg_ref[...])
            gl = dot(xs["lo"], wg_ref[:, :LANES])
            pre = gh[:, :LANES] + gh[:, LANES:] + gl + bg_ref[...]
            lane = lax.broadcasted_iota(jnp.int32, pre.shape, 1)
            gv = jnp.where(lane < M_HEADS, pre, _log_sigmoid(pre))
            gcol_s[slot] = gv
            if L < LANES:
                gv = jnp.concatenate([gv, jnp.zeros((LANES - L, LANES), F32)], axis=0)
            grow_s[slot] = gv.T[:GATE_ROWS, :L]

        def cumulate():
            bcol_s[slot] = sum(dot(tri, part) for part in _split3(gcol_s[slot]))
            brow_s[slot] = sum(dot(part, tri_t) for part in _split3(grow_s[slot]))

        return [queries, gates, cumulate, keys, values(0), values(1), out_gate(0), out_gate(1)]

    row = lax.broadcasted_iota(jnp.int32, (L, L), 0)
    col = lax.broadcasted_iota(jnp.int32, (L, L), 1)
    causal = col <= row
    tri = jnp.where(causal, 1.0, 0.0).astype(BF16)
    tri_t = jnp.where(row <= col, 1.0, 0.0).astype(BF16)
    ones = jnp.ones((L, LANES), BF16)

    def scan(slot, rows, fillers):
        fillers = list(fillers)

        def fill():
            if fillers:
                fillers.pop(0)()

        gcol = gcol_s[slot]
        grow = grow_s[slot]
        bcol = bcol_s[slot]
        brow = brow_s[slot]

        for h in range(M_HEADS):
            q = q_s[slot, :, h * M_DK:(h + 1) * M_DK]
            k = k_s[slot, :, h * M_DK:(h + 1) * M_DK]
            v = v_s[slot, :, h * M_DV:(h + 1) * M_DV]
            ig_row = grow[h:h + 1, :]
            b_row = brow[M_HEADS + h:M_HEADS + h + 1, :]
            ig = jnp.broadcast_to(gcol[:, h:h + 1], (L, LANES))
            b = jnp.broadcast_to(bcol[:, M_HEADS + h:M_HEADS + h + 1], (L, LANES))
            m0 = m_s[h:h + 1, :]
            ct = ct_s[h]

            d = jnp.where(causal, wide(b, L) - b_row + ig_row, -jnp.inf)
            g = b + m0
            m = jnp.maximum(g, jnp.max(d, axis=-1, keepdims=True))
            w_inter = jnp.exp(g - m)
            qk = lax.dot_general(q, k, (((1,), (1,)), ((), ())), preferred_element_type=F32)
            s = (jnp.exp(d - wide(m, L)) * qk).astype(BF16)
            inter = dot(q, ct.astype(BF16))
            fill()
            num = wide(w_inter, M_DV) * inter[:, :M_DV] + dot(s, v)
            den = w_inter * inter[:, M_DV:] + dot(s, ones)
            inv = 1.0 / jnp.maximum(jnp.abs(den), jnp.exp(-m))
            hh = num * wide(inv, M_DV)

            m_last = m[L - 1:L, :]
            w_state = jnp.exp(g[L - 1:L, :] - m_last)
            w_rows = jnp.exp(b[L - 1:L, :] - b + ig - m_last)
            vw = jnp.concatenate([(v.astype(F32) * wide(w_rows, M_DV)).astype(BF16), w_rows.astype(BF16)], axis=1)
            upd = lax.dot_general(k, vw, (((0,), (0,)), ((), ())), preferred_element_type=F32)
            ct_s[h] = wide(w_state, M_DV + LANES) * ct + upd
            m_s[h:h + 1, :] = m_last

            hn = _rms(hh, ghead_ref[:, h * M_DV:(h + 1) * M_DV])
            gate = jax.nn.sigmoid(o_s[slot, :, h * M_DV:(h + 1) * M_DV])
            hcat_s[slot, :, h * M_DV:(h + 1) * M_DV] = (hn * gate).astype(BF16)
            fill()
        while fillers:
            fill()

        y = dot(hcat_s[slot], wout_ref[...])
        x1_ref[0, rows, :] = x_ref[0, rows, :] + _rms(y, gpost_ref[...])

    @pl.when(c == 0)
    def _():
        for h in range(M_HEADS):
            ct_s[h, :, :M_DV] = c0_ref[0, h].T
            ct_s[h, :, M_DV:] = jnp.broadcast_to(n0_ref[0, h:h + 1, :], (LANES, M_DK)).T
        m_s[...] = m0_ref[0]
        for step in projection(lambda: x_ref[0, :L, :], 0):
            step()

    for j in range(chunks_per_step):
        rows = slice(j * L, (j + 1) * L)
        if j + 1 < chunks_per_step:
            following = projection(lambda j=j: x_ref[0, (j + 1) * L:(j + 2) * L, :], (j + 1) % 2)
        elif n_steps > 1:
            following = projection(lambda: xnext_ref[0], (j + 1) % 2)
        else:
            following = []
        scan(j % 2, rows, following)

    @pl.when(c == n_steps - 1)
    def _():
        for h in range(M_HEADS):
            t = ct_s[h].T
            cout_ref[0, h] = t[:M_DV]
            nout_ref[0, h:h + 1, :] = t[M_DV:M_DV + 1]
        mout_ref[0] = m_s[...]


def _mlstm(x2d, g_in, w_main, w_gate, b_gate, c0, n0, m0b, w_out, g_head, g_post, *, batch, chunk):
    rows = x2d.shape[0]
    seq = rows // batch
    n_chunks = seq // chunk
    cps = 2 if n_chunks % 2 == 0 else 1
    assert cps == 2 or n_chunks == 1
    n_steps = n_chunks // cps
    x3d = x2d.reshape(batch, seq, D_MODEL)
    step_spec = pl.BlockSpec((1, cps * chunk, D_MODEL), lambda b, c: (b, c, 0))
    next_spec = pl.BlockSpec((1, chunk, D_MODEL), lambda b, c: (b, jnp.minimum(cps * (c + 1), n_chunks - 1), 0))
    state4 = pl.BlockSpec((1, M_HEADS, M_DV, M_DK), lambda b, c: (b, 0, 0, 0))
    state3 = pl.BlockSpec((1, M_HEADS, LANES), lambda b, c: (b, 0, 0))
    vec = _const_spec((1, D_MODEL))
    x1, c_new, n_new, m_new = pl.pallas_call(
        functools.partial(_mlstm_kernel, chunk=chunk, chunks_per_step=cps, n_steps=n_steps),
        grid=(batch, n_steps),
        in_specs=[step_spec, next_spec, vec, _const_spec(w_main.shape), _const_spec(w_gate.shape),
                  _const_spec((1, LANES)), state4, state3, state3,
                  _const_spec((M_V, D_MODEL)), _const_spec((1, M_V)), vec],
        out_specs=[step_spec, state4, state3, state3],
        out_shape=[jax.ShapeDtypeStruct((batch, seq, D_MODEL), F32),
                   jax.ShapeDtypeStruct((batch, M_HEADS, M_DV, M_DK), F32),
                   jax.ShapeDtypeStruct((batch, M_HEADS, LANES), F32),
                   jax.ShapeDtypeStruct((batch, M_HEADS, LANES), F32)],
        scratch_shapes=[pltpu.VMEM((2, chunk, M_QK), BF16), pltpu.VMEM((2, chunk, M_QK), BF16),
                        pltpu.VMEM((2, chunk, M_V), BF16), pltpu.VMEM((2, chunk, D_MODEL), F32),
                        pltpu.VMEM((2, chunk, LANES), F32), pltpu.VMEM((2, GATE_ROWS, chunk), F32),
                        pltpu.VMEM((2, chunk, LANES), F32), pltpu.VMEM((2, GATE_ROWS, chunk), F32),
                        pltpu.VMEM((M_HEADS, M_DK, M_DV + LANES), F32), pltpu.VMEM((M_HEADS, LANES), F32),
                        pltpu.VMEM((2, chunk, M_V), BF16)],
        compiler_params=_params(2),
        name="mlstm",
    )(x3d, x3d, g_in, w_main, w_gate, b_gate, c0, n0, m0b, w_out, g_head, g_post)
    return x1.reshape(rows, D_MODEL), c_new, n_new, m_new


def _ffn_tile(load_x, gin_ref, win_ref, wout_ref, gout_ref, y_ref):
    dot = functools.partial(jnp.dot, preferred_element_type=F32)
    rows = y_ref.shape[0]
    n = rows // FFN_GROUP_ROWS if rows % FFN_GROUP_ROWS == 0 else 1
    groups = [slice(r * rows // n, (r + 1) * rows // n) for r in range(n)]

    def prologue(r):
        x = load_x(groups[r])
        return x, _rms(x, gin_ref[...]).astype(BF16)

    def epilogue(r, x, acc):
        y_ref[groups[r], :] = x + _rms(acc, gout_ref[...])

    x, xn = prologue(0)
    pending = None
    for r in range(n):
        acc = None
        start = 0
        nxt = None
        for j, width in enumerate(FFN_COL_CHUNKS):
            gate = dot(xn, win_ref[:, start:start + width])
            up = dot(xn, win_ref[:, FFN_HIDDEN + start:FFN_HIDDEN + start + width])
            act = (jax.nn.silu(gate) * up).astype(BF16)
            part = dot(act, wout_ref[start:start + width, :])
            acc = part if acc is None else acc + part
            start += width
            if j == 0:
                if pending is not None:
                    epilogue(*pending)
                if r + 1 < n:
                    nxt = prologue(r + 1)
        pending = (r, x, acc)
        if nxt is not None:
            x, xn = nxt
    epilogue(*pending)


def _ffn_kernel(x_ref, gin_ref, win_ref, wout_ref, gout_ref, y_ref):
    _ffn_tile(lambda rows: x_ref[rows, :], gin_ref, win_ref, wout_ref, gout_ref, y_ref)


def _attn_out_ffn_kernel(x_ref, a_ref, wo_ref, gpre_ref, gin_ref, win_ref, wout_ref, gout_ref, y_ref):
    def load_x(rows):
        mix = jnp.dot(a_ref[rows, :], wo_ref[...], preferred_element_type=F32)
        return x_ref[rows, :] + _rms(mix, gpre_ref[...])

    _ffn_tile(load_x, gin_ref, win_ref, wout_ref, gout_ref, y_ref)


def _ffn(x2d, g_in, w_in, w_out, g_out, *, layer, tile, attn=None, w_o=None, g_pre=None):
    rows = x2d.shape[0]
    row_spec = lambda cols: pl.BlockSpec((tile, cols), lambda i: (i, 0))
    vec = _const_spec((1, D_MODEL))
    layer_spec = lambda w: pl.BlockSpec((None,) + w.shape[1:], lambda i: (layer, 0, 0),
                                        pipeline_mode=pl.Buffered(1))
    ffn_specs = [vec, layer_spec(w_in), layer_spec(w_out), vec]
    if attn is None:
        kern, in_specs = _ffn_kernel, [row_spec(D_MODEL)] + ffn_specs
        args = (x2d, g_in, w_in, w_out, g_out)
    else:
        kern = _attn_out_ffn_kernel
        in_specs = [row_spec(D_MODEL), row_spec(DA_Q), _const_spec(w_o.shape), vec] + ffn_specs
        args = (x2d, attn, w_o, g_pre, g_in, w_in, w_out, g_out)
    return pl.pallas_call(
        kern, grid=(rows // tile,), in_specs=in_specs, out_specs=row_spec(D_MODEL),
        out_shape=jax.ShapeDtypeStruct((rows, D_MODEL), F32),
        compiler_params=_params(1), name="ffn",
    )(*args)


def _qkv_kernel(x_ref, gq_ref, gkv_ref, wq_ref, wkv_ref, q_ref, kf_ref, vf_ref, kb_ref, vb_ref, *, k_transposed):
    dot = functools.partial(jnp.dot, preferred_element_type=F32)
    rows = x_ref.shape[0]
    n = rows // FFN_GROUP_ROWS if rows % FFN_GROUP_ROWS == 0 else 1
    groups = [slice(r * rows // n, (r + 1) * rows // n) for r in range(n)]

    def normed(g):
        x = x_ref[g, :]
        xhat = x * lax.rsqrt(jnp.mean(x * x, axis=-1, keepdims=True) + EPS)
        return (xhat * gq_ref[...]).astype(BF16), (xhat * gkv_ref[...]).astype(BF16)

    cur = normed(groups[0])
    for r, g in enumerate(groups):
        xq, xkv = cur
        q_ref[g, :] = (dot(xq, wq_ref[...]) * (DA_DH ** -0.5 * LOG2E)).astype(BF16)
        if r + 1 < n:
            cur = normed(groups[r + 1])
        kk = dot(xkv, wkv_ref[:, :DA_Q])
        if k_transposed:
            kt = kk.T
            kf_ref[0, :, g] = kt
            kb_ref[0, :, g] = kt.astype(BF16)
        else:
            kf_ref[g, :] = kk
            kb_ref[g, :] = kk.astype(BF16)
        vv = dot(xkv, wkv_ref[:, DA_Q:])
        vf_ref[g, :] = vv
        vb_ref[g, :] = vv.astype(BF16)


def _qkv(x2d, g_q, g_kv, w_q, w_kv, *, tile, batch, k_transposed):
    rows = x2d.shape[0]
    seq = rows // batch
    row_spec = pl.BlockSpec((tile, D_MODEL), lambda i: (i, 0))
    vec = _const_spec((1, D_MODEL))
    shp = lambda dt: jax.ShapeDtypeStruct((rows, DA_Q), dt)
    if k_transposed:
        tiles_per_seq = seq // tile
        k_spec = pl.BlockSpec((1, DA_Q, tile), lambda i: (i // tiles_per_seq, 0, i % tiles_per_seq))
        k_shp = lambda dt: jax.ShapeDtypeStruct((batch, DA_Q, seq), dt)
    else:
        k_spec, k_shp = row_spec, shp
    return pl.pallas_call(
        functools.partial(_qkv_kernel, k_transposed=k_transposed), grid=(rows // tile,),
        in_specs=[row_spec, vec, vec, _const_spec(w_q.shape), _const_spec(w_kv.shape)],
        out_specs=[row_spec, k_spec, row_spec, k_spec, row_spec],
        out_shape=[shp(BF16), k_shp(F32), shp(F32), k_shp(BF16), shp(BF16)],
        compiler_params=_params(1), name="attn_qkv",
    )(x2d, g_q, g_kv, w_q, w_kv)


def _rel_bias_tile(rb_ref, head, q_pos, k_pos):
    rel = k_pos - q_pos
    n = jnp.abs(rel)
    half = N_BUCKETS // 2

    def table(offset):
        val = jnp.full(rel.shape, rb_ref[offset + half - 1, head], F32)
        for bucket in reversed(range(half - 1)):
            val = jnp.where(n < _BUCKET_BOUNDS[bucket], rb_ref[offset + bucket, head], val)
        return val

    bias = jnp.where(rel > 0, table(half), table(0)) * LOG2E
    shift = ATTN_CHUNK.bit_length() - 1
    visible = jnp.right_shift(k_pos, shift) <= jnp.right_shift(q_pos, shift)
    return jnp.where(visible, bias, -jnp.inf)


def _lambda(l1q_ref, l1k_ref, l2q_ref, l2k_ref, lam_init):
    a = jnp.exp(jnp.sum(l1q_ref[...] * l1k_ref[...], axis=-1, keepdims=True))
    b = jnp.exp(jnp.sum(l2q_ref[...] * l2k_ref[...], axis=-1, keepdims=True))
    return a - b + lam_init


def _stack_maps(q):
    lane = lax.broadcasted_iota(jnp.int32, q.shape, 1)
    zero = jnp.zeros_like(q)
    return jnp.concatenate([jnp.where(lane < DA_DH, q, zero), jnp.where(lane >= DA_DH, q, zero)], axis=0)


def _diff_finish(acc, l, lam, g, lam_init):
    t = acc.shape[0] // 2
    o = acc[:t] / l[:t] - lam * (acc[t:] / l[t:])
    return _rms(o, g) * (1.0 - lam_init)


def _largest_divisor(n, candidates):
    return max(u for u in candidates if n % u == 0)


def _prompt_attn_kernel(rb_ref, q_ref, k_ref, v_ref, l1q_ref, l1k_ref, l2q_ref, l2k_ref, g_ref,
                        o_ref, bias_s, m_s, acc_s, snext_s, *, tile, n_tiles, lam_init):
    T = tile
    hd = 2 * DA_DH
    h = pl.program_id(0)
    b = pl.program_id(1)
    dot = functools.partial(jnp.dot, preferred_element_type=F32)

    @pl.when(b == 0)
    def _():
        qp = lax.broadcasted_iota(jnp.int32, (T, 2 * T), 0) + T
        kp = lax.broadcasted_iota(jnp.int32, (T, 2 * T), 1)
        near = _rel_bias_tile(rb_ref, h, qp, kp)
        bias_s[...] = jnp.concatenate([near, near], axis=0)

    far_bias = rb_ref[N_BUCKETS // 2 - 1, h] * LOG2E
    lane = lax.broadcasted_iota(jnp.int32, (T, hd), 1)

    def update(i, key_block, n_blocks, bias):
        apply(i, scores(i, key_block, n_blocks), key_block, n_blocks, bias)

    def scores(i, key_block, n_blocks):
        q = q_ref[pl.ds(pl.multiple_of(i * T, T), T), :]
        kb = k_ref[0, :, pl.ds(pl.multiple_of(key_block * T, T), n_blocks * T)]
        zero = jnp.zeros_like(q)
        return jnp.concatenate([dot(jnp.where(lane < DA_DH, q, zero), kb),
                                dot(jnp.where(lane >= DA_DH, q, zero), kb)], axis=0)

    def apply(i, s, key_block, n_blocks, bias):
        vb = v_ref[pl.ds(pl.multiple_of(key_block * T, T), n_blocks * T), :]
        vext = jnp.concatenate([vb, jnp.ones_like(vb)], axis=1)
        first = bias is not None
        if first:
            s = s + bias
            m_new = jnp.broadcast_to(jnp.max(s, axis=-1, keepdims=True), (2 * T, LANES))
        else:
            m_prev = m_s[i]
            m_new = jnp.maximum(m_prev, jnp.max(s, axis=-1, keepdims=True))
        p = jnp.exp2(s - jnp.concatenate([m_new] * (n_blocks * T // LANES), axis=1))
        pv = jnp.dot(p.astype(BF16), vext, preferred_element_type=F32)
        if first:
            acc_s[i] = pv
            m_s[i] = m_new - far_bias
        else:
            alpha = jnp.exp2(m_prev - m_new)
            acc_s[i] = jnp.concatenate([alpha, alpha], axis=1) * acc_s[i] + pv
            m_s[i] = m_new

    update(0, 0, 1, bias_s[:, T:])
    near_unroll = _largest_divisor(n_tiles - 1, (15, 5, 3, 2, 1))

    last_tile = n_tiles - 1
    if n_tiles > 1:
        snext_s[:, :2 * T] = scores(1, 0, 2)

    def near_body(it, carry):
        s = snext_s[:, :2 * T]
        for u in range(near_unroll):
            i = 1 + it * near_unroll + u
            i_next = jnp.minimum(i + 1, last_tile)
            s_next = scores(i_next, i_next - 1, 2)
            apply(i, s, i - 1, 2, bias_s[...])
            s = s_next
        snext_s[:, :2 * T] = s
        return carry

    lax.fori_loop(0, (n_tiles - 1) // near_unroll, near_body, 0)

    rest = []
    for i in range(2, n_tiles):
        n_far = i - 1
        if n_far % 2:
            rest.append((i, n_far - 1, 1))
        if n_far % FAR_GROUP >= 2:
            rest.append((i, n_far - n_far % FAR_GROUP, 2))
    s = scores(*rest[0]) if rest else None
    for idx, (i, key_block, n_blocks) in enumerate(rest):
        s_next = scores(*rest[idx + 1]) if idx + 1 < len(rest) else None
        apply(i, s, key_block, n_blocks, None)
        s = s_next
    n_group = sum((i - 1) // FAR_GROUP for i in range(1, n_tiles))
    far_unroll = _largest_divisor(n_group, (7, 5, 4, 3, 2, 1))
    shift = FAR_GROUP.bit_length() - 1

    def far_body(_, carry):
        i, t = carry
        s = snext_s[...]
        for _u in range(far_unroll):
            wrap = t + 1 >= jnp.right_shift(i - 1, shift)
            i_next, t_next = jnp.where(wrap, i + 1, i), jnp.where(wrap, 0, t + 1)
            t_next = jnp.where(i_next > last_tile, 0, t_next)
            i_next = jnp.minimum(i_next, last_tile)
            s_next = scores(i_next, FAR_GROUP * t_next, FAR_GROUP)
            apply(i, s, FAR_GROUP * t, FAR_GROUP, None)
            s, i, t = s_next, i_next, t_next
        snext_s[...] = s
        return i, t

    if n_group:
        snext_s[...] = scores(FAR_GROUP + 1, 0, FAR_GROUP)
        lax.fori_loop(0, n_group // far_unroll, far_body, (jnp.int32(FAR_GROUP + 1), jnp.int32(0)))

    lam = _lambda(l1q_ref, l1k_ref, l2q_ref, l2k_ref, lam_init)
    for i in range(n_tiles):
        acc = acc_s[i]
        o_ref[i * T:(i + 1) * T, :] = _diff_finish(acc[:, :hd], acc[:, hd:], lam, g_ref[...], lam_init).astype(BF16)


def _prompt_attn(rel_bias, q, k, v, lams, g_head, *, batch, seq, tile, lam_init):
    assert tile >= MAX_DISTANCE and tile % ATTN_CHUNK == 0 and seq % tile == 0
    nq = seq // tile
    hd = 2 * DA_DH
    lam_spec = _const_spec((1, DA_DH))
    seq_spec = pl.BlockSpec((seq, hd), lambda h, b: (b, h))
    feat_spec = pl.BlockSpec((1, hd, seq), lambda h, b: (b, h, 0))
    return pl.pallas_call(
        functools.partial(_prompt_attn_kernel, tile=tile, n_tiles=nq, lam_init=lam_init),
        grid=(DA_HEADS, batch),
        in_specs=[pl.BlockSpec(memory_space=pltpu.SMEM), seq_spec,
                  feat_spec, seq_spec,
                  lam_spec, lam_spec, lam_spec, lam_spec, _const_spec((1, hd))],
        out_specs=seq_spec,
        out_shape=jax.ShapeDtypeStruct((batch * seq, DA_Q), BF16),
        scratch_shapes=[pltpu.VMEM((2 * tile, 2 * tile), F32),
                        pltpu.VMEM((nq, 2 * tile, LANES), F32), pltpu.VMEM((nq, 2 * tile, 2 * hd), F32),
                        pltpu.VMEM((2 * tile, FAR_GROUP * tile), F32)],
        compiler_params=_params(2), name="diff_attn_prompt",
    )(rel_bias, q, k, v, *lams, g_head)


def _sample_attn_kernel(rb_ref, q_ref, kp_ref, vp_ref, kn_ref, vn_ref, l1q_ref, l1k_ref, l2q_ref, l2k_ref,
                        g_ref, o_ref, bias_p, bias_n, *, past, seq, lam_init):
    b = pl.program_id(0)
    hd = 2 * DA_DH
    dot = functools.partial(jnp.dot, preferred_element_type=F32)
    nt = lambda a, bb: lax.dot_general(a, bb, (((1,), (1,)), ((), ())), preferred_element_type=F32)

    @pl.when(b == 0)
    def _():
        qp = lax.broadcasted_iota(jnp.int32, (seq, past), 0) + past
        kp = lax.broadcasted_iota(jnp.int32, (seq, past), 1)
        qn = lax.broadcasted_iota(jnp.int32, (seq, seq), 0) + past
        kn = lax.broadcasted_iota(jnp.int32, (seq, seq), 1) + past
        for h in range(DA_HEADS):
            tile = _rel_bias_tile(rb_ref, h, qp, kp)
            bias_p[h] = jnp.concatenate([tile, tile], axis=0)
            tile = _rel_bias_tile(rb_ref, h, qn, kn)
            bias_n[h] = jnp.concatenate([tile, tile], axis=0)

    lam = _lambda(l1q_ref, l1k_ref, l2q_ref, l2k_ref, lam_init)
    v_heads = jnp.transpose(vp_ref[0], (1, 0, 2))
    for h in range(DA_HEADS):
        cols = slice(h * hd, (h + 1) * hd)
        qs = _stack_maps(q_ref[:, cols])
        s_p = dot(qs, kp_ref[0, cols, :].astype(BF16)) + bias_p[h]
        s_n = nt(qs, kn_ref[:, cols]) + bias_n[h]
        m = jnp.maximum(jnp.max(s_p, axis=-1, keepdims=True), jnp.max(s_n, axis=-1, keepdims=True))
        p_p = jnp.exp2(s_p - m)
        p_n = jnp.exp2(s_n - m)
        l = jnp.sum(p_p, axis=-1, keepdims=True) + jnp.sum(p_n, axis=-1, keepdims=True)
        acc = (dot(p_p.astype(BF16), v_heads[h].astype(BF16))
               + dot(p_n.astype(BF16), vn_ref[:, cols]))
        o_ref[:, cols] = _diff_finish(acc, l, lam, g_ref[...], lam_init).astype(BF16)


def _sample_attn(rel_bias, q, cache_kt, cache_v, k_new, v_new, lams, g_head, *, batch, seq, lam_init):
    past = cache_kt.shape[2]
    hd = 2 * DA_DH
    lam_spec = _const_spec((1, DA_DH))
    new_spec = pl.BlockSpec((seq, DA_Q), lambda b: (b, 0))
    return pl.pallas_call(
        functools.partial(_sample_attn_kernel, past=past, seq=seq, lam_init=lam_init),
        grid=(batch,),
        in_specs=[pl.BlockSpec(memory_space=pltpu.SMEM), new_spec,
                  pl.BlockSpec((1, DA_Q, past), lambda b: (b, 0, 0)),
                  pl.BlockSpec((1, past, DA_HEADS, hd), lambda b: (b, 0, 0, 0)),
                  new_spec, new_spec, lam_spec, lam_spec, lam_spec, lam_spec, _const_spec((1, hd))],
        out_specs=new_spec,
        out_shape=jax.ShapeDtypeStruct((batch * seq, DA_Q), BF16),
        scratch_shapes=[pltpu.VMEM((DA_HEADS, 2 * seq, past), F32), pltpu.VMEM((DA_HEADS, 2 * seq, seq), F32)],
        compiler_params=_params(1), name="diff_attn_sample",
    )(rel_bias, q, cache_kt, cache_v, k_new, v_new, *lams, g_head)


def _prep_weights(norm_g, mlstm_w_in, mlstm_b_gate, mlstm_g_head, mlstm_w_out, kv_g, kv_w, diff_w_q,
                  diff_lam_q1, diff_lam_k1, diff_lam_q2, diff_lam_k2, diff_g_head, diff_w_o,
                  ffn_w_in, ffn_w_out):
    n_main = 2 * M_QK + M_V + D_MODEL
    w_in = mlstm_w_in[0]
    wg = jnp.pad(w_in[:, n_main:], ((0, 0), (0, LANES - 2 * M_HEADS)))
    wg_hi = wg.astype(BF16)
    wg_lo = (wg - wg_hi.astype(F32)).astype(BF16)
    row = lambda a: a.reshape(1, -1)
    return dict(
        norm=[[row(norm_g[l, j]) for j in range(4)] for l in range(2)],
        w_main=w_in.astype(BF16),
        w_gate=jnp.concatenate([wg_hi, wg_lo], axis=1),
        b_gate=jnp.pad(mlstm_b_gate[0], (0, LANES - 2 * M_HEADS)).reshape(1, LANES),
        g_mhead=row(mlstm_g_head[0]),
        w_mout=mlstm_w_out[0].astype(BF16),
        kv_g=row(kv_g), kv_w=kv_w.astype(BF16), w_q=diff_w_q[0].astype(BF16),
        lams=[row(diff_lam_q1[0]), row(diff_lam_k1[0]), row(diff_lam_q2[0]), row(diff_lam_k2[0])],
        g_dhead=row(diff_g_head[0]), w_o=diff_w_o[0].astype(BF16),
        ffn_in=ffn_w_in.astype(BF16), ffn_out=ffn_w_out.astype(BF16),
    )


def _trunk(x, c0, n0, m0, past_k, past_v, rel_bias, w, *, row_tile, ffn_tile, chunk, attn_tile):
    batch, seq, _ = x.shape
    rows = batch * seq
    x2d = x.reshape(rows, D_MODEL)
    lam_init = 0.8 - 0.6 * math.exp(-0.3 * 1)

    m0b = jnp.broadcast_to(m0[..., None], (batch, M_HEADS, LANES))
    x1, c_new, n_new, m_new = _mlstm(x2d, w["norm"][0][0], w["w_main"], w["w_gate"], w["b_gate"], c0, n0, m0b,
                                     w["w_mout"], w["g_mhead"], w["norm"][0][1], batch=batch, chunk=chunk)
    x2 = _ffn(x1, w["norm"][0][2], w["ffn_in"], w["ffn_out"], w["norm"][0][3], layer=0, tile=ffn_tile)

    aq, k_f32, v_f32, k_bf, v_bf = _qkv(x2, w["norm"][1][0], w["kv_g"], w["w_q"], w["kv_w"], tile=ffn_tile,
                                        batch=batch, k_transposed=past_k is None)
    if past_k is None:
        attn = _prompt_attn(rel_bias, aq, k_bf, v_bf, w["lams"], w["g_dhead"],
                            batch=batch, seq=seq, tile=attn_tile, lam_init=lam_init)
        k_f32 = k_f32.reshape(batch, DA_HEADS, 2, DA_DH, seq).transpose(0, 4, 1, 2, 3)
    else:
        past = past_k.shape[1]
        past_kt = past_k.transpose(0, 2, 3, 4, 1).reshape(batch, DA_Q, past)
        attn = _sample_attn(rel_bias, aq, past_kt, past_v, k_bf, v_bf, w["lams"], w["g_dhead"],
                            batch=batch, seq=seq, lam_init=lam_init)
    y = _ffn(x2, w["norm"][1][2], w["ffn_in"], w["ffn_out"], w["norm"][1][3], layer=1, tile=ffn_tile,
             attn=attn, w_o=w["w_o"], g_pre=w["norm"][1][1])

    return (y.reshape(batch, seq, D_MODEL), c_new[None], n_new[None], m_new[None, :, :, 0],
            k_f32.reshape(batch, seq, DA_HEADS, 2, DA_DH), v_f32.reshape(batch, seq, DA_HEADS, 2 * DA_DH))


def kernel(x_prompt, x_sample, state_C, state_n, state_m, cache_k, cache_v, norm_g, mlstm_w_in, mlstm_b_gate, mlstm_g_head, mlstm_w_out, kv_g, kv_w, rel_bias, diff_w_q, diff_lam_q1, diff_lam_k1, diff_lam_q2, diff_lam_k2, diff_g_head, diff_w_o, ffn_w_in, ffn_w_out):
    w = _prep_weights(norm_g, mlstm_w_in, mlstm_b_gate, mlstm_g_head, mlstm_w_out, kv_g, kv_w, diff_w_q,
                      diff_lam_q1, diff_lam_k1, diff_lam_q2, diff_lam_k2, diff_g_head, diff_w_o,
                      ffn_w_in, ffn_w_out)
    pb = x_prompt.shape[0]
    zeros = lambda *s: jnp.zeros(s, F32)
    y_p, p_c, p_n, p_m, p_k, p_v = _trunk(
        x_prompt, zeros(pb, M_HEADS, M_DV, M_DK), zeros(pb, M_HEADS, M_DK), zeros(pb, M_HEADS),
        None, None, rel_bias, w, row_tile=512, ffn_tile=1024, chunk=256, attn_tile=256)
    sb, ss, _ = x_sample.shape
    y_s, s_c, s_n, s_m, s_k, s_v = _trunk(
        x_sample, state_C[0], state_n[0], state_m[0], cache_k, cache_v, rel_bias, w,
        row_tile=sb * ss, ffn_tile=sb * ss, chunk=ss, attn_tile=None)
    return (y_p, y_s, p_c, p_n, p_m, p_k, p_v, s_c, s_n, s_m, s_k, s_v)
```
